```python
import jax, jax.numpy as jnp
from jax import lax
import numpy as np

D_MODEL = 1024
BATCH = 8
SEQ = 16384
DEPTH = 4

N_MIXERS = 3
N_CONV_LAYERS = (DEPTH + 2) // 3
N_SGU_LAYERS = (DEPTH + 1) // 3
N_POOL_LAYERS = DEPTH // 3

CONV_WIDTH = 31
CHUNK = 128
SGU_HEADS = 8
SGU_HEAD_DIM = D_MODEL // SGU_HEADS
POOL_WINDOWS = (2, 4, 8, 16)
POOL_GROUPS = len(POOL_WINDOWS)
POOL_GROUP_DIM = D_MODEL // POOL_GROUPS
D_FF = 2816
FFN_CONV_WIDTH = 3
DEEPNORM_ALPHA = float((2 * DEPTH) ** 0.25)
DEEPNORM_BETA = float((8 * DEPTH) ** -0.25)
LN_EPS = 1e-5

kernel_name = "interleaved_conv_sgu_pool_deepnorm_trunk"


def layer_norm(x, g, b):
    xf = x.astype(jnp.float32)
    mu = jnp.mean(xf, axis=-1, keepdims=True)
    var = jnp.mean(jnp.square(xf - mu), axis=-1, keepdims=True)
    y = (xf - mu) * lax.rsqrt(var + LN_EPS) * g.astype(jnp.float32) + b.astype(jnp.float32)
    return y.astype(x.dtype)


def causal_depthwise_conv(x, w):
    k, c = w.shape
    return lax.conv_general_dilated(
        x, w[:, None, :].astype(x.dtype), window_strides=(1,), padding=[(k - 1, 0)],
        dimension_numbers=("NWC", "WIO", "NWC"), feature_group_count=c)


def conformer_conv(x, w_in, dw, dw_b, ln_g, ln_b, w_out):
    h = x @ w_in
    a, gate = jnp.split(h, 2, axis=-1)
    h = a * jax.nn.sigmoid(gate)
    h = causal_depthwise_conv(h, dw) + dw_b
    h = jax.nn.silu(layer_norm(h, ln_g, ln_b))
    return h @ w_out


def chunked_sgu(x, w_in, ln_g, ln_b, ws, bs, w_out):
    bsz, seq, _ = x.shape
    z = jax.nn.gelu(x @ w_in, approximate=False)
    u, v = jnp.split(z, 2, axis=-1)
    v = layer_norm(v, ln_g, ln_b)
    v = v.reshape(bsz, seq // CHUNK, CHUNK, SGU_HEADS, SGU_HEAD_DIM)
    mask = jnp.tril(jnp.ones((CHUNK, CHUNK), dtype=ws.dtype))
    s = jnp.einsum("hts,bnshc->bnthc", ws * mask, v)
    s = s + jnp.transpose(bs)[None, None, :, :, None]
    s = s.reshape(bsz, seq, D_MODEL)
    return (u * s) @ w_out


def multiscale_pool(x, w_in, w_grp, scale, w_out):
    bsz, seq, _ = x.shape
    y = x @ w_in
    yf = y.astype(jnp.float32)
    cs = jnp.concatenate([jnp.zeros((bsz, 1, D_MODEL), jnp.float32),
                          lax.cumsum(yf, axis=1)], axis=1)
    pos = jnp.arange(seq)
    groups = []
    for g, w in enumerate(POOL_WINDOWS):
        sl = slice(g * POOL_GROUP_DIM, (g + 1) * POOL_GROUP_DIM)
        c = cs[..., sl]
        upper = c[:, 1:]
        lower = jnp.pad(c[:, :seq - w + 1], ((0, 0), (w - 1, 0), (0, 0)))
        count = jnp.minimum(pos + 1, w).astype(jnp.float32)[None, :, None]
        groups.append((upper - lower) / count - yf[..., sl])
    p = jnp.stack(groups, axis=2).astype(y.dtype)
    z = jnp.einsum("bsgc,gcd->bsgd", p, w_grp).reshape(bsz, seq, D_MODEL) * scale
    return z @ w_out


def conv_ffn(x, w_up, dw, w_down):
    h = causal_depthwise_conv(x @ w_up, dw)
    g, v = jnp.split(h, 2, axis=-1)
    return (jax.nn.silu(g) * v) @ w_down


def _fwd_setup_inputs(seed: int = 0) -> dict:
    key = jax.random.key(seed)
    ks = iter(jax.random.split(key, 32))

    def nrm(shape, scale):
        return jax.random.normal(next(ks), shape, jnp.float32) * scale

    d = D_MODEL
    return {
        "x": nrm((BATCH, SEQ, d), 1.0),
        "a_w_in": nrm((N_CONV_LAYERS, d, 2 * d), d ** -0.5),
        "a_dw": nrm((N_CONV_LAYERS, CONV_WIDTH, d), CONV_WIDTH ** -0.5),
        "a_dw_b": nrm((N_CONV_LAYERS, d), 0.02),
        "a_ln_g": 1.0 + nrm((N_CONV_LAYERS, d), 0.02),
        "a_ln_b": nrm((N_CONV_LAYERS, d), 0.02),
        "a_w_out": nrm((N_CONV_LAYERS, d, d), d ** -0.5 * DEEPNORM_BETA),
        "b_w_in": nrm((N_SGU_LAYERS, d, 2 * d), d ** -0.5),
        "b_ln_g": 1.0 + nrm((N_SGU_LAYERS, d), 0.02),
        "b_ln_b": nrm((N_SGU_LAYERS, d), 0.02),
        "b_ws": nrm((N_SGU_LAYERS, SGU_HEADS, CHUNK, CHUNK), CHUNK ** -0.5),
        "b_bs": 1.0 + nrm((N_SGU_LAYERS, SGU_HEADS, CHUNK), 0.01),
        "b_w_out": nrm((N_SGU_LAYERS, d, d), d ** -0.5 * DEEPNORM_BETA),
        "c_w_in": nrm((N_POOL_LAYERS, d, d), d ** -0.5),
        "c_w_grp": nrm((N_POOL_LAYERS, POOL_GROUPS, POOL_GROUP_DIM, POOL_GROUP_DIM), POOL_GROUP_DIM ** -0.5),
        "c_scale": 1.0 + nrm((N_POOL_LAYERS, d), 0.1),
        "c_w_out": nrm((N_POOL_LAYERS, d, d), d ** -0.5 * DEEPNORM_BETA),
        "f_w_up": nrm((DEPTH, d, 2 * D_FF), d ** -0.5),
        "f_dw": nrm((DEPTH, FFN_CONV_WIDTH, 2 * D_FF), FFN_CONV_WIDTH ** -0.5),
        "f_w_down": nrm((DEPTH, D_FF, d), D_FF ** -0.5 * DEEPNORM_BETA),
        "ln1_g": 1.0 + nrm((DEPTH, d), 0.02),
        "ln1_b": nrm((DEPTH, d), 0.02),
        "ln2_g": 1.0 + nrm((DEPTH, d), 0.02),
        "ln2_b": nrm((DEPTH, d), 0.02),
    }


def _fwd_reference(x, a_w_in, a_dw, a_dw_b, a_ln_g, a_ln_b, a_w_out,
              b_w_in, b_ln_g, b_ln_b, b_ws, b_bs, b_w_out,
              c_w_in, c_w_grp, c_scale, c_w_out,
              f_w_up, f_dw, f_w_down,
              ln1_g, ln1_b, ln2_g, ln2_b):
    for i in range(DEPTH):
        kind, j = i % N_MIXERS, i // N_MIXERS
        if kind == 0:
            h = conformer_conv(x, a_w_in[j], a_dw[j], a_dw_b[j], a_ln_g[j], a_ln_b[j], a_w_out[j])
        elif kind == 1:
            h = chunked_sgu(x, b_w_in[j], b_ln_g[j], b_ln_b[j], b_ws[j], b_bs[j], b_w_out[j])
        else:
            h = multiscale_pool(x, c_w_in[j], c_w_grp[j], c_scale[j], c_w_out[j])
        x = layer_norm(DEEPNORM_ALPHA * x + h, ln1_g[i], ln1_b[i])
        x = layer_norm(DEEPNORM_ALPHA * x + conv_ffn(x, f_w_up[i], f_dw[i], f_w_down[i]), ln2_g[i], ln2_b[i])
    return x


import jax as _jax
import jax.numpy as _jnp

TWIN_FORMAT = 'train_step'
FWD_PARAMS = ['x', 'a_w_in', 'a_dw', 'a_dw_b', 'a_ln_g', 'a_ln_b', 'a_w_out', 'b_w_in', 'b_ln_g', 'b_ln_b', 'b_ws', 'b_bs', 'b_w_out', 'c_w_in', 'c_w_grp', 'c_scale', 'c_w_out', 'f_w_up', 'f_dw', 'f_w_down', 'ln1_g', 'ln1_b', 'ln2_g', 'ln2_b']
TWIN_WEIGHTS = ['a_w_in', 'a_dw', 'a_dw_b', 'a_ln_g', 'a_ln_b', 'a_w_out', 'b_w_in', 'b_ln_g', 'b_ln_b', 'b_ws', 'b_bs', 'b_w_out', 'c_w_in', 'c_w_grp', 'c_scale', 'c_w_out', 'f_w_up', 'f_dw', 'f_w_down', 'ln1_g', 'ln1_b', 'ln2_g', 'ln2_b']
TWIN_DIFF_INPUT = 'x'
TWIN_INPUTS = ['x', 'a_w_in', 'a_dw', 'a_dw_b', 'a_ln_g', 'a_ln_b', 'a_w_out', 'b_w_in', 'b_ln_g', 'b_ln_b', 'b_ws', 'b_bs', 'b_w_out', 'c_w_in', 'c_w_grp', 'c_scale', 'c_w_out', 'f_w_up', 'f_dw', 'f_w_down', 'ln1_g', 'ln1_b', 'ln2_g', 'ln2_b', 'loss_target', 'm_a_w_in', 'm_a_dw', 'm_a_dw_b', 'm_a_ln_g', 'm_a_ln_b', 'm_a_w_out', 'm_b_w_in', 'm_b_ln_g', 'm_b_ln_b', 'm_b_ws', 'm_b_bs', 'm_b_w_out', 'm_c_w_in', 'm_c_w_grp', 'm_c_scale', 'm_c_w_out', 'm_f_w_up', 'm_f_dw', 'm_f_w_down', 'm_ln1_g', 'm_ln1_b', 'm_ln2_g', 'm_ln2_b', 'v_a_w_in', 'v_a_dw', 'v_a_dw_b', 'v_a_ln_g', 'v_a_ln_b', 'v_a_w_out', 'v_b_w_in', 'v_b_ln_g', 'v_b_ln_b', 'v_b_ws', 'v_b_bs', 'v_b_w_out', 'v_c_w_in', 'v_c_w_grp', 'v_c_scale', 'v_c_w_out', 'v_f_w_up', 'v_f_dw', 'v_f_w_down', 'v_ln1_g', 'v_ln1_b', 'v_ln2_g', 'v_ln2_b']
TWIN_OUTPUTS = ['loss', 'grad_x', 'grad_a_w_in', 'grad_a_dw', 'grad_a_dw_b', 'grad_a_ln_g', 'grad_a_ln_b', 'grad_a_w_out', 'grad_b_w_in', 'grad_b_ln_g', 'grad_b_ln_b', 'grad_b_ws', 'grad_b_bs', 'grad_b_w_out', 'grad_c_w_in', 'grad_c_w_grp', 'grad_c_scale', 'grad_c_w_out', 'grad_f_w_up', 'grad_f_dw', 'grad_f_w_down', 'grad_ln1_g', 'grad_ln1_b', 'grad_ln2_g', 'grad_ln2_b', 'delta_a_w_in', 'delta_a_dw', 'delta_a_dw_b', 'delta_a_ln_g', 'delta_a_ln_b', 'delta_a_w_out', 'delta_b_w_in', 'delta_b_ln_g', 'delta_b_ln_b', 'delta_b_ws', 'delta_b_bs', 'delta_b_w_out', 'delta_c_w_in', 'delta_c_w_grp', 'delta_c_scale', 'delta_c_w_out', 'delta_f_w_up', 'delta_f_dw', 'delta_f_w_down', 'delta_ln1_g', 'delta_ln1_b', 'delta_ln2_g', 'delta_ln2_b', 'new_m_a_w_in', 'new_m_a_dw', 'new_m_a_dw_b', 'new_m_a_ln_g', 'new_m_a_ln_b', 'new_m_a_w_out', 'new_m_b_w_in', 'new_m_b_ln_g', 'new_m_b_ln_b', 'new_m_b_ws', 'new_m_b_bs', 'new_m_b_w_out', 'new_m_c_w_in', 'new_m_c_w_grp', 'new_m_c_scale', 'new_m_c_w_out', 'new_m_f_w_up', 'new_m_f_dw', 'new_m_f_w_down', 'new_m_ln1_g', 'new_m_ln1_b', 'new_m_ln2_g', 'new_m_ln2_b', 'new_v_a_w_in', 'new_v_a_dw', 'new_v_a_dw_b', 'new_v_a_ln_g', 'new_v_a_ln_b', 'new_v_a_w_out', 'new_v_b_w_in', 'new_v_b_ln_g', 'new_v_b_ln_b', 'new_v_b_ws', 'new_v_b_bs', 'new_v_b_w_out', 'new_v_c_w_in', 'new_v_c_w_grp', 'new_v_c_scale', 'new_v_c_w_out', 'new_v_f_w_up', 'new_v_f_dw', 'new_v_f_w_down', 'new_v_ln1_g', 'new_v_ln1_b', 'new_v_ln2_g', 'new_v_ln2_b']
TWIN_LEAF_KINDS = {'loss': 'loss', 'grad_x': 'grad_x', 'grad_a_w_in': 'grad_w', 'grad_a_dw': 'grad_w', 'grad_a_dw_b': 'grad_w', 'grad_a_ln_g': 'grad_w', 'grad_a_ln_b': 'grad_w', 'grad_a_w_out': 'grad_w', 'grad_b_w_in': 'grad_w', 'grad_b_ln_g': 'grad_w', 'grad_b_ln_b': 'grad_w', 'grad_b_ws': 'grad_w', 'grad_b_bs': 'grad_w', 'grad_b_w_out': 'grad_w', 'grad_c_w_in': 'grad_w', 'grad_c_w_grp': 'grad_w', 'grad_c_scale': 'grad_w', 'grad_c_w_out': 'grad_w', 'grad_f_w_up': 'grad_w', 'grad_f_dw': 'grad_w', 'grad_f_w_down': 'grad_w', 'grad_ln1_g': 'grad_w', 'grad_ln1_b': 'grad_w', 'grad_ln2_g': 'grad_w', 'grad_ln2_b': 'grad_w', 'delta_a_w_in': 'delta_w', 'delta_a_dw': 'delta_w', 'delta_a_dw_b': 'delta_w', 'delta_a_ln_g': 'delta_w', 'delta_a_ln_b': 'delta_w', 'delta_a_w_out': 'delta_w', 'delta_b_w_in': 'delta_w', 'delta_b_ln_g': 'delta_w', 'delta_b_ln_b': 'delta_w', 'delta_b_ws': 'delta_w', 'delta_b_bs': 'delta_w', 'delta_b_w_out': 'delta_w', 'delta_c_w_in': 'delta_w', 'delta_c_w_grp': 'delta_w', 'delta_c_scale': 'delta_w', 'delta_c_w_out': 'delta_w', 'delta_f_w_up': 'delta_w', 'delta_f_dw': 'delta_w', 'delta_f_w_down': 'delta_w', 'delta_ln1_g': 'delta_w', 'delta_ln1_b': 'delta_w', 'delta_ln2_g': 'delta_w', 'delta_ln2_b': 'delta_w', 'new_m_a_w_in': 'new_m', 'new_m_a_dw': 'new_m', 'new_m_a_dw_b': 'new_m', 'new_m_a_ln_g': 'new_m', 'new_m_a_ln_b': 'new_m', 'new_m_a_w_out': 'new_m', 'new_m_b_w_in': 'new_m', 'new_m_b_ln_g': 'new_m', 'new_m_b_ln_b': 'new_m', 'new_m_b_ws': 'new_m', 'new_m_b_bs': 'new_m', 'new_m_b_w_out': 'new_m', 'new_m_c_w_in': 'new_m', 'new_m_c_w_grp': 'new_m', 'new_m_c_scale': 'new_m', 'new_m_c_w_out': 'new_m', 'new_m_f_w_up': 'new_m', 'new_m_f_dw': 'new_m', 'new_m_f_w_down': 'new_m', 'new_m_ln1_g': 'new_m', 'new_m_ln1_b': 'new_m', 'new_m_ln2_g': 'new_m', 'new_m_ln2_b': 'new_m', 'new_v_a_w_in': 'new_v', 'new_v_a_dw': 'new_v', 'new_v_a_dw_b': 'new_v', 'new_v_a_ln_g': 'new_v', 'new_v_a_ln_b': 'new_v', 'new_v_a_w_out': 'new_v', 'new_v_b_w_in': 'new_v', 'new_v_b_ln_g': 'new_v', 'new_v_b_ln_b': 'new_v', 'new_v_b_ws': 'new_v', 'new_v_b_bs': 'new_v', 'new_v_b_w_out': 'new_v', 'new_v_c_w_in': 'new_v', 'new_v_c_w_grp': 'new_v', 'new_v_c_scale': 'new_v', 'new_v_c_w_out': 'new_v', 'new_v_f_w_up': 'new_v', 'new_v_f_dw': 'new_v', 'new_v_f_w_down': 'new_v', 'new_v_ln1_g': 'new_v', 'new_v_ln1_b': 'new_v', 'new_v_ln2_g': 'new_v', 'new_v_ln2_b': 'new_v'}


def _forward(args):
    return _fwd_reference(*[args[k] for k in FWD_PARAMS])


def _output_shape():
    def fwd():
        inp = _fwd_setup_inputs(0)
        return _fwd_reference(*[inp[k] for k in FWD_PARAMS])
    out = _jax.eval_shape(fwd)
    return out.shape, out.dtype

N_MICROBATCH = 1
ADAM_LR = 0.001
ADAM_B1 = 0.9
ADAM_B2 = 0.999
ADAM_EPS = 1e-08
ADAM_WD = 0.01
ADAM_STEP = 10
PER_EXAMPLE_BATCH_AXIS = {'x': 0, 'loss_target': 0}
SHARED_INPUTS = []
_WEIGHT_DTYPES = {'a_w_in': _jnp.float32, 'a_dw': _jnp.float32, 'a_dw_b': _jnp.float32, 'a_ln_g': _jnp.float32, 'a_ln_b': _jnp.float32, 'a_w_out': _jnp.float32, 'b_w_in': _jnp.float32, 'b_ln_g': _jnp.float32, 'b_ln_b': _jnp.float32, 'b_ws': _jnp.float32, 'b_bs': _jnp.float32, 'b_w_out': _jnp.float32, 'c_w_in': _jnp.float32, 'c_w_grp': _jnp.float32, 'c_scale': _jnp.float32, 'c_w_out': _jnp.float32, 'f_w_up': _jnp.float32, 'f_dw': _jnp.float32, 'f_w_down': _jnp.float32, 'ln1_g': _jnp.float32, 'ln1_b': _jnp.float32, 'ln2_g': _jnp.float32, 'ln2_b': _jnp.float32}
MOMENT_SCALE = {'a_w_in': 4.319086e-02, 'a_dw': 5.840319e-02, 'a_dw_b': 3.253053e-01, 'a_ln_g': 1.281794e-01, 'a_ln_b': 2.107965e-01, 'a_w_out': 1.964265e-01, 'b_w_in': 6.317002e-02, 'b_ln_g': 4.285068e-02, 'b_ln_b': 4.253155e-02, 'b_ws': 4.250579e-02, 'b_bs': 6.002206e-02, 'b_w_out': 2.589577e-01, 'c_w_in': 7.874951e-02, 'c_w_grp': 7.942263e-02, 'c_scale': 7.762131e-02, 'c_w_out': 1.886901e-01, 'f_w_up': 3.383554e-02, 'f_dw': 3.469225e-02, 'f_w_down': 1.318292e-01, 'ln1_g': 3.504576e+00, 'ln1_b': 1.887552e+00, 'ln2_g': 6.430094e+01, 'ln2_b': 6.789907e+00}


def _to_microbatches(a, axis):
    t = _jnp.moveaxis(a, axis, 0)
    t = t.reshape((N_MICROBATCH, t.shape[0] // N_MICROBATCH) + t.shape[1:])
    return _jnp.moveaxis(t, 1, axis + 1)


def setup_inputs(seed: int = 0) -> dict:
    inp = _fwd_setup_inputs(seed)
    key = _jax.random.fold_in(_jax.random.key(seed), 7919)
    shape, _ = _output_shape()
    out = dict(inp)
    out["loss_target"] = _jax.random.normal(_jax.random.fold_in(key, 0), shape, _jnp.float32)
    for i, name in enumerate(TWIN_WEIGHTS):
        w = inp[name].astype(_jnp.float32)
        if MOMENT_SCALE is None:
            s = _jnp.sqrt(_jnp.mean(_jnp.square(w)) + 1e-30)
        else:
            s = MOMENT_SCALE[name]
        km, kv = _jax.random.split(_jax.random.fold_in(key, i + 1))
        out[name] = w
        out["m_" + name] = s * _jax.random.normal(km, w.shape, _jnp.float32)
        out["v_" + name] = (s * s) * _jax.random.uniform(kv, w.shape, _jnp.float32, 0.5, 1.5)
    if N_MICROBATCH > 1:
        for name, axis in PER_EXAMPLE_BATCH_AXIS.items():
            out[name] = _to_microbatches(out[name], axis)
    return {'x': out['x'], 'a_w_in': out['a_w_in'], 'a_dw': out['a_dw'], 'a_dw_b': out['a_dw_b'], 'a_ln_g': out['a_ln_g'], 'a_ln_b': out['a_ln_b'], 'a_w_out': out['a_w_out'], 'b_w_in': out['b_w_in'], 'b_ln_g': out['b_ln_g'], 'b_ln_b': out['b_ln_b'], 'b_ws': out['b_ws'], 'b_bs': out['b_bs'], 'b_w_out': out['b_w_out'], 'c_w_in': out['c_w_in'], 'c_w_grp': out['c_w_grp'], 'c_scale': out['c_scale'], 'c_w_out': out['c_w_out'], 'f_w_up': out['f_w_up'], 'f_dw': out['f_dw'], 'f_w_down': out['f_w_down'], 'ln1_g': out['ln1_g'], 'ln1_b': out['ln1_b'], 'ln2_g': out['ln2_g'], 'ln2_b': out['ln2_b'], 'loss_target': out['loss_target'], 'm_a_w_in': out['m_a_w_in'], 'm_a_dw': out['m_a_dw'], 'm_a_dw_b': out['m_a_dw_b'], 'm_a_ln_g': out['m_a_ln_g'], 'm_a_ln_b': out['m_a_ln_b'], 'm_a_w_out': out['m_a_w_out'], 'm_b_w_in': out['m_b_w_in'], 'm_b_ln_g': out['m_b_ln_g'], 'm_b_ln_b': out['m_b_ln_b'], 'm_b_ws': out['m_b_ws'], 'm_b_bs': out['m_b_bs'], 'm_b_w_out': out['m_b_w_out'], 'm_c_w_in': out['m_c_w_in'], 'm_c_w_grp': out['m_c_w_grp'], 'm_c_scale': out['m_c_scale'], 'm_c_w_out': out['m_c_w_out'], 'm_f_w_up': out['m_f_w_up'], 'm_f_dw': out['m_f_dw'], 'm_f_w_down': out['m_f_w_down'], 'm_ln1_g': out['m_ln1_g'], 'm_ln1_b': out['m_ln1_b'], 'm_ln2_g': out['m_ln2_g'], 'm_ln2_b': out['m_ln2_b'], 'v_a_w_in': out['v_a_w_in'], 'v_a_dw': out['v_a_dw'], 'v_a_dw_b': out['v_a_dw_b'], 'v_a_ln_g': out['v_a_ln_g'], 'v_a_ln_b': out['v_a_ln_b'], 'v_a_w_out': out['v_a_w_out'], 'v_b_w_in': out['v_b_w_in'], 'v_b_ln_g': out['v_b_ln_g'], 'v_b_ln_b': out['v_b_ln_b'], 'v_b_ws': out['v_b_ws'], 'v_b_bs': out['v_b_bs'], 'v_b_w_out': out['v_b_w_out'], 'v_c_w_in': out['v_c_w_in'], 'v_c_w_grp': out['v_c_w_grp'], 'v_c_scale': out['v_c_scale'], 'v_c_w_out': out['v_c_w_out'], 'v_f_w_up': out['v_f_w_up'], 'v_f_dw': out['v_f_dw'], 'v_f_w_down': out['v_f_w_down'], 'v_ln1_g': out['v_ln1_g'], 'v_ln1_b': out['v_ln1_b'], 'v_ln2_g': out['v_ln2_g'], 'v_ln2_b': out['v_ln2_b']}


def _loss(weights, diff, rest, loss_target):
    with _jax.named_scope("forward"):
        args = {**rest, TWIN_DIFF_INPUT: diff, **{k: w.astype(_WEIGHT_DTYPES[k]) for k, w in weights.items()}}
        y = _forward(args)
    with _jax.named_scope("loss_head"):
        err = _jnp.square(y.astype(_jnp.float32) - loss_target)
        return 0.5 * _jnp.sum(_jnp.mean(err, axis=-1)) if err.ndim else 0.5 * err


def _adamw(w, g, m, v):
    m = ADAM_B1 * m + (1.0 - ADAM_B1) * g
    v = ADAM_B2 * v + (1.0 - ADAM_B2) * _jnp.square(g)
    m_hat = m / (1.0 - ADAM_B1 ** ADAM_STEP)
    v_hat = v / (1.0 - ADAM_B2 ** ADAM_STEP)
    delta = -ADAM_LR * (m_hat / (_jnp.sqrt(v_hat) + ADAM_EPS) + ADAM_WD * w)
    return delta, m, v


def reference(x, a_w_in, a_dw, a_dw_b, a_ln_g, a_ln_b, a_w_out, b_w_in, b_ln_g, b_ln_b, b_ws, b_bs, b_w_out, c_w_in, c_w_grp, c_scale, c_w_out, f_w_up, f_dw, f_w_down, ln1_g, ln1_b, ln2_g, ln2_b, loss_target, m_a_w_in, m_a_dw, m_a_dw_b, m_a_ln_g, m_a_ln_b, m_a_w_out, m_b_w_in, m_b_ln_g, m_b_ln_b, m_b_ws, m_b_bs, m_b_w_out, m_c_w_in, m_c_w_grp, m_c_scale, m_c_w_out, m_f_w_up, m_f_dw, m_f_w_down, m_ln1_g, m_ln1_b, m_ln2_g, m_ln2_b, v_a_w_in, v_a_dw, v_a_dw_b, v_a_ln_g, v_a_ln_b, v_a_w_out, v_b_w_in, v_b_ln_g, v_b_ln_b, v_b_ws, v_b_bs, v_b_w_out, v_c_w_in, v_c_w_grp, v_c_scale, v_c_w_out, v_f_w_up, v_f_dw, v_f_w_down, v_ln1_g, v_ln1_b, v_ln2_g, v_ln2_b):
    given = dict(x=x, a_w_in=a_w_in, a_dw=a_dw, a_dw_b=a_dw_b, a_ln_g=a_ln_g, a_ln_b=a_ln_b, a_w_out=a_w_out, b_w_in=b_w_in, b_ln_g=b_ln_g, b_ln_b=b_ln_b, b_ws=b_ws, b_bs=b_bs, b_w_out=b_w_out, c_w_in=c_w_in, c_w_grp=c_w_grp, c_scale=c_scale, c_w_out=c_w_out, f_w_up=f_w_up, f_dw=f_dw, f_w_down=f_w_down, ln1_g=ln1_g, ln1_b=ln1_b, ln2_g=ln2_g, ln2_b=ln2_b, loss_target=loss_target, m_a_w_in=m_a_w_in, m_a_dw=m_a_dw, m_a_dw_b=m_a_dw_b, m_a_ln_g=m_a_ln_g, m_a_ln_b=m_a_ln_b, m_a_w_out=m_a_w_out, m_b_w_in=m_b_w_in, m_b_ln_g=m_b_ln_g, m_b_ln_b=m_b_ln_b, m_b_ws=m_b_ws, m_b_bs=m_b_bs, m_b_w_out=m_b_w_out, m_c_w_in=m_c_w_in, m_c_w_grp=m_c_w_grp, m_c_scale=m_c_scale, m_c_w_out=m_c_w_out, m_f_w_up=m_f_w_up, m_f_dw=m_f_dw, m_f_w_down=m_f_w_down, m_ln1_g=m_ln1_g, m_ln1_b=m_ln1_b, m_ln2_g=m_ln2_g, m_ln2_b=m_ln2_b, v_a_w_in=v_a_w_in, v_a_dw=v_a_dw, v_a_dw_b=v_a_dw_b, v_a_ln_g=v_a_ln_g, v_a_ln_b=v_a_ln_b, v_a_w_out=v_a_w_out, v_b_w_in=v_b_w_in, v_b_ln_g=v_b_ln_g, v_b_ln_b=v_b_ln_b, v_b_ws=v_b_ws, v_b_bs=v_b_bs, v_b_w_out=v_b_w_out, v_c_w_in=v_c_w_in, v_c_w_grp=v_c_w_grp, v_c_scale=v_c_scale, v_c_w_out=v_c_w_out, v_f_w_up=v_f_w_up, v_f_dw=v_f_dw, v_f_w_down=v_f_w_down, v_ln1_g=v_ln1_g, v_ln1_b=v_ln1_b, v_ln2_g=v_ln2_g, v_ln2_b=v_ln2_b)
    weights = {n: given[n] for n in TWIN_WEIGHTS}
    shared = {n: given[n] for n in SHARED_INPUTS}
    per_example = {n: given[n] for n in ['x']}
    grad_fn = _jax.value_and_grad(_loss, argnums=(0, 1))

    def one_microbatch(ex, loss_target):
        ex = dict(ex)
        diff = ex.pop(TWIN_DIFF_INPUT)
        return grad_fn(weights, diff, {**shared, **ex}, loss_target)

    if N_MICROBATCH == 1:
        loss, (grad_w, grad_x) = one_microbatch(per_example, given["loss_target"])
    else:
        def body(carry, xs):
            loss_sum, grad_sum = carry
            l_k, (gw_k, gx_k) = one_microbatch(xs[0], xs[1])
            with _jax.named_scope("update"):
                return (loss_sum + l_k, _jax.tree.map(_jnp.add, grad_sum, gw_k)), gx_k

        init = (_jnp.zeros((), _jnp.float32), _jax.tree.map(_jnp.zeros_like, weights))
        (loss, grad_w), grad_x = _jax.lax.scan(body, init, (per_example, given["loss_target"]))
    with _jax.named_scope("update"):
        delta_w, new_m, new_v = {}, {}, {}
        for n in TWIN_WEIGHTS:
            delta_w[n], new_m[n], new_v[n] = _adamw(weights[n], grad_w[n], given["m_" + n], given["v_" + n])
    return (loss, grad_x, *[grad_w[n] for n in TWIN_WEIGHTS], *[delta_w[n] for n in TWIN_WEIGHTS],
            *[new_m[n] for n in TWIN_WEIGHTS], *[new_v[n] for n in TWIN_WEIGHTS])
```

```python
import functools

import jax
import jax.numpy as jnp
from jax import lax
from jax.experimental import pallas as pl
from jax.experimental.pallas import tpu as pltpu

N_DEV = 8
DEPTH = 4
ALPHA = float((2 * DEPTH) ** 0.25)
LN_EPS = 1e-5
CONV_W = 31
CONV_HALO = 32
FFN_HALO = 16
POOL_WINDOWS = (2, 4, 8, 16)
CHUNK = 128
HEADS = 8
LANES = 1024
ADAM_LR, ADAM_B1, ADAM_B2, ADAM_EPS, ADAM_WD, ADAM_STEP = 0.001, 0.9, 0.999, 1e-08, 0.01, 10
VMEM_LIMIT = 52 * 1024 * 1024
F32, BF16 = jnp.float32, jnp.bfloat16
MESH = pl.DeviceIdType.MESH

WEIGHTS = ['a_w_in', 'a_dw', 'a_dw_b', 'a_ln_g', 'a_ln_b', 'a_w_out', 'b_w_in', 'b_ln_g', 'b_ln_b', 'b_ws', 'b_bs',
           'b_w_out', 'c_w_in', 'c_w_grp', 'c_scale', 'c_w_out', 'f_w_up', 'f_dw', 'f_w_down', 'ln1_g', 'ln1_b',
           'ln2_g', 'ln2_b']
REPLICATED = ('b_ln_g', 'b_ln_b', 'b_ws', 'b_bs', 'ln1_g', 'ln1_b', 'ln2_g', 'ln2_b')
GATHER_BF16 = ('a_w_in', 'a_w_out', 'b_w_in', 'b_w_out', 'c_w_in', 'c_w_grp', 'c_w_out', 'f_w_up', 'f_w_down')
GATHER_F32 = ('a_dw', 'a_dw_b', 'a_ln_g', 'a_ln_b', 'c_scale', 'f_dw')


def _params(n_axes):
    return pltpu.CompilerParams(dimension_semantics=("arbitrary",) * n_axes, vmem_limit_bytes=VMEM_LIMIT)


def _sigmoid(x):
    return 1.0 / (1.0 + jnp.exp(-x))


def _gelu(x):
    return 0.5 * x * (1.0 + lax.erf(x * 0.7071067811865476))


def _gelu_grad(x):
    return 0.5 * (1.0 + lax.erf(x * 0.7071067811865476)) + x * jnp.exp(-0.5 * x * x) * 0.3989422804014327


def _ln_stats(r):
    mu = jnp.mean(r, axis=-1, keepdims=True)
    xc = r - mu
    var = jnp.mean(xc * xc, axis=-1, keepdims=True)
    rstd = lax.rsqrt(var + LN_EPS)
    return xc * rstd, rstd


def _ln_bwd(dy, xhat, rstd, g):
    dxhat = dy * g
    m1 = jnp.mean(dxhat, axis=-1, keepdims=True)
    m2 = jnp.mean(dxhat * xhat, axis=-1, keepdims=True)
    dr = rstd * (dxhat - m1 - xhat * m2)
    return dr, jnp.sum(dy * xhat, axis=0, keepdims=True), jnp.sum(dy, axis=0, keepdims=True)


def _accumulate(ref, value, first):
    @pl.when(first)
    def _():
        ref[...] = value

    @pl.when(jnp.logical_not(first))
    def _():
        ref[...] += value


def _dot(a, b):
    return jnp.dot(a, b, preferred_element_type=F32)


def _dot_nt(a, b):
    return lax.dot_general(a, b, (((1,), (1,)), ((), ())), preferred_element_type=F32)


def _dot_tn(a, b):
    return lax.dot_general(a, b, (((0,), (0,)), ((), ())), preferred_element_type=F32)


def _cat_lanes(h, lo, hi):
    return jnp.concatenate([h[s] for s in range(lo, hi)], axis=-1)


def _mm_nn(x, w, out_dtype, name, tm=512):
    t, k = x.shape
    s_n, _, n = w.shape
    tm = min(tm, t)

    def body(x_ref, w_ref, o_ref):
        o_ref[...] = _dot(x_ref[...], w_ref[...]).astype(o_ref.dtype)

    return pl.pallas_call(
        body, out_shape=jax.ShapeDtypeStruct((s_n, t, n), out_dtype), grid=(t // tm, s_n),
        in_specs=[pl.BlockSpec((tm, k), lambda i, s: (i, 0)), pl.BlockSpec((None, k, n), lambda i, s: (s, 0, 0))],
        out_specs=pl.BlockSpec((None, tm, n), lambda i, s: (s, i, 0)),
        name=name, compiler_params=_params(2))(x, w)


def _mm_res_ln(a, w, res, gp, bp, g, b, name, tm=512):
    s_n, t, ka = a.shape
    d = w.shape[-1]
    tm = min(tm, t)

    def body(a_ref, w_ref, res_ref, gp_ref, bp_ref, g_ref, b_ref, xhat_ref, y_ref, rstd_ref):
        acc = _dot(a_ref[0], w_ref[0])
        for s in range(1, s_n):
            acc += _dot(a_ref[s], w_ref[s])
        r = ALPHA * (res_ref[...] * gp_ref[...] + bp_ref[...]) + acc
        xhat, rstd = _ln_stats(r)
        xhat_ref[...] = xhat
        y_ref[...] = (xhat * g_ref[...] + b_ref[...]).astype(BF16)
        rstd_ref[...] = rstd

    row = pl.BlockSpec((1, d), lambda i: (0, 0))
    tile = pl.BlockSpec((tm, d), lambda i: (i, 0))
    return pl.pallas_call(
        body,
        out_shape=(jax.ShapeDtypeStruct((t, d), F32), jax.ShapeDtypeStruct((t, d), BF16),
                   jax.ShapeDtypeStruct((t, 1), F32)),
        grid=(t // tm,),
        in_specs=[pl.BlockSpec((s_n, tm, ka), lambda i: (0, i, 0)), pl.BlockSpec((s_n, ka, d), lambda i: (0, 0, 0)),
                  tile, row, row, row, row],
        out_specs=(tile, tile, pl.BlockSpec((tm, 1), lambda i: (i, 0))),
        name=name, compiler_params=_params(1))(a, w, res, gp, bp, g, b)


def _mm_nt_out(x, w, name, tm=512):
    t, n = x.shape
    s_n, k, _ = w.shape
    tm = min(tm, t)

    def body(x_ref, w_ref, o_ref):
        o_ref[...] = _dot_nt(x_ref[...], w_ref[...]).astype(o_ref.dtype)

    return pl.pallas_call(
        body, out_shape=jax.ShapeDtypeStruct((s_n, t, k), BF16), grid=(t // tm, s_n),
        in_specs=[pl.BlockSpec((tm, n), lambda i, s: (i, 0)), pl.BlockSpec((None, k, n), lambda i, s: (s, 0, 0))],
        out_specs=pl.BlockSpec((None, tm, k), lambda i, s: (s, i, 0)),
        name=name, compiler_params=_params(2))(x, w)


def _mm_nt_lnb(dh, w, drn, xhat, rstd, g, name, tm=512):
    s_n, t, n = dh.shape
    k = w.shape[1]
    tm = min(tm, t)

    def body(dh_ref, w_ref, drn_ref, xhat_ref, rstd_ref, g_ref, dr_ref, drb_ref, dg_ref, db_ref, acc_ref):
        i, s = pl.program_id(0), pl.program_id(1)
        _accumulate(acc_ref, _dot_nt(dh_ref[...], w_ref[...]), s == 0)

        @pl.when(s == s_n - 1)
        def _():
            dy = acc_ref[...] + ALPHA * drn_ref[...]
            dr, dg, db = _ln_bwd(dy, xhat_ref[...], rstd_ref[...], g_ref[...])
            dr_ref[...] = dr
            drb_ref[...] = dr.astype(BF16)
            _accumulate(dg_ref, dg, i == 0)
            _accumulate(db_ref, db, i == 0)

    tile = pl.BlockSpec((tm, k), lambda i, s: (i, 0))
    row = pl.BlockSpec((1, k), lambda i, s: (0, 0))
    return pl.pallas_call(
        body,
        out_shape=(jax.ShapeDtypeStruct((t, k), F32), jax.ShapeDtypeStruct((t, k), BF16),
                   jax.ShapeDtypeStruct((1, k), F32), jax.ShapeDtypeStruct((1, k), F32)),
        grid=(t // tm, s_n),
        in_specs=[pl.BlockSpec((None, tm, n), lambda i, s: (s, i, 0)),
                  pl.BlockSpec((None, k, n), lambda i, s: (s, 0, 0)),
                  tile, tile, pl.BlockSpec((tm, 1), lambda i, s: (i, 0)), row],
        out_specs=(tile, tile, row, row),
        scratch_shapes=[pltpu.VMEM((tm, k), F32)],
        name=name, compiler_params=_params(2))(dh, w, drn, xhat, rstd, g)


def _mm_nt_res(dh, w, drn, name, tm=512):
    s_n, t, n = dh.shape
    k = w.shape[1]
    tm = min(tm, t)

    def body(dh_ref, w_ref, drn_ref, o_ref, acc_ref):
        s = pl.program_id(1)
        _accumulate(acc_ref, _dot_nt(dh_ref[...], w_ref[...]), s == 0)

        @pl.when(s == s_n - 1)
        def _():
            o_ref[...] = acc_ref[...] + ALPHA * drn_ref[...]

    tile = pl.BlockSpec((tm, k), lambda i, s: (i, 0))
    return pl.pallas_call(
        body, out_shape=jax.ShapeDtypeStruct((t, k), F32), grid=(t // tm, s_n),
        in_specs=[pl.BlockSpec((None, tm, n), lambda i, s: (s, i, 0)),
                  pl.BlockSpec((None, k, n), lambda i, s: (s, 0, 0)), tile],
        out_specs=tile, scratch_shapes=[pltpu.VMEM((tm, k), F32)],
        name=name, compiler_params=_params(2))(dh, w, drn)


def _mm_tn(lhs, rhs, name, tm=512):
    sl, t, kl = lhs.shape
    sr, _, n = rhs.shape
    s_n = max(sl, sr)
    tm = min(tm, t)

    def body(l_ref, r_ref, o_ref):
        _accumulate(o_ref, _dot_tn(l_ref[...], r_ref[...]), pl.program_id(1) == 0)

    return pl.pallas_call(
        body, out_shape=jax.ShapeDtypeStruct((s_n, kl, n), F32), grid=(s_n, t // tm),
        in_specs=[pl.BlockSpec((None, tm, kl), (lambda s, i: (s, i, 0)) if sl > 1 else (lambda s, i: (0, i, 0))),
                  pl.BlockSpec((None, tm, n), (lambda s, i: (s, i, 0)) if sr > 1 else (lambda s, i: (0, i, 0)))],
        out_specs=pl.BlockSpec((None, kl, n), lambda s, i: (s, 0, 0)),
        name=name, compiler_params=_params(2))(lhs, rhs)


def _glu(h):
    return _cat_lanes(h, 0, 4).astype(F32) * _sigmoid(_cat_lanes(h, 4, 8).astype(F32))


def _tap_loop(src_ref, dst_ref, weight_ref, rows, offset_of_tap, tap_of_weight, row_block=64):
    d = dst_ref.shape[-1]

    def block(cb, carry):
        c0 = pl.multiple_of(cb * 128, 128)
        for r0 in range(0, rows, row_block):
            acc = jnp.zeros((row_block, 128), F32)
            for k in range(CONV_W):
                wk = weight_ref[pl.ds(tap_of_weight(k), 1), pl.ds(c0, 128)]
                acc += wk * src_ref[pl.ds(r0 + offset_of_tap(k), row_block), pl.ds(c0, 128)]
            dst_ref[pl.ds(r0, row_block), pl.ds(c0, 128)] = acc
        return carry

    lax.fori_loop(0, d // 128, block, 0)


def _conv_fwd(h1, dw, dwb, g, b, name, tm=256):
    _, t, _ = h1.shape
    d = dw.shape[-1]
    tm = min(tm, t)
    hb = tm // CONV_HALO

    def body(h_ref, halo_ref, dw_ref, dwb_ref, g_ref, b_ref, s_ref, q_ref, ext_ref):
        i = pl.program_id(0)
        ext_ref[pl.ds(0, CONV_HALO), :] = _glu(halo_ref[...]) * (i > 0).astype(F32)
        ext_ref[pl.ds(CONV_HALO, tm), :] = _glu(h_ref[...])
        _tap_loop(ext_ref, q_ref, dw_ref, tm, lambda k: 2 + k, lambda k: k)
        q = q_ref[...] + dwb_ref[...]
        q_ref[...] = q
        qhat, _ = _ln_stats(q)
        z = qhat * g_ref[...] + b_ref[...]
        s_ref[...] = (z * _sigmoid(z)).astype(BF16)

    row = pl.BlockSpec((1, d), lambda i: (0, 0))
    tile = pl.BlockSpec((tm, d), lambda i: (i, 0))
    return pl.pallas_call(
        body, out_shape=(jax.ShapeDtypeStruct((t, d), BF16), jax.ShapeDtypeStruct((t, d), F32)), grid=(t // tm,),
        in_specs=[pl.BlockSpec((8, tm, 256), lambda i: (0, i, 0)),
                  pl.BlockSpec((8, CONV_HALO, 256), lambda i: (0, jnp.maximum(i * hb - 1, 0), 0)),
                  pl.BlockSpec((CONV_W, d), lambda i: (0, 0)), row, row, row],
        out_specs=(tile, tile), scratch_shapes=[pltpu.VMEM((tm + CONV_HALO, d), F32)],
        name=name, compiler_params=_params(1))(h1, h1, dw, dwb, g, b)


def _conv_bwd(ds, q, h1, dw, g, b, name, tm=256):
    _, t, _ = h1.shape
    d = dw.shape[-1]
    tm = min(tm, t)
    hb = tm // CONV_HALO
    n_t = t // tm
    last_halo = t // CONV_HALO - 1

    def body(ds_ref, dsn_ref, q_ref, qn_ref, h_ref, hp_ref, dw_ref, g_ref, b_ref,
             dh_ref, ddw_ref, ddwb_ref, dg_ref, db_ref, dq_ref, p_ref, dp_ref):
        i = pl.program_id(0)
        first = i == 0
        valid = (i < n_t - 1).astype(F32)

        def dq_rows(ds_rows, q_rows, scale):
            qhat, rstd = _ln_stats(q_rows)
            z = qhat * g_ref[...] + b_ref[...]
            sg = _sigmoid(z)
            dz = ds_rows.astype(F32) * (sg * (1.0 + z * (1.0 - sg))) * scale
            dq, dg, db = _ln_bwd(dz, qhat, rstd, g_ref[...])
            return dq, dg, db

        dq, dg, db = dq_rows(ds_ref[...], q_ref[...], 1.0)
        dq_ref[pl.ds(0, tm), :] = dq
        dq_ref[pl.ds(tm, CONV_HALO), :] = dq_rows(dsn_ref[...], qn_ref[...], valid)[0]
        _accumulate(dg_ref, dg, first)
        _accumulate(db_ref, db, first)
        _accumulate(ddwb_ref, jnp.sum(dq, axis=0, keepdims=True), first)

        p_ref[pl.ds(0, CONV_HALO), :] = _glu(hp_ref[...]) * (i > 0).astype(F32)
        p_ref[pl.ds(CONV_HALO, tm), :] = _glu(h_ref[...])

        _tap_loop(dq_ref, dp_ref, dw_ref, tm, lambda o: o, lambda o: CONV_W - 1 - o)

        @pl.when(first)
        def _():
            ddw_ref[...] = jnp.zeros_like(ddw_ref)

        row_block = 64

        def block(cb, carry):
            c0 = pl.multiple_of(cb * 128, 128)
            for r0 in range(0, tm, row_block):
                dqb = dq_ref[pl.ds(r0, row_block), pl.ds(c0, 128)]
                for k in range(CONV_W):
                    prod = dqb * p_ref[pl.ds(r0 + 2 + k, row_block), pl.ds(c0, 128)]
                    ddw_ref[k, :, pl.ds(c0, 128)] += jnp.sum(prod.reshape(row_block // 8, 8, 128), axis=0)
            return carry

        lax.fori_loop(0, d // 128, block, 0)

        h = h_ref[...]
        a = _cat_lanes(h, 0, 4).astype(F32)
        sg = _sigmoid(_cat_lanes(h, 4, 8).astype(F32))
        dp = dp_ref[...]
        da = (dp * sg).astype(BF16)
        dgate = (dp * a * sg * (1.0 - sg)).astype(BF16)
        for s in range(4):
            dh_ref[s] = da[:, s * 256:(s + 1) * 256]
            dh_ref[4 + s] = dgate[:, s * 256:(s + 1) * 256]

    row = pl.BlockSpec((1, d), lambda i: (0, 0))
    tile = pl.BlockSpec((tm, d), lambda i: (i, 0))
    nxt = pl.BlockSpec((CONV_HALO, d), lambda i: (jnp.minimum((i + 1) * hb, last_halo), 0))
    return pl.pallas_call(
        body,
        out_shape=(jax.ShapeDtypeStruct((8, t, 256), BF16), jax.ShapeDtypeStruct((CONV_W, 8, d), F32),
                   jax.ShapeDtypeStruct((1, d), F32), jax.ShapeDtypeStruct((1, d), F32),
                   jax.ShapeDtypeStruct((1, d), F32)),
        grid=(n_t,),
        in_specs=[tile, nxt, tile, nxt,
                  pl.BlockSpec((8, tm, 256), lambda i: (0, i, 0)),
                  pl.BlockSpec((8, CONV_HALO, 256), lambda i: (0, jnp.maximum(i * hb - 1, 0), 0)),
                  pl.BlockSpec((CONV_W, d), lambda i: (0, 0)), row, row],
        out_specs=(pl.BlockSpec((8, tm, 256), lambda i: (0, i, 0)),
                   pl.BlockSpec((CONV_W, 8, d), lambda i: (0, 0, 0)), row, row, row),
        scratch_shapes=[pltpu.VMEM((tm + CONV_HALO, d), F32), pltpu.VMEM((tm + CONV_HALO, d), F32),
                        pltpu.VMEM((tm, d), F32)],
        name=name, compiler_params=_params(1))(ds, ds, q, q, h1, h1, dw, g, b)


def _conv3(ext, dw):
    e1 = pltpu.roll(ext, 1, 0)
    e2 = pltpu.roll(ext, 2, 0)
    return dw[2:3] * ext + dw[1:2] * e1 + dw[0:1] * e2, e1, e2


def _ffn_gate_fwd(u, fdw, name, tm=256):
    _, t, n = u.shape
    tm = min(tm, t)
    hb = tm // FFN_HALO
    u4 = u.reshape(2, 4, t, n)
    fdw4 = fdw.reshape(2, 4, 3, n)

    def body(u_ref, up_ref, dw_ref, a_ref):
        i = pl.program_id(1)
        keep = (i > 0).astype(F32)
        h = []
        for p in range(2):
            ext = jnp.concatenate([up_ref[p].astype(F32) * keep, u_ref[p].astype(F32)], axis=0)
            h.append(_conv3(ext, dw_ref[p])[0][FFN_HALO:])
        a_ref[...] = (h[0] * _sigmoid(h[0]) * h[1]).astype(BF16)

    return pl.pallas_call(
        body, out_shape=jax.ShapeDtypeStruct((4, t, n), BF16), grid=(4, t // tm),
        in_specs=[pl.BlockSpec((2, None, tm, n), lambda j, i: (0, j, i, 0)),
                  pl.BlockSpec((2, None, FFN_HALO, n), lambda j, i: (0, j, jnp.maximum(i * hb - 1, 0), 0)),
                  pl.BlockSpec((2, None, 3, n), lambda j, i: (0, j, 0, 0))],
        out_specs=pl.BlockSpec((None, tm, n), lambda j, i: (j, i, 0)),
        name=name, compiler_params=_params(2))(u4, u4, fdw4)


def _ffn_gate_bwd(u, da, fdw, name, tm=256):
    _, t, n = u.shape
    tm = min(tm, t)
    hb = tm // FFN_HALO
    n_t = t // tm
    last_halo = t // FFN_HALO - 1
    u4 = u.reshape(2, 4, t, n)
    fdw4 = fdw.reshape(2, 4, 3, n)

    def body(u_ref, up_ref, un_ref, da_ref, dan_ref, dw_ref, du_ref, ddw_ref):
        i = pl.program_id(1)
        keep_prev = (i > 0).astype(F32)
        keep_next = (i < n_t - 1).astype(F32)
        h, taps = [], []
        for p in range(2):
            ext = jnp.concatenate([up_ref[p].astype(F32) * keep_prev, u_ref[p].astype(F32),
                                   un_ref[p].astype(F32) * keep_next], axis=0)
            hp, e1, e2 = _conv3(ext, dw_ref[p])
            h.append(hp[FFN_HALO:])
            taps.append((e2, e1, ext))
        hg, hv = h
        sg = _sigmoid(hg)
        da_ext = jnp.concatenate([da_ref[...].astype(F32), dan_ref[...].astype(F32) * keep_next], axis=0)
        dh = (da_ext * hv * (sg * (1.0 + hg * (1.0 - sg))), da_ext * hg * sg)
        rows = tm + FFN_HALO
        for p in range(2):
            dwp = dw_ref[p]
            d1 = pltpu.roll(dh[p], rows - 1, 0)
            d2 = pltpu.roll(dh[p], rows - 2, 0)
            du_ref[p] = (dwp[2:3] * dh[p] + dwp[1:2] * d1 + dwp[0:1] * d2)[:tm].astype(BF16)
            dht = dh[p][:tm]
            part = jnp.concatenate(
                [jnp.sum(dht * taps[p][k][FFN_HALO:FFN_HALO + tm], axis=0, keepdims=True) for k in range(3)], axis=0)
            _accumulate(ddw_ref.at[p], part, i == 0)

    tile = pl.BlockSpec((2, None, tm, n), lambda j, i: (0, j, i, 0))
    prev = pl.BlockSpec((2, None, FFN_HALO, n), lambda j, i: (0, j, jnp.maximum(i * hb - 1, 0), 0))
    nxt = pl.BlockSpec((2, None, FFN_HALO, n), lambda j, i: (0, j, jnp.minimum((i + 1) * hb, last_halo), 0))
    du, ddw = pl.pallas_call(
        body, out_shape=(jax.ShapeDtypeStruct((2, 4, t, n), BF16), jax.ShapeDtypeStruct((2, 4, 3, n), F32)),
        grid=(4, n_t),
        in_specs=[tile, prev, nxt,
                  pl.BlockSpec((None, tm, n), lambda j, i: (j, i, 0)),
                  pl.BlockSpec((None, FFN_HALO, n), lambda j, i: (j, jnp.minimum((i + 1) * hb, last_halo), 0)),
                  pl.BlockSpec((2, None, 3, n), lambda j, i: (0, j, 0, 0))],
        out_specs=(tile, pl.BlockSpec((2, None, 3, n), lambda j, i: (0, j, 0, 0))),
        name=name, compiler_params=_params(2))(u4, u4, u4, da, da, fdw4)
    return du.reshape(8, t, n), ddw.reshape(8, 3, n)


def _tril_mask():
    r = lax.broadcasted_iota(jnp.int32, (CHUNK, CHUNK), 0)
    c = lax.broadcasted_iota(jnp.int32, (CHUNK, CHUNK), 1)
    return (r >= c).astype(F32)


def _sgu_fwd(h1, g, b, ws, bst, name, tm=256):
    _, t, _ = h1.shape
    d = g.shape[-1]
    tm = min(tm, t)

    def body(h_ref, g_ref, b_ref, ws_ref, bst_ref, m_ref):
        h = h_ref[...]
        u = _gelu(_cat_lanes(h, 0, 4).astype(F32))
        v = _gelu(_cat_lanes(h, 4, 8).astype(F32))
        vn = (_ln_stats(v)[0] * g_ref[...] + b_ref[...]).astype(BF16)
        mask = _tril_mask()
        for hh in range(HEADS):
            cols = slice(hh * CHUNK, (hh + 1) * CHUNK)
            wm = (ws_ref[hh] * mask).astype(BF16)
            bias = bst_ref[:, hh:hh + 1]
            for c in range(tm // CHUNK):
                rows = slice(c * CHUNK, (c + 1) * CHUNK)
                sblk = _dot(wm, vn[rows, cols]) + bias
                m_ref[rows, cols] = (u[rows, cols] * sblk).astype(BF16)

    row = pl.BlockSpec((1, d), lambda i: (0, 0))
    return pl.pallas_call(
        body, out_shape=jax.ShapeDtypeStruct((t, d), BF16), grid=(t // tm,),
        in_specs=[pl.BlockSpec((8, tm, 256), lambda i: (0, i, 0)), row, row,
                  pl.BlockSpec((HEADS, CHUNK, CHUNK), lambda i: (0, 0, 0)),
                  pl.BlockSpec((CHUNK, HEADS), lambda i: (0, 0))],
        out_specs=pl.BlockSpec((tm, d), lambda i: (i, 0)),
        name=name, compiler_params=_params(1))(h1, g, b, ws, bst)


def _sgu_bwd(h1, dm, g, b, ws, bst, name, tm=256):
    _, t, _ = h1.shape
    d = g.shape[-1]
    tm = min(tm, t)

    def body(h_ref, dm_ref, g_ref, b_ref, ws_ref, bst_ref, dh_ref, dg_ref, db_ref, dws_ref, dbias_ref,
             du_ref, dvn_ref):
        first = pl.program_id(0) == 0
        h = h_ref[...]
        zu = _cat_lanes(h, 0, 4).astype(F32)
        zv = _cat_lanes(h, 4, 8).astype(F32)
        u = _gelu(zu)
        vhat, rstd = _ln_stats(_gelu(zv))
        vn = (vhat * g_ref[...] + b_ref[...]).astype(BF16)
        dm = dm_ref[...].astype(F32)
        mask = _tril_mask()

        @pl.when(first)
        def _():
            dws_ref[...] = jnp.zeros_like(dws_ref)
            dbias_ref[...] = jnp.zeros_like(dbias_ref)

        for hh in range(HEADS):
            cols = slice(hh * CHUNK, (hh + 1) * CHUNK)
            wm = (ws_ref[hh] * mask).astype(BF16)
            bias = bst_ref[:, hh:hh + 1]
            for c in range(tm // CHUNK):
                rows = slice(c * CHUNK, (c + 1) * CHUNK)
                vb = vn[rows, cols]
                sblk = _dot(wm, vb) + bias
                dmb = dm[rows, cols]
                du_ref[rows, cols] = dmb * sblk
                dsb = dmb * u[rows, cols]
                dbias_ref[:, cols] += dsb
                dsb16 = dsb.astype(BF16)
                dws_ref[hh] += _dot_nt(dsb16, vb) * mask
                dvn_ref[rows, cols] = _dot_tn(wm, dsb16)

        dvn = dvn_ref[...]
        dv, dg, db = _ln_bwd(dvn, vhat, rstd, g_ref[...])
        _accumulate(dg_ref, dg, first)
        _accumulate(db_ref, db, first)
        dzu = (du_ref[...] * _gelu_grad(zu)).astype(BF16)
        dzv = (dv * _gelu_grad(zv)).astype(BF16)
        for s in range(4):
            dh_ref[s] = dzu[:, s * 256:(s + 1) * 256]
            dh_ref[4 + s] = dzv[:, s * 256:(s + 1) * 256]

    row = pl.BlockSpec((1, d), lambda i: (0, 0))
    tile = pl.BlockSpec((tm, d), lambda i: (i, 0))
    h_tile = pl.BlockSpec((8, tm, 256), lambda i: (0, i, 0))
    return pl.pallas_call(
        body,
        out_shape=(jax.ShapeDtypeStruct((8, t, 256), BF16), jax.ShapeDtypeStruct((1, d), F32),
                   jax.ShapeDtypeStruct((1, d), F32), jax.ShapeDtypeStruct((HEADS, CHUNK, CHUNK), F32),
                   jax.ShapeDtypeStruct((CHUNK, d), F32)),
        grid=(t // tm,),
        in_specs=[h_tile, tile, row, row, pl.BlockSpec((HEADS, CHUNK, CHUNK), lambda i: (0, 0, 0)),
                  pl.BlockSpec((CHUNK, HEADS), lambda i: (0, 0))],
        out_specs=(h_tile, row, row, pl.BlockSpec((HEADS, CHUNK, CHUNK), lambda i: (0, 0, 0)),
                   pl.BlockSpec((CHUNK, d), lambda i: (0, 0))),
        scratch_shapes=[pltpu.VMEM((tm, d), F32), pltpu.VMEM((tm, d), F32)],
        name=name, compiler_params=_params(1))(h1, dm, g, b, ws, bst)


def _pool_minus_self(ext, first_token, grp):
    s = ext
    for step in range(grp + 1):
        s = s + pltpu.roll(s, 1 << step, 0)
    rows = ext.shape[0] - FFN_HALO
    tok = first_token + lax.broadcasted_iota(jnp.int32, (rows, 1), 0)
    count = jnp.minimum(tok + 1, POOL_WINDOWS[grp]).astype(F32)
    return s[FFN_HALO:] / count - ext[FFN_HALO:]


def _pool_fwd(y, wgrp, scale, name, tm=256):
    t, d = y.shape
    tm = min(tm, t)
    hb = tm // FFN_HALO
    gd = d // len(POOL_WINDOWS)

    def body(y_ref, yp_ref, w_ref, sc_ref, z_ref):
        i = pl.program_id(0)
        ext = jnp.concatenate([yp_ref[...] * (i > 0).astype(F32), y_ref[...]], axis=0)
        for grp in range(len(POOL_WINDOWS)):
            cols = slice(grp * gd, (grp + 1) * gd)
            p = _pool_minus_self(ext[:, cols], i * tm, grp)
            z_ref[:, cols] = (_dot(p.astype(BF16), w_ref[grp]) * sc_ref[:, cols]).astype(BF16)

    return pl.pallas_call(
        body, out_shape=jax.ShapeDtypeStruct((t, d), BF16), grid=(t // tm,),
        in_specs=[pl.BlockSpec((tm, d), lambda i: (i, 0)),
                  pl.BlockSpec((FFN_HALO, d), lambda i: (jnp.maximum(i * hb - 1, 0), 0)),
                  pl.BlockSpec((len(POOL_WINDOWS), gd, gd), lambda i: (0, 0, 0)),
                  pl.BlockSpec((1, d), lambda i: (0, 0))],
        out_specs=pl.BlockSpec((tm, d), lambda i: (i, 0)),
        name=name, compiler_params=_params(1))(y, y, wgrp, scale)


def _pool_bwd(y, dz, wgrp, scale, name, tm=256):
    t, d = y.shape
    tm = min(tm, t)
    hb = tm // FFN_HALO
    n_t = t // tm
    last_halo = t // FFN_HALO - 1
    gd = d // len(POOL_WINDOWS)
    rows = tm + FFN_HALO

    def body(y_ref, yp_ref, dz_ref, dzn_ref, w_ref, sc_ref, dy_ref, dsc_ref, dw_ref):
        i = pl.program_id(0)
        first = i == 0
        ext = jnp.concatenate([yp_ref[...] * (i > 0).astype(F32), y_ref[...]], axis=0)
        dz_ext = jnp.concatenate([dz_ref[...].astype(F32), dzn_ref[...].astype(F32) * (i < n_t - 1).astype(F32)],
                                 axis=0)
        tok = i * tm + lax.broadcasted_iota(jnp.int32, (rows, 1), 0)
        dsc = []
        for grp in range(len(POOL_WINDOWS)):
            cols = slice(grp * gd, (grp + 1) * gd)
            p16 = _pool_minus_self(ext[:, cols], i * tm, grp).astype(BF16)
            zg = _dot(p16, w_ref[grp])
            dsc.append(jnp.sum(dz_ext[:tm, cols] * zg, axis=0, keepdims=True))
            dzg = (dz_ext[:, cols] * sc_ref[:, cols]).astype(BF16)
            _accumulate(dw_ref.at[grp], _dot_tn(p16, dzg[:tm]), first)
            dp = _dot_nt(dzg, w_ref[grp])
            s = dp / jnp.minimum(tok + 1, POOL_WINDOWS[grp]).astype(F32)
            for step in range(grp + 1):
                s = s + pltpu.roll(s, rows - (1 << step), 0)
            dy_ref[:, cols] = (s[:tm] - dp[:tm]).astype(BF16)
        _accumulate(dsc_ref, jnp.concatenate(dsc, axis=-1), first)

    tile = pl.BlockSpec((tm, d), lambda i: (i, 0))
    return pl.pallas_call(
        body,
        out_shape=(jax.ShapeDtypeStruct((t, d), BF16), jax.ShapeDtypeStruct((1, d), F32),
                   jax.ShapeDtypeStruct((len(POOL_WINDOWS), gd, gd), F32)),
        grid=(n_t,),
        in_specs=[tile, pl.BlockSpec((FFN_HALO, d), lambda i: (jnp.maximum(i * hb - 1, 0), 0)),
                  tile, pl.BlockSpec((FFN_HALO, d), lambda i: (jnp.minimum((i + 1) * hb, last_halo), 0)),
                  pl.BlockSpec((len(POOL_WINDOWS), gd, gd), lambda i: (0, 0, 0)),
                  pl.BlockSpec((1, d), lambda i: (0, 0))],
        out_specs=(tile, pl.BlockSpec((1, d), lambda i: (0, 0)),
                   pl.BlockSpec((len(POOL_WINDOWS), gd, gd), lambda i: (0, 0, 0))),
        name=name, compiler_params=_params(1))(y, y, dz, dz, wgrp, scale)


def _loss_head(xhat, rstd, g, b, target, name, tm=512):
    t, d = xhat.shape
    tm = min(tm, t)

    def body(xhat_ref, rstd_ref, g_ref, b_ref, tgt_ref, dr_ref, drb_ref, dg_ref, db_ref, sq_ref):
        first = pl.program_id(0) == 0
        xhat_t = xhat_ref[...]
        diff = xhat_t * g_ref[...] + b_ref[...] - tgt_ref[...]
        dr, dg, db = _ln_bwd(diff * (1.0 / d), xhat_t, rstd_ref[...], g_ref[...])
        dr_ref[...] = dr
        drb_ref[...] = dr.astype(BF16)
        _accumulate(dg_ref, dg, first)
        _accumulate(db_ref, db, first)
        _accumulate(sq_ref, jnp.sum(diff * diff, axis=0, keepdims=True), first)

    row = pl.BlockSpec((1, d), lambda i: (0, 0))
    tile = pl.BlockSpec((tm, d), lambda i: (i, 0))
    return pl.pallas_call(
        body,
        out_shape=(jax.ShapeDtypeStruct((t, d), F32), jax.ShapeDtypeStruct((t, d), BF16),
                   jax.ShapeDtypeStruct((1, d), F32), jax.ShapeDtypeStruct((1, d), F32),
                   jax.ShapeDtypeStruct((1, d), F32)),
        grid=(t // tm,),
        in_specs=[tile, pl.BlockSpec((tm, 1), lambda i: (i, 0)), row, row, tile],
        out_specs=(tile, tile, row, row, row),
        name=name, compiler_params=_params(1))(xhat, rstd, g, b, target)


def _my_place():
    return lax.axis_index("x"), lax.axis_index("y"), lax.axis_index("c")


def _flip(coord, bit):
    return 1 - coord if bit else coord


def _all_gather(arrays, name):
    n = len(arrays)

    def body(*refs):
        ins, outs = refs[:n], refs[n:2 * n]
        send_sems, recv_sems, local_sems = refs[2 * n:]
        x, y, c = _my_place()
        me, sibling = (x, y, c), (x, y, 1 - c)
        chips = [(1 - x, y), (x, 1 - y), (1 - x, 1 - y)]

        def copy(a, k, block, to, src=None):
            idx = 4 * block[0] + 2 * block[1] + block[2]
            return pltpu.make_async_remote_copy(
                src_ref=outs[a].at[idx] if src is None else src, dst_ref=outs[a].at[idx],
                send_sem=send_sems.at[a, k], recv_sem=recv_sems.at[a, k], device_id=to, device_id_type=MESH)

        mine, first, passed = [], [], []
        for a in range(n):
            cp = pltpu.make_async_copy(ins[a], outs[a].at[4 * x + 2 * y + c], local_sems.at[a])
            cp.start()
            mine.append(cp)
            first.append(copy(a, 0, me, sibling, src=ins[a]))
            first += [copy(a, 1 + j, me, (*chip, c), src=ins[a]) for j, chip in enumerate(chips)]
        for cp in first:
            cp.start()
        for j, chip in enumerate(chips):
            for a in range(n):
                copy(a, 1 + j, (*chip, c), me).wait_recv()
                cp = copy(a, 4 + j, (*chip, c), sibling)
                cp.start()
                passed.append(cp)
        for a in range(n):
            copy(a, 0, sibling, me).wait_recv()
            for j, chip in enumerate(chips):
                copy(a, 4 + j, (*chip, 1 - c), me).wait_recv()
        for cp in first + passed:
            cp.wait_send()
        for cp in mine:
            cp.wait()

    hbm = pl.BlockSpec(memory_space=pltpu.HBM)
    return pl.pallas_call(
        body, out_shape=tuple(jax.ShapeDtypeStruct((N_DEV,) + a.shape, a.dtype) for a in arrays),
        in_specs=[hbm] * n, out_specs=tuple([hbm] * n),
        scratch_shapes=[pltpu.SemaphoreType.DMA((n, 7)), pltpu.SemaphoreType.DMA((n, 7)),
                        pltpu.SemaphoreType.DMA((n,))],
        name=name)(*arrays)


def _exchange_pieces(pieces, name):
    def body(g_ref, o_ref, send_sems, recv_sems, local_sem):
        x, y, c = _my_place()
        me = 4 * x + 2 * y + c
        local = pltpu.make_async_copy(g_ref.at[me], o_ref.at[me], local_sem)
        local.start()
        peers = [(_flip(x, k & 4), _flip(y, k & 2), _flip(c, k & 1)) for k in range(1, N_DEV)]
        copies = []
        for k, peer in enumerate(peers):
            idx = 4 * peer[0] + 2 * peer[1] + peer[2]
            cp = pltpu.make_async_remote_copy(
                src_ref=g_ref.at[idx], dst_ref=o_ref.at[me], send_sem=send_sems.at[k], recv_sem=recv_sems.at[k],
                device_id=peer, device_id_type=MESH)
            cp.start()
            copies.append((cp, idx, peer))
        for k, (cp, idx, peer) in enumerate(copies):
            pltpu.make_async_remote_copy(
                src_ref=g_ref.at[idx], dst_ref=o_ref.at[idx], send_sem=send_sems.at[k], recv_sem=recv_sems.at[k],
                device_id=peer, device_id_type=MESH).wait_recv()
        for cp, _, _ in copies:
            cp.wait_send()
        local.wait()

    hbm = pl.BlockSpec(memory_space=pltpu.HBM)
    return pl.pallas_call(
        body, out_shape=jax.ShapeDtypeStruct(pieces.shape, pieces.dtype), in_specs=[hbm], out_specs=hbm,
        scratch_shapes=[pltpu.SemaphoreType.DMA((7,)), pltpu.SemaphoreType.DMA((7,)), pltpu.SemaphoreType.DMA],
        name=name)(pieces)


def _adamw(pieces, w, m, v, name, tr=128):
    r, lanes = w.shape
    tr = min(tr, r)
    c1 = 1.0 / (1.0 - ADAM_B1 ** ADAM_STEP)
    c2 = 1.0 / (1.0 - ADAM_B2 ** ADAM_STEP)

    def body(p_ref, w_ref, m_ref, v_ref, g_ref, d_ref, nm_ref, nv_ref):
        g = p_ref[0]
        for k in range(1, N_DEV):
            g = g + p_ref[k]
        nm = ADAM_B1 * m_ref[...] + (1.0 - ADAM_B1) * g
        nv = ADAM_B2 * v_ref[...] + (1.0 - ADAM_B2) * (g * g)
        g_ref[...] = g
        nm_ref[...] = nm
        nv_ref[...] = nv
        d_ref[...] = -ADAM_LR * ((nm * c1) / (jnp.sqrt(nv * c2) + ADAM_EPS) + ADAM_WD * w_ref[...])

    tile = pl.BlockSpec((tr, lanes), lambda i: (i, 0))
    out = jax.ShapeDtypeStruct((r, lanes), F32)
    return pl.pallas_call(
        body, out_shape=(out, out, out, out), grid=(r // tr,),
        in_specs=[pl.BlockSpec((N_DEV, tr, lanes), lambda i: (0, i, 0)), tile, tile, tile],
        out_specs=(tile, tile, tile, tile), name=name, compiler_params=_params(1))(pieces, w, m, v)


def _rows_of(numel, row_tile):
    rows = -(-numel // LANES)
    return -(-rows // row_tile) * row_tile


def _pack(flat_list, row_tile, lead=()):
    parts = []
    for a in flat_list:
        numel = a.shape[-1]
        rows = _rows_of(numel, row_tile)
        pad = [(0, 0)] * len(lead) + [(0, rows * LANES - numel)]
        parts.append(jnp.pad(a, pad).reshape(*lead, rows, LANES))
    return jnp.concatenate(parts, axis=len(lead))


def _unpack(buf, shapes, row_tile, lead=()):
    out, r0 = [], 0
    for shape in shapes:
        numel = 1
        for s in shape:
            numel *= s
        rows = _rows_of(numel, row_tile)
        part = lax.slice_in_dim(buf, r0, r0 + rows, axis=len(lead)).reshape(*lead, rows * LANES)
        out.append(lax.slice_in_dim(part, 0, numel, axis=len(lead)).reshape(*lead, *shape))
        r0 += rows
    return out


def _to_shards(full, axis):
    shape = full.shape
    cut = full.reshape(shape[:axis] + (N_DEV, shape[axis] // N_DEV) + shape[axis + 1:])
    return jnp.moveaxis(cut, axis, 0)


def _step(x, target, w, m, v):
    t, d = x.shape[1], x.shape[2]
    x2 = x.reshape(t, d)
    tgt2 = target.reshape(t, d)

    big = _pack([w[k].astype(BF16).reshape(-1) for k in GATHER_BF16], 16)
    small = _pack([w[k].reshape(-1) for k in GATHER_F32], 8)
    big_all, small_all = _all_gather([big, small], "all_gather_weights")
    gw = dict(zip(GATHER_BF16, _unpack(big_all, [w[k].shape for k in GATHER_BF16], 16, (N_DEV,))))
    gw.update(zip(GATHER_F32, _unpack(small_all, [w[k].shape for k in GATHER_F32], 8, (N_DEV,))))

    def full_cols(name, layer):
        a = gw[name][:, layer]
        if a.ndim == 2:
            return a.reshape(1, -1)
        return jnp.moveaxis(a, 0, 1).reshape(a.shape[1], -1)

    ones = jnp.ones((1, d), F32)
    zeros = jnp.zeros((1, d), F32)

    saved = []
    res, res_g, res_b = x2, ones, zeros
    xin = x2.astype(BF16)
    for i in range(DEPTH):
        kind, j = i % 3, i // 3
        sv = {"xin": xin, "kind": kind, "j": j}
        if kind == 0:
            w_in = gw["a_w_in"][:, j]
            w_out = gw["a_w_out"][:, j].reshape(1, d, d)
            dw, dwb = full_cols("a_dw", j), full_cols("a_dw_b", j)
            lg, lb = full_cols("a_ln_g", j), full_cols("a_ln_b", j)
            h1 = _mm_nn(xin, w_in, BF16, f"conv_in_{i}")
            s_act, q = _conv_fwd(h1, dw, dwb, lg, lb, f"conv_mix_{i}")
            sv.update(h1=h1, q=q, w_in=w_in, dw=dw, lg=lg, lb=lb)
        elif kind == 1:
            w_in = gw["b_w_in"][:, j]
            w_out = gw["b_w_out"][:, j].reshape(1, d, d)
            lg, lb = w["b_ln_g"][j].reshape(1, d), w["b_ln_b"][j].reshape(1, d)
            ws, bst = w["b_ws"][j], w["b_bs"][j].T
            h1 = _mm_nn(xin, w_in, BF16, f"sgu_in_{i}")
            s_act = _sgu_fwd(h1, lg, lb, ws, bst, f"sgu_mix_{i}")
            sv.update(h1=h1, w_in=w_in, lg=lg, lb=lb, ws=ws, bst=bst)
        else:
            w_in = gw["c_w_in"][:, j].reshape(1, d, d)
            w_out = gw["c_w_out"][:, j].reshape(1, d, d)
            wgrp = jnp.moveaxis(gw["c_w_grp"][:, j], 0, 1).reshape(4, d // 4, d // 4)
            scale = full_cols("c_scale", j)
            yp = _mm_nn(xin, w_in, F32, f"pool_in_{i}")[0]
            s_act = _pool_fwd(yp, wgrp, scale, f"pool_mix_{i}")
            sv.update(yp=yp, w_in=w_in, wgrp=wgrp, scale=scale)
        g1, b1 = w["ln1_g"][i].reshape(1, d), w["ln1_b"][i].reshape(1, d)
        xhat1, y1, rstd1 = _mm_res_ln(s_act.reshape(1, t, d), w_out, res, res_g, res_b, g1, b1, f"mix_out_ln_{i}")
        w_up = gw["f_w_up"][:, i]
        fdw = gw["f_dw"][:, i]
        n_ff = w_up.shape[-1]
        w_down = gw["f_w_down"][:, i].reshape(4, n_ff, d)
        u = _mm_nn(y1, w_up, BF16, f"ffn_up_{i}")
        a_act = _ffn_gate_fwd(u, fdw, f"ffn_gate_{i}")
        g2, b2 = w["ln2_g"][i].reshape(1, d), w["ln2_b"][i].reshape(1, d)
        xhat2, y2, rstd2 = _mm_res_ln(a_act, w_down, xhat1, g1, b1, g2, b2, f"ffn_down_ln_{i}")
        sv.update(s_act=s_act, w_out=w_out, xhat1=xhat1, y1=y1, rstd1=rstd1, g1=g1, u=u, a_act=a_act, w_up=w_up,
                  fdw=fdw, w_down=w_down, xhat2=xhat2, rstd2=rstd2, g2=g2, b2=b2)
        saved.append(sv)
        res, res_g, res_b, xin = xhat2, g2, b2, y2

    last = saved[-1]
    dr2, dr2b, dg2, db2, sq = _loss_head(last["xhat2"], last["rstd2"], last["g2"], last["b2"], tgt2, "loss_head")
    loss = lax.psum((0.5 / d) * jnp.sum(sq), ("x", "y", "c"))

    grads = {k: [None] * w[k].shape[0] for k in WEIGHTS}
    grad_x = None
    for i in reversed(range(DEPTH)):
        sv = saved[i]
        kind, j = sv["kind"], sv["j"]
        grads["ln2_g"][i], grads["ln2_b"][i] = dg2, db2
        da = _mm_nt_out(dr2b, sv["w_down"], f"ffn_da_{i}")
        grads["f_w_down"][i] = _to_shards(_mm_tn(sv["a_act"], dr2b.reshape(1, t, d), f"ffn_dwdown_{i}")
                                          .reshape(-1, d), 0)
        du, dfdw = _ffn_gate_bwd(sv["u"], da, sv["fdw"], f"ffn_gate_bwd_{i}")
        grads["f_dw"][i] = dfdw
        grads["f_w_up"][i] = _mm_tn(sv["y1"].reshape(1, t, d), du, f"ffn_dwup_{i}")
        dr1, dr1b, dg1, db1 = _mm_nt_lnb(du, sv["w_up"], dr2, sv["xhat1"], sv["rstd1"], sv["g1"], f"ffn_dx_ln_{i}")
        grads["ln1_g"][i], grads["ln1_b"][i] = dg1, db1
        ds = _mm_nt_out(dr1b, sv["w_out"], f"mix_ds_{i}")[0]
        dw_out = _to_shards(_mm_tn(sv["s_act"].reshape(1, t, d), dr1b.reshape(1, t, d), f"mix_dwout_{i}")[0], 0)
        xin3 = sv["xin"].reshape(1, t, d)
        if kind == 0:
            dh1, ddw, ddwb, dlg, dlb = _conv_bwd(ds, sv["q"], sv["h1"], sv["dw"], sv["lg"], sv["lb"],
                                                  f"conv_mix_bwd_{i}")
            grads["a_w_out"][j] = dw_out
            grads["a_dw"][j] = _to_shards(jnp.sum(ddw, axis=1), 1)
            grads["a_dw_b"][j] = _to_shards(ddwb[0], 0)
            grads["a_ln_g"][j] = _to_shards(dlg[0], 0)
            grads["a_ln_b"][j] = _to_shards(dlb[0], 0)
            grads["a_w_in"][j] = _mm_tn(xin3, dh1, f"conv_dwin_{i}")
            dh_in, w_in = dh1, sv["w_in"]
        elif kind == 1:
            dh1, dlg, dlb, dws, dbias = _sgu_bwd(sv["h1"], ds, sv["lg"], sv["lb"], sv["ws"], sv["bst"],
                                                  f"sgu_mix_bwd_{i}")
            grads["b_w_out"][j] = dw_out
            grads["b_ln_g"][j], grads["b_ln_b"][j] = dlg[0], dlb[0]
            grads["b_ws"][j] = dws
            grads["b_bs"][j] = jnp.sum(dbias.reshape(CHUNK, HEADS, CHUNK), axis=-1).T
            grads["b_w_in"][j] = _mm_tn(xin3, dh1, f"sgu_dwin_{i}")
            dh_in, w_in = dh1, sv["w_in"]
        else:
            dyp, dscale, dwgrp = _pool_bwd(sv["yp"], ds, sv["wgrp"], sv["scale"], f"pool_mix_bwd_{i}")
            grads["c_w_out"][j] = dw_out
            grads["c_scale"][j] = _to_shards(dscale[0], 0)
            grads["c_w_grp"][j] = _to_shards(dwgrp, 1)
            dh_in, w_in = dyp.reshape(1, t, d), sv["w_in"]
            grads["c_w_in"][j] = _to_shards(_mm_tn(xin3, dh_in, f"pool_dwin_{i}")[0], 0)
        if i > 0:
            prev = saved[i - 1]
            dr2, dr2b, dg2, db2 = _mm_nt_lnb(dh_in, w_in, dr1, prev["xhat2"], prev["rstd2"], prev["g2"],
                                              f"mix_dx_ln_{i}")
        else:
            grad_x = _mm_nt_res(dh_in, w_in, dr1, "mix_dx_0").reshape(x.shape)

    flat = []
    for k in WEIGHTS:
        if k in REPLICATED:
            full = jnp.stack([gk.reshape(w[k].shape[1:]) for gk in grads[k]], axis=0)
            flat.append(jnp.broadcast_to(full.reshape(1, -1), (N_DEV, full.size)))
        else:
            flat.append(jnp.stack(grads[k], axis=1).reshape(N_DEV, -1))
    pieces = _pack(flat, 8, (N_DEV,))
    rows = pieces.shape[1]
    pad_rows = -(-rows // 128) * 128 - rows
    pieces = jnp.pad(pieces, ((0, 0), (0, pad_rows), (0, 0)))
    received = _exchange_pieces(pieces, "exchange_gradients")

    def packed(tree):
        return jnp.pad(_pack([tree[k].reshape(-1) for k in WEIGHTS], 8), ((0, pad_rows), (0, 0)))

    g_sum, delta, new_m, new_v = _adamw(received, packed(w), packed(m), packed(v), "adamw")
    shapes = [w[k].shape for k in WEIGHTS]
    outs = []
    for buf in (g_sum, delta, new_m, new_v):
        outs += _unpack(buf, shapes, 8)
    return (loss, grad_x, *outs)


def kernel(x, a_w_in, a_dw, a_dw_b, a_ln_g, a_ln_b, a_w_out, b_w_in, b_ln_g, b_ln_b, b_ws, b_bs, b_w_out, c_w_in, c_w_grp, c_scale, c_w_out, f_w_up, f_dw, f_w_down, ln1_g, ln1_b, ln2_g, ln2_b, loss_target, m_a_w_in, m_a_dw, m_a_dw_b, m_a_ln_g, m_a_ln_b, m_a_w_out, m_b_w_in, m_b_ln_g, m_b_ln_b, m_b_ws, m_b_bs, m_b_w_out, m_c_w_in, m_c_w_grp, m_c_scale, m_c_w_out, m_f_w_up, m_f_dw, m_f_w_down, m_ln1_g, m_ln1_b, m_ln2_g, m_ln2_b, v_a_w_in, v_a_dw, v_a_dw_b, v_a_ln_g, v_a_ln_b, v_a_w_out, v_b_w_in, v_b_ln_g, v_b_ln_b, v_b_ws, v_b_bs, v_b_w_out, v_c_w_in, v_c_w_grp, v_c_scale, v_c_w_out, v_f_w_up, v_f_dw, v_f_w_down, v_ln1_g, v_ln1_b, v_ln2_g, v_ln2_b):
    w = dict(zip(WEIGHTS, (a_w_in, a_dw, a_dw_b, a_ln_g, a_ln_b, a_w_out, b_w_in, b_ln_g, b_ln_b, b_ws, b_bs, b_w_out,
                           c_w_in, c_w_grp, c_scale, c_w_out, f_w_up, f_dw, f_w_down, ln1_g, ln1_b, ln2_g, ln2_b)))
    m = dict(zip(WEIGHTS, (m_a_w_in, m_a_dw, m_a_dw_b, m_a_ln_g, m_a_ln_b, m_a_w_out, m_b_w_in, m_b_ln_g, m_b_ln_b,
                           m_b_ws, m_b_bs, m_b_w_out, m_c_w_in, m_c_w_grp, m_c_scale, m_c_w_out, m_f_w_up, m_f_dw,
                           m_f_w_down, m_ln1_g, m_ln1_b, m_ln2_g, m_ln2_b)))
    v = dict(zip(WEIGHTS, (v_a_w_in, v_a_dw, v_a_dw_b, v_a_ln_g, v_a_ln_b, v_a_w_out, v_b_w_in, v_b_ln_g, v_b_ln_b,
                           v_b_ws, v_b_bs, v_b_w_out, v_c_w_in, v_c_w_grp, v_c_scale, v_c_w_out, v_f_w_up, v_f_dw,
                           v_f_w_down, v_ln1_g, v_ln1_b, v_ln2_g, v_ln2_b)))
    return _step(x, loss_target, w, m, v)
```

```python
import functools

import jax
import jax.numpy as jnp
from jax import lax
from jax.experimental import pallas as pl
from jax.experimental.pallas import tpu as pltpu

N_DEV = 8
DEPTH = 4
ALPHA = float((2 * DEPTH) ** 0.25)
LN_EPS = 1e-5
CONV_W = 31
CONV_HALO = 32
FFN_HALO = 16
POOL_WINDOWS = (2, 4, 8, 16)
CHUNK = 128
HEADS = 8
LANES = 1024
ADAM_LR, ADAM_B1, ADAM_B2, ADAM_EPS, ADAM_WD, ADAM_STEP = 0.001, 0.9, 0.999, 1e-08, 0.01, 10
VMEM_LIMIT = 56 * 1024 * 1024
F32, BF16 = jnp.float32, jnp.bfloat16
MESH = pl.DeviceIdType.MESH

WEIGHTS = ['a_w_in', 'a_dw', 'a_dw_b', 'a_ln_g', 'a_ln_b', 'a_w_out', 'b_w_in', 'b_ln_g', 'b_ln_b', 'b_ws', 'b_bs',
           'b_w_out', 'c_w_in', 'c_w_grp', 'c_scale', 'c_w_out', 'f_w_up', 'f_dw', 'f_w_down', 'ln1_g', 'ln1_b',
           'ln2_g', 'ln2_b']
REPLICATED = ('b_ln_g', 'b_ln_b', 'b_ws', 'b_bs', 'ln1_g', 'ln1_b', 'ln2_g', 'ln2_b')
GATHER_BF16 = ('a_w_in', 'a_w_out', 'b_w_in', 'b_w_out', 'c_w_in', 'c_w_grp', 'c_w_out', 'f_w_up', 'f_w_down')
GATHER_F32 = ('a_dw', 'a_dw_b', 'a_ln_g', 'a_ln_b', 'c_scale', 'f_dw')


def _params(n_axes):
    return pltpu.CompilerParams(dimension_semantics=("arbitrary",) * n_axes, vmem_limit_bytes=VMEM_LIMIT)


def _resident(shape):
    zeros = (0,) * len(shape)
    return pl.BlockSpec(shape, lambda i: zeros, pipeline_mode=pl.Buffered(1))


def _sigmoid(x):
    return 1.0 / (1.0 + jnp.exp(-x))


def _gelu(x):
    return 0.5 * x * (1.0 + lax.erf(x * 0.7071067811865476))


def _gelu_grad(x):
    return 0.5 * (1.0 + lax.erf(x * 0.7071067811865476)) + x * jnp.exp(-0.5 * x * x) * 0.3989422804014327


def _ln_stats(r):
    mu = jnp.mean(r, axis=-1, keepdims=True)
    xc = r - mu
    var = jnp.mean(xc * xc, axis=-1, keepdims=True)
    rstd = lax.rsqrt(var + LN_EPS)
    return xc * rstd, rstd


def _ln_bwd(dy, xhat, rstd, g):
    dxhat = dy * g
    m1 = jnp.mean(dxhat, axis=-1, keepdims=True)
    m2 = jnp.mean(dxhat * xhat, axis=-1, keepdims=True)
    dr = rstd * (dxhat - m1 - xhat * m2)
    return dr, jnp.sum(dy * xhat, axis=0, keepdims=True), jnp.sum(dy, axis=0, keepdims=True)


def _accumulate(ref, value, first):
    @pl.when(first)
    def _():
        ref[...] = value

    @pl.when(jnp.logical_not(first))
    def _():
        ref[...] += value


def _dot(a, b):
    return jnp.dot(a, b, preferred_element_type=F32)


def _dot_nt(a, b):
    return lax.dot_general(a, b, (((1,), (1,)), ((), ())), preferred_element_type=F32)


def _dot_tn(a, b):
    return lax.dot_general(a, b, (((0,), (0,)), ((), ())), preferred_element_type=F32)


def _cat_lanes(h, lo, hi):
    return jnp.concatenate([h[s] for s in range(lo, hi)], axis=-1)


def _mm_nn(x, w, out_dtype, name, tm=512):
    t, k = x.shape
    s_n, _, n = w.shape
    tm = min(tm, t)

    def body(x_ref, w_ref, o_ref):
        x_tile = x_ref[...]
        for s in range(s_n):
            o_ref[s] = _dot(x_tile, w_ref[s]).astype(o_ref.dtype)

    return pl.pallas_call(
        body, out_shape=jax.ShapeDtypeStruct((s_n, t, n), out_dtype), grid=(t // tm,),
        in_specs=[pl.BlockSpec((tm, k), lambda i: (i, 0)), _resident((s_n, k, n))],
        out_specs=pl.BlockSpec((s_n, tm, n), lambda i: (0, i, 0)),
        name=name, compiler_params=_params(1))(x, w)


def _mm_res_ln(a, w, res, gp, bp, g, b, name, tm=512):
    s_n, t, ka = a.shape
    d = w.shape[-1]
    tm = min(tm, t)

    def body(a_ref, w_ref, res_ref, gp_ref, bp_ref, g_ref, b_ref, xhat_ref, y_ref, rstd_ref):
        acc = _dot(a_ref[0], w_ref[0])
        for s in range(1, s_n):
            acc += _dot(a_ref[s], w_ref[s])
        r = ALPHA * (res_ref[...] * gp_ref[...] + bp_ref[...]) + acc
        xhat, rstd = _ln_stats(r)
        xhat_ref[...] = xhat
        y_ref[...] = (xhat * g_ref[...] + b_ref[...]).astype(BF16)
        rstd_ref[...] = rstd

    row = pl.BlockSpec((1, d), lambda i: (0, 0))
    tile = pl.BlockSpec((tm, d), lambda i: (i, 0))
    return pl.pallas_call(
        body,
        out_shape=(jax.ShapeDtypeStruct((t, d), F32), jax.ShapeDtypeStruct((t, d), BF16),
                   jax.ShapeDtypeStruct((t, 1), F32)),
        grid=(t // tm,),
        in_specs=[pl.BlockSpec((s_n, tm, ka), lambda i: (0, i, 0)), _resident((s_n, ka, d)),
                  tile, row, row, row, row],
        out_specs=(tile, tile, pl.BlockSpec((tm, 1), lambda i: (i, 0))),
        name=name, compiler_params=_params(1))(a, w, res, gp, bp, g, b)


def _mm_nt_out(x, w, name, tm=512):
    t, n = x.shape
    s_n, k, _ = w.shape
    tm = min(tm, t)

    def body(x_ref, w_ref, o_ref):
        x_tile = x_ref[...]
        for s in range(s_n):
            o_ref[s] = _dot_nt(x_tile, w_ref[s]).astype(o_ref.dtype)

    return pl.pallas_call(
        body, out_shape=jax.ShapeDtypeStruct((s_n, t, k), BF16), grid=(t // tm,),
        in_specs=[pl.BlockSpec((tm, n), lambda i: (i, 0)), _resident((s_n, k, n))],
        out_specs=pl.BlockSpec((s_n, tm, k), lambda i: (0, i, 0)),
        name=name, compiler_params=_params(1))(x, w)


def _mm_nt_lnb(dh, w, drn, xhat, rstd, g, name, tm=512):
    s_n, t, n = dh.shape
    k = w.shape[1]
    tm = min(tm, t)

    def body(dh_ref, w_ref, drn_ref, xhat_ref, rstd_ref, g_ref, dr_ref, drb_ref, dg_ref, db_ref):
        first = pl.program_id(0) == 0
        acc = _dot_nt(dh_ref[0], w_ref[0])
        for s in range(1, s_n):
            acc += _dot_nt(dh_ref[s], w_ref[s])
        dy = acc + ALPHA * drn_ref[...]
        dr, dg, db = _ln_bwd(dy, xhat_ref[...], rstd_ref[...], g_ref[...])
        dr_ref[...] = dr
        drb_ref[...] = dr.astype(BF16)
        _accumulate(dg_ref, dg, first)
        _accumulate(db_ref, db, first)

    tile = pl.BlockSpec((tm, k), lambda i: (i, 0))
    row = pl.BlockSpec((1, k), lambda i: (0, 0))
    return pl.pallas_call(
        body,
        out_shape=(jax.ShapeDtypeStruct((t, k), F32), jax.ShapeDtypeStruct((t, k), BF16),
                   jax.ShapeDtypeStruct((1, k), F32), jax.ShapeDtypeStruct((1, k), F32)),
        grid=(t // tm,),
        in_specs=[pl.BlockSpec((s_n, tm, n), lambda i: (0, i, 0)), _resident((s_n, k, n)),
                  tile, tile, pl.BlockSpec((tm, 1), lambda i: (i, 0)), row],
        out_specs=(tile, tile, row, row),
        name=name, compiler_params=_params(1))(dh, w, drn, xhat, rstd, g)


def _mm_nt_res(dh, w, drn, name, tm=512):
    s_n, t, n = dh.shape
    k = w.shape[1]
    tm = min(tm, t)

    def body(dh_ref, w_ref, drn_ref, o_ref):
        acc = _dot_nt(dh_ref[0], w_ref[0])
        for s in range(1, s_n):
            acc += _dot_nt(dh_ref[s], w_ref[s])
        o_ref[...] = acc + ALPHA * drn_ref[...]

    tile = pl.BlockSpec((tm, k), lambda i: (i, 0))
    return pl.pallas_call(
        body, out_shape=jax.ShapeDtypeStruct((t, k), F32), grid=(t // tm,),
        in_specs=[pl.BlockSpec((s_n, tm, n), lambda i: (0, i, 0)), _resident((s_n, k, n)), tile],
        out_specs=tile, name=name, compiler_params=_params(1))(dh, w, drn)


def _mm_tn(lhs, rhs, name, tm=2048):
    sl, t, kl = lhs.shape
    sr, _, n = rhs.shape
    s_n = max(sl, sr)
    tm = min(tm, t)

    def body(l_ref, r_ref, o_ref):
        _accumulate(o_ref, _dot_tn(l_ref[...], r_ref[...]), pl.program_id(1) == 0)

    return pl.pallas_call(
        body, out_shape=jax.ShapeDtypeStruct((s_n, kl, n), F32), grid=(s_n, t // tm),
        in_specs=[pl.BlockSpec((None, tm, kl), (lambda s, i: (s, i, 0)) if sl > 1 else (lambda s, i: (0, i, 0))),
                  pl.BlockSpec((None, tm, n), (lambda s, i: (s, i, 0)) if sr > 1 else (lambda s, i: (0, i, 0)))],
        out_specs=pl.BlockSpec((None, kl, n), lambda s, i: (s, 0, 0)),
        name=name, compiler_params=_params(2))(lhs, rhs)


def _glu(h):
    return _cat_lanes(h, 0, 4).astype(F32) * _sigmoid(_cat_lanes(h, 4, 8).astype(F32))


def _tap_loop(src_ref, dst_ref, weight_ref, rows, offset_of_tap, tap_of_weight, row_block=64):
    d = dst_ref.shape[-1]

    def block(cb, carry):
        c0 = pl.multiple_of(cb * 128, 128)
        for r0 in range(0, rows, row_block):
            acc = jnp.zeros((row_block, 128), F32)
            for k in range(CONV_W):
                wk = weight_ref[pl.ds(tap_of_weight(k), 1), pl.ds(c0, 128)]
                acc += wk * src_ref[pl.ds(r0 + offset_of_tap(k), row_block), pl.ds(c0, 128)]
            dst_ref[pl.ds(r0, row_block), pl.ds(c0, 128)] = acc
        return carry

    lax.fori_loop(0, d // 128, block, 0)


def _conv_fwd(h1, dw, dwb, g, b, name, tm=256):
    _, t, _ = h1.shape
    d = dw.shape[-1]
    tm = min(tm, t)
    hb = tm // CONV_HALO

    def body(h_ref, halo_ref, dw_ref, dwb_ref, g_ref, b_ref, s_ref, q_ref, ext_ref):
        i = pl.program_id(0)
        ext_ref[pl.ds(0, CONV_HALO), :] = _glu(halo_ref[...]) * (i > 0).astype(F32)
        ext_ref[pl.ds(CONV_HALO, tm), :] = _glu(h_ref[...])
        _tap_loop(ext_ref, q_ref, dw_ref, tm, lambda k: 2 + k, lambda k: k)
        q = q_ref[...] + dwb_ref[...]
        q_ref[...] = q
        qhat, _ = _ln_stats(q)
        z = qhat * g_ref[...] + b_ref[...]
        s_ref[...] = (z * _sigmoid(z)).astype(BF16)

    row = pl.BlockSpec((1, d), lambda i: (0, 0))
    tile = pl.BlockSpec((tm, d), lambda i: (i, 0))
    return pl.pallas_call(
        body, out_shape=(jax.ShapeDtypeStruct((t, d), BF16), jax.ShapeDtypeStruct((t, d), F32)), grid=(t // tm,),
        in_specs=[pl.BlockSpec((8, tm, 256), lambda i: (0, i, 0)),
                  pl.BlockSpec((8, CONV_HALO, 256), lambda i: (0, jnp.maximum(i * hb - 1, 0), 0)),
                  pl.BlockSpec((CONV_W, d), lambda i: (0, 0)), row, row, row],
        out_specs=(tile, tile), scratch_shapes=[pltpu.VMEM((tm + CONV_HALO, d), F32)],
        name=name, compiler_params=_params(1))(h1, h1, dw, dwb, g, b)


def _conv_bwd(ds, q, h1, dw, g, b, name, tm=256):
    _, t, _ = h1.shape
    d = dw.shape[-1]
    tm = min(tm, t)
    hb = tm // CONV_HALO
    n_t = t // tm
    last_halo = t // CONV_HALO - 1

    def body(ds_ref, dsn_ref, q_ref, qn_ref, h_ref, hp_ref, dw_ref, g_ref, b_ref,
             dh_ref, ddw_ref, ddwb_ref, dg_ref, db_ref, dq_ref, p_ref, dp_ref):
        i = pl.program_id(0)
        first = i == 0
        valid = (i < n_t - 1).astype(F32)

        def dq_rows(ds_rows, q_rows, scale):
            qhat, rstd = _ln_stats(q_rows)
            z = qhat * g_ref[...] + b_ref[...]
            sg = _sigmoid(z)
            dz = ds_rows.astype(F32) * (sg * (1.0 + z * (1.0 - sg))) * scale
            dq, dg, db = _ln_bwd(dz, qhat, rstd, g_ref[...])
            return dq, dg, db

        dq, dg, db = dq_rows(ds_ref[...], q_ref[...], 1.0)
        dq_ref[pl.ds(0, tm), :] = dq
        dq_ref[pl.ds(tm, CONV_HALO), :] = dq_rows(dsn_ref[...], qn_ref[...], valid)[0]
        _accumulate(dg_ref, dg, first)
        _accumulate(db_ref, db, first)
        _accumulate(ddwb_ref, jnp.sum(dq, axis=0, keepdims=True), first)

        p_ref[pl.ds(0, CONV_HALO), :] = _glu(hp_ref[...]) * (i > 0).astype(F32)
        p_ref[pl.ds(CONV_HALO, tm), :] = _glu(h_ref[...])

        _tap_loop(dq_ref, dp_ref, dw_ref, tm, lambda o: o, lambda o: CONV_W - 1 - o)

        @pl.when(first)
        def _():
            ddw_ref[...] = jnp.zeros_like(ddw_ref)

        row_block = 64

        def block(cb, carry):
            c0 = pl.multiple_of(cb * 128, 128)
            for r0 in range(0, tm, row_block):
                dqb = dq_ref[pl.ds(r0, row_block), pl.ds(c0, 128)]
                for k in range(CONV_W):
                    prod = dqb * p_ref[pl.ds(r0 + 2 + k, row_block), pl.ds(c0, 128)]
                    ddw_ref[k, :, pl.ds(c0, 128)] += jnp.sum(prod.reshape(row_block // 8, 8, 128), axis=0)
            return carry

        lax.fori_loop(0, d // 128, block, 0)

        h = h_ref[...]
        a = _cat_lanes(h, 0, 4).astype(F32)
        sg = _sigmoid(_cat_lanes(h, 4, 8).astype(F32))
        dp = dp_ref[...]
        da = (dp * sg).astype(BF16)
        dgate = (dp * a * sg * (1.0 - sg)).astype(BF16)
        for s in range(4):
            dh_ref[s] = da[:, s * 256:(s + 1) * 256]
            dh_ref[4 + s] = dgate[:, s * 256:(s + 1) * 256]

    row = pl.BlockSpec((1, d), lambda i: (0, 0))
    tile = pl.BlockSpec((tm, d), lambda i: (i, 0))
    nxt = pl.BlockSpec((CONV_HALO, d), lambda i: (jnp.minimum((i + 1) * hb, last_halo), 0))
    return pl.pallas_call(
        body,
        out_shape=(jax.ShapeDtypeStruct((8, t, 256), BF16), jax.ShapeDtypeStruct((CONV_W, 8, d), F32),
                   jax.ShapeDtypeStruct((1, d), F32), jax.ShapeDtypeStruct((1, d), F32),
                   jax.ShapeDtypeStruct((1, d), F32)),
        grid=(n_t,),
        in_specs=[tile, nxt, tile, nxt,
                  pl.BlockSpec((8, tm, 256), lambda i: (0, i, 0)),
                  pl.BlockSpec((8, CONV_HALO, 256), lambda i: (0, jnp.maximum(i * hb - 1, 0), 0)),
                  pl.BlockSpec((CONV_W, d), lambda i: (0, 0)), row, row],
        out_specs=(pl.BlockSpec((8, tm, 256), lambda i: (0, i, 0)),
                   pl.BlockSpec((CONV_W, 8, d), lambda i: (0, 0, 0)), row, row, row),
        scratch_shapes=[pltpu.VMEM((tm + CONV_HALO, d), F32), pltpu.VMEM((tm + CONV_HALO, d), F32),
                        pltpu.VMEM((tm, d), F32)],
        name=name, compiler_params=_params(1))(ds, ds, q, q, h1, h1, dw, g, b)


def _conv3(ext, dw):
    e1 = pltpu.roll(ext, 1, 0)
    e2 = pltpu.roll(ext, 2, 0)
    return dw[2:3] * ext + dw[1:2] * e1 + dw[0:1] * e2, e1, e2


def _ffn_fwd(x, w_up, fdw, w_down, res, gp, bp, g, b, name, tm=512):
    t, d = x.shape
    _, _, n = w_up.shape
    tm = min(tm, t)
    carry_rows = 8

    def body(x_ref, wu_ref, dw_ref, wd_ref, res_ref, gp_ref, bp_ref, g_ref, b_ref,
             u_ref, a_ref, xhat_ref, y_ref, rstd_ref, carry_ref):
        @pl.when(pl.program_id(0) == 0)
        def _():
            carry_ref[...] = jnp.zeros_like(carry_ref)

        x_tile = x_ref[...]
        acc = None
        for j in range(4):
            h = []
            for s in (j, 4 + j):
                ub = _dot(x_tile, wu_ref[s]).astype(BF16)
                u_ref[s] = ub
                uf = ub.astype(F32)
                ext = jnp.concatenate([carry_ref[s], uf], axis=0)
                carry_ref[s] = uf[tm - carry_rows:]
                h.append(_conv3(ext, dw_ref[s])[0][carry_rows:])
            a = (h[0] * _sigmoid(h[0]) * h[1]).astype(BF16)
            a_ref[j] = a
            part = _dot(a, wd_ref[j])
            acc = part if acc is None else acc + part
        r = ALPHA * (res_ref[...] * gp_ref[...] + bp_ref[...]) + acc
        xhat, rstd = _ln_stats(r)
        xhat_ref[...] = xhat
        y_ref[...] = (xhat * g_ref[...] + b_ref[...]).astype(BF16)
        rstd_ref[...] = rstd

    row = pl.BlockSpec((1, d), lambda i: (0, 0))
    tile = pl.BlockSpec((tm, d), lambda i: (i, 0))
    return pl.pallas_call(
        body,
        out_shape=(jax.ShapeDtypeStruct((8, t, n), BF16), jax.ShapeDtypeStruct((4, t, n), BF16),
                   jax.ShapeDtypeStruct((t, d), F32), jax.ShapeDtypeStruct((t, d), BF16),
                   jax.ShapeDtypeStruct((t, 1), F32)),
        grid=(t // tm,),
        in_specs=[tile, _resident((8, d, n)), pl.BlockSpec((8, 3, n), lambda i: (0, 0, 0)), _resident((4, n, d)),
                  tile, row, row, row, row],
        out_specs=(pl.BlockSpec((8, tm, n), lambda i: (0, i, 0)), pl.BlockSpec((4, tm, n), lambda i: (0, i, 0)),
                   tile, tile, pl.BlockSpec((tm, 1), lambda i: (i, 0))),
        scratch_shapes=[pltpu.VMEM((8, carry_rows, n), F32)],
        name=name, compiler_params=_params(1))(x, w_up, fdw, w_down, res, gp, bp, g, b)


def _ffn_gate_bwd(u, da, fdw, name, tm=256):
    _, t, n = u.shape
    tm = min(tm, t)
    hb = tm // FFN_HALO
    n_t = t // tm
    last_halo = t // FFN_HALO - 1
    u4 = u.reshape(2, 4, t, n)
    fdw4 = fdw.reshape(2, 4, 3, n)

    def body(u_ref, up_ref, un_ref, da_ref, dan_ref, dw_ref, du_ref, ddw_ref):
        i = pl.program_id(1)
        keep_prev = (i > 0).astype(F32)
        keep_next = (i < n_t - 1).astype(F32)
        h, taps = [], []
        for p in range(2):
            ext = jnp.concatenate([up_ref[p].astype(F32) * keep_prev, u_ref[p].astype(F32),
                                   un_ref[p].astype(F32) * keep_next], axis=0)
            hp, e1, e2 = _conv3(ext, dw_ref[p])
            h.append(hp[FFN_HALO:])
            taps.append((e2, e1, ext))
        hg, hv = h
        sg = _sigmoid(hg)
        da_ext = jnp.concatenate([da_ref[...].astype(F32), dan_ref[...].astype(F32) * keep_next], axis=0)
        dh = (da_ext * hv * (sg * (1.0 + hg * (1.0 - sg))), da_ext * hg * sg)
        rows = tm + FFN_HALO
        for p in range(2):
            dwp = dw_ref[p]
            d1 = pltpu.roll(dh[p], rows - 1, 0)
            d2 = pltpu.roll(dh[p], rows - 2, 0)
            du_ref[p] = (dwp[2:3] * dh[p] + dwp[1:2] * d1 + dwp[0:1] * d2)[:tm].astype(BF16)
            dht = dh[p][:tm]
            part = jnp.concatenate(
                [jnp.sum(dht * taps[p][k][FFN_HALO:FFN_HALO + tm], axis=0, keepdims=True) for k in range(3)], axis=0)
            _accumulate(ddw_ref.at[p], part, i == 0)

    tile = pl.BlockSpec((2, None, tm, n), lambda j, i: (0, j, i, 0))
    prev = pl.BlockSpec((2, None, FFN_HALO, n), lambda j, i: (0, j, jnp.maximum(i * hb - 1, 0), 0))
    nxt = pl.BlockSpec((2, None, FFN_HALO, n), lambda j, i: (0, j, jnp.minimum((i + 1) * hb, last_halo), 0))
    du, ddw = pl.pallas_call(
        body, out_shape=(jax.ShapeDtypeStruct((2, 4, t, n), BF16), jax.ShapeDtypeStruct((2, 4, 3, n), F32)),
        grid=(4, n_t),
        in_specs=[tile, prev, nxt,
                  pl.BlockSpec((None, tm, n), lambda j, i: (j, i, 0)),
                  pl.BlockSpec((None, FFN_HALO, n), lambda j, i: (j, jnp.minimum((i + 1) * hb, last_halo), 0)),
                  pl.BlockSpec((2, None, 3, n), lambda j, i: (0, j, 0, 0))],
        out_specs=(tile, pl.BlockSpec((2, None, 3, n), lambda j, i: (0, j, 0, 0))),
        name=name, compiler_params=_params(2))(u4, u4, u4, da, da, fdw4)
    return du.reshape(8, t, n), ddw.reshape(8, 3, n)


def _tril_mask():
    r = lax.broadcasted_iota(jnp.int32, (CHUNK, CHUNK), 0)
    c = lax.broadcasted_iota(jnp.int32, (CHUNK, CHUNK), 1)
    return (r >= c).astype(F32)


def _sgu_fwd(h1, g, b, ws, bst, name, tm=256):
    _, t, _ = h1.shape
    d = g.shape[-1]
    tm = min(tm, t)

    def body(h_ref, g_ref, b_ref, ws_ref, bst_ref, m_ref):
        h = h_ref[...]
        u = _gelu(_cat_lanes(h, 0, 4).astype(F32))
        v = _gelu(_cat_lanes(h, 4, 8).astype(F32))
        vn = (_ln_stats(v)[0] * g_ref[...] + b_ref[...]).astype(BF16)
        mask = _tril_mask()
        for hh in range(HEADS):
            cols = slice(hh * CHUNK, (hh + 1) * CHUNK)
            wm = (ws_ref[hh] * mask).astype(BF16)
            bias = bst_ref[:, hh:hh + 1]
            for c in range(tm // CHUNK):
                rows = slice(c * CHUNK, (c + 1) * CHUNK)
                sblk = _dot(wm, vn[rows, cols]) + bias
                m_ref[rows, cols] = (u[rows, cols] * sblk).astype(BF16)

    row = pl.BlockSpec((1, d), lambda i: (0, 0))
    return pl.pallas_call(
        body, out_shape=jax.ShapeDtypeStruct((t, d), BF16), grid=(t // tm,),
        in_specs=[pl.BlockSpec((8, tm, 256), lambda i: (0, i, 0)), row, row,
                  pl.BlockSpec((HEADS, CHUNK, CHUNK), lambda i: (0, 0, 0)),
                  pl.BlockSpec((CHUNK, HEADS), lambda i: (0, 0))],
        out_specs=pl.BlockSpec((tm, d), lambda i: (i, 0)),
        name=name, compiler_params=_params(1))(h1, g, b, ws, bst)


def _sgu_bwd(h1, dm, g, b, ws, bst, name, tm=256):
    _, t, _ = h1.shape
    d = g.shape[-1]
    tm = min(tm, t)

    def body(h_ref, dm_ref, g_ref, b_ref, ws_ref, bst_ref, dh_ref, dg_ref, db_ref, dws_ref, dbias_ref,
             du_ref, dvn_ref):
        first = pl.program_id(0) == 0
        h = h_ref[...]
        zu = _cat_lanes(h, 0, 4).astype(F32)
        zv = _cat_lanes(h, 4, 8).astype(F32)
        u = _gelu(zu)
        vhat, rstd = _ln_stats(_gelu(zv))
        vn = (vhat * g_ref[...] + b_ref[...]).astype(BF16)
        dm = dm_ref[...].astype(F32)
        mask = _tril_mask()

        @pl.when(first)
        def _():
            dws_ref[...] = jnp.zeros_like(dws_ref)
            dbias_ref[...] = jnp.zeros_like(dbias_ref)

        for hh in range(HEADS):
            cols = slice(hh * CHUNK, (hh + 1) * CHUNK)
            wm = (ws_ref[hh] * mask).astype(BF16)
            bias = bst_ref[:, hh:hh + 1]
            for c in range(tm // CHUNK):
                rows = slice(c * CHUNK, (c + 1) * CHUNK)
                vb = vn[rows, cols]
                sblk = _dot(wm, vb) + bias
                dmb = dm[rows, cols]
                du_ref[rows, cols] = dmb * sblk
                dsb = dmb * u[rows, cols]
                dbias_ref[:, cols] += dsb
                dsb16 = dsb.astype(BF16)
                dws_ref[hh] += _dot_nt(dsb16, vb) * mask
                dvn_ref[rows, cols] = _dot_tn(wm, dsb16)

        dvn = dvn_ref[...]
        dv, dg, db = _ln_bwd(dvn, vhat, rstd, g_ref[...])
        _accumulate(dg_ref, dg, first)
        _accumulate(db_ref, db, first)
        dzu = (du_ref[...] * _gelu_grad(zu)).astype(BF16)
        dzv = (dv * _gelu_grad(zv)).astype(BF16)
        for s in range(4):
            dh_ref[s] = dzu[:, s * 256:(s + 1) * 256]
            dh_ref[4 + s] = dzv[:, s * 256:(s + 1) * 256]

    row = pl.BlockSpec((1, d), lambda i: (0, 0))
    tile = pl.BlockSpec((tm, d), lambda i: (i, 0))
    h_tile = pl.BlockSpec((8, tm, 256), lambda i: (0, i, 0))
    return pl.pallas_call(
        body,
        out_shape=(jax.ShapeDtypeStruct((8, t, 256), BF16), jax.ShapeDtypeStruct((1, d), F32),
                   jax.ShapeDtypeStruct((1, d), F32), jax.ShapeDtypeStruct((HEADS, CHUNK, CHUNK), F32),
                   jax.ShapeDtypeStruct((CHUNK, d), F32)),
        grid=(t // tm,),
        in_specs=[h_tile, tile, row, row, pl.BlockSpec((HEADS, CHUNK, CHUNK), lambda i: (0, 0, 0)),
                  pl.BlockSpec((CHUNK, HEADS), lambda i: (0, 0))],
        out_specs=(h_tile, row, row, pl.BlockSpec((HEADS, CHUNK, CHUNK), lambda i: (0, 0, 0)),
                   pl.BlockSpec((CHUNK, d), lambda i: (0, 0))),
        scratch_shapes=[pltpu.VMEM((tm, d), F32), pltpu.VMEM((tm, d), F32)],
        name=name, compiler_params=_params(1))(h1, dm, g, b, ws, bst)


def _pool_minus_self(ext, first_token, grp):
    s = ext
    for step in range(grp + 1):
        s = s + pltpu.roll(s, 1 << step, 0)
    rows = ext.shape[0] - FFN_HALO
    tok = first_token + lax.broadcasted_iota(jnp.int32, (rows, 1), 0)
    count = jnp.minimum(tok + 1, POOL_WINDOWS[grp]).astype(F32)
    return s[FFN_HALO:] / count - ext[FFN_HALO:]


def _pool_fwd(y, wgrp, scale, name, tm=256):
    t, d = y.shape
    tm = min(tm, t)
    hb = tm // FFN_HALO
    gd = d // len(POOL_WINDOWS)

    def body(y_ref, yp_ref, w_ref, sc_ref, z_ref):
        i = pl.program_id(0)
        ext = jnp.concatenate([yp_ref[...] * (i > 0).astype(F32), y_ref[...]], axis=0)
        for grp in range(len(POOL_WINDOWS)):
            cols = slice(grp * gd, (grp + 1) * gd)
            p = _pool_minus_self(ext[:, cols], i * tm, grp)
            z_ref[:, cols] = (_dot(p.astype(BF16), w_ref[grp]) * sc_ref[:, cols]).astype(BF16)

    return pl.pallas_call(
        body, out_shape=jax.ShapeDtypeStruct((t, d), BF16), grid=(t // tm,),
        in_specs=[pl.BlockSpec((tm, d), lambda i: (i, 0)),
                  pl.BlockSpec((FFN_HALO, d), lambda i: (jnp.maximum(i * hb - 1, 0), 0)),
                  pl.BlockSpec((len(POOL_WINDOWS), gd, gd), lambda i: (0, 0, 0)),
                  pl.BlockSpec((1, d), lambda i: (0, 0))],
        out_specs=pl.BlockSpec((tm, d), lambda i: (i, 0)),
        name=name, compiler_params=_params(1))(y, y, wgrp, scale)


def _pool_bwd(y, dz, wgrp, scale, name, tm=256):
    t, d = y.shape
    tm = min(tm, t)
    hb = tm // FFN_HALO
    n_t = t // tm
    last_halo = t // FFN_HALO - 1
    gd = d // len(POOL_WINDOWS)
    rows = tm + FFN_HALO

    def body(y_ref, yp_ref, dz_ref, dzn_ref, w_ref, sc_ref, dy_ref, dsc_ref, dw_ref):
        i = pl.program_id(0)
        first = i == 0
        ext = jnp.concatenate([yp_ref[...] * (i > 0).astype(F32), y_ref[...]], axis=0)
        dz_ext = jnp.concatenate([dz_ref[...].astype(F32), dzn_ref[...].astype(F32) * (i < n_t - 1).astype(F32)],
                                 axis=0)
        tok = i * tm + lax.broadcasted_iota(jnp.int32, (rows, 1), 0)
        dsc = []
        for grp in range(len(POOL_WINDOWS)):
            cols = slice(grp * gd, (grp + 1) * gd)
            p16 = _pool_minus_self(ext[:, cols], i * tm, grp).astype(BF16)
            zg = _dot(p16, w_ref[grp])
            dsc.append(jnp.sum(dz_ext[:tm, cols] * zg, axis=0, keepdims=True))
            dzg = (dz_ext[:, cols] * sc_ref[:, cols]).astype(BF16)
            _accumulate(dw_ref.at[grp], _dot_tn(p16, dzg[:tm]), first)
            dp = _dot_nt(dzg, w_ref[grp])
            s = dp / jnp.minimum(tok + 1, POOL_WINDOWS[grp]).astype(F32)
            for step in range(grp + 1):
                s = s + pltpu.roll(s, rows - (1 << step), 0)
            dy_ref[:, cols] = (s[:tm] - dp[:tm]).astype(BF16)
        _accumulate(dsc_ref, jnp.concatenate(dsc, axis=-1), first)

    tile = pl.BlockSpec((tm, d), lambda i: (i, 0))
    return pl.pallas_call(
        body,
        out_shape=(jax.ShapeDtypeStruct((t, d), BF16), jax.ShapeDtypeStruct((1, d), F32),
                   jax.ShapeDtypeStruct((len(POOL_WINDOWS), gd, gd), F32)),
        grid=(n_t,),
        in_specs=[tile, pl.BlockSpec((FFN_HALO, d), lambda i: (jnp.maximum(i * hb - 1, 0), 0)),
                  tile, pl.BlockSpec((FFN_HALO, d), lambda i: (jnp.minimum((i + 1) * hb, last_halo), 0)),
                  pl.BlockSpec((len(POOL_WINDOWS), gd, gd), lambda i: (0, 0, 0)),
                  pl.BlockSpec((1, d), lambda i: (0, 0))],
        out_specs=(tile, pl.BlockSpec((1, d), lambda i: (0, 0)),
                   pl.BlockSpec((len(POOL_WINDOWS), gd, gd), lambda i: (0, 0, 0))),
        name=name, compiler_params=_params(1))(y, y, dz, dz, wgrp, scale)


def _loss_head(xhat, rstd, g, b, target, name, tm=512):
    t, d = xhat.shape
    tm = min(tm, t)

    def body(xhat_ref, rstd_ref, g_ref, b_ref, tgt_ref, dr_ref, drb_ref, dg_ref, db_ref, sq_ref):
        first = pl.program_id(0) == 0
        xhat_t = xhat_ref[...]
        diff = xhat_t * g_ref[...] + b_ref[...] - tgt_ref[...]
        dr, dg, db = _ln_bwd(diff * (1.0 / d), xhat_t, rstd_ref[...], g_ref[...])
        dr_ref[...] = dr
        drb_ref[...] = dr.astype(BF16)
        _accumulate(dg_ref, dg, first)
        _accumulate(db_ref, db, first)
        _accumulate(sq_ref, jnp.sum(diff * diff, axis=0, keepdims=True), first)

    row = pl.BlockSpec((1, d), lambda i: (0, 0))
    tile = pl.BlockSpec((tm, d), lambda i: (i, 0))
    return pl.pallas_call(
        body,
        out_shape=(jax.ShapeDtypeStruct((t, d), F32), jax.ShapeDtypeStruct((t, d), BF16),
                   jax.ShapeDtypeStruct((1, d), F32), jax.ShapeDtypeStruct((1, d), F32),
                   jax.ShapeDtypeStruct((1, d), F32)),
        grid=(t // tm,),
        in_specs=[tile, pl.BlockSpec((tm, 1), lambda i: (i, 0)), row, row, tile],
        out_specs=(tile, tile, row, row, row),
        name=name, compiler_params=_params(1))(xhat, rstd, g, b, target)


def _my_place():
    return lax.axis_index("x"), lax.axis_index("y"), lax.axis_index("c")


def _flip(coord, bit):
    return 1 - coord if bit else coord


def _all_gather(arrays, name):
    n = len(arrays)

    def body(*refs):
        ins, outs = refs[:n], refs[n:2 * n]
        send_sems, recv_sems, local_sems = refs[2 * n:]
        x, y, c = _my_place()
        me, sibling = (x, y, c), (x, y, 1 - c)
        chips = [(1 - x, y), (x, 1 - y), (1 - x, 1 - y)]

        def copy(a, k, block, to, src=None):
            idx = 4 * block[0] + 2 * block[1] + block[2]
            return pltpu.make_async_remote_copy(
                src_ref=outs[a].at[idx] if src is None else src, dst_ref=outs[a].at[idx],
                send_sem=send_sems.at[a, k], recv_sem=recv_sems.at[a, k], device_id=to, device_id_type=MESH)

        mine, first, passed = [], [], []
        for a in range(n):
            cp = pltpu.make_async_copy(ins[a], outs[a].at[4 * x + 2 * y + c], local_sems.at[a])
            cp.start()
            mine.append(cp)
            first.append(copy(a, 0, me, sibling, src=ins[a]))
            first += [copy(a, 1 + j, me, (*chip, c), src=ins[a]) for j, chip in enumerate(chips)]
        for cp in first:
            cp.start()
        for j, chip in enumerate(chips):
            for a in range(n):
                copy(a, 1 + j, (*chip, c), me).wait_recv()
                cp = copy(a, 4 + j, (*chip, c), sibling)
                cp.start()
                passed.append(cp)
        for a in range(n):
            copy(a, 0, sibling, me).wait_recv()
            for j, chip in enumerate(chips):
                copy(a, 4 + j, (*chip, 1 - c), me).wait_recv()
        for cp in first + passed:
            cp.wait_send()
        for cp in mine:
            cp.wait()

    hbm = pl.BlockSpec(memory_space=pltpu.HBM)
    return pl.pallas_call(
        body, out_shape=tuple(jax.ShapeDtypeStruct((N_DEV,) + a.shape, a.dtype) for a in arrays),
        in_specs=[hbm] * n, out_specs=tuple([hbm] * n),
        scratch_shapes=[pltpu.SemaphoreType.DMA((n, 7)), pltpu.SemaphoreType.DMA((n, 7)),
                        pltpu.SemaphoreType.DMA((n,))],
        name=name)(*arrays)


def _exchange_pieces(arrays, name):
    n = len(arrays)

    def body(*refs):
        ins, outs = refs[:n], refs[n:2 * n]
        send_sems, recv_sems, local_sems = refs[2 * n:]
        x, y, c = _my_place()
        me = 4 * x + 2 * y + c
        peers = [(_flip(x, k & 4), _flip(y, k & 2), _flip(c, k & 1)) for k in range(1, N_DEV)]
        slots = [4 * p[0] + 2 * p[1] + p[2] for p in peers]
        local, sent = [], []
        for a in range(n):
            cp = pltpu.make_async_copy(ins[a].at[me], outs[a].at[me], local_sems.at[a])
            cp.start()
            local.append(cp)
        for k, peer in enumerate(peers):
            for a in range(n):
                cp = pltpu.make_async_remote_copy(
                    src_ref=ins[a].at[slots[k]], dst_ref=outs[a].at[me], send_sem=send_sems.at[a, k],
                    recv_sem=recv_sems.at[a, k], device_id=peer, device_id_type=MESH)
                cp.start()
                sent.append(cp)
        for k, peer in enumerate(peers):
            for a in range(n):
                pltpu.make_async_remote_copy(
                    src_ref=ins[a].at[slots[k]], dst_ref=outs[a].at[slots[k]], send_sem=send_sems.at[a, k],
                    recv_sem=recv_sems.at[a, k], device_id=peer, device_id_type=MESH).wait_recv()
        for cp in sent:
            cp.wait_send()
        for cp in local:
            cp.wait()

    hbm = pl.BlockSpec(memory_space=pltpu.HBM)
    return pl.pallas_call(
        body, out_shape=tuple(jax.ShapeDtypeStruct(a.shape, a.dtype) for a in arrays),
        in_specs=[hbm] * n, out_specs=tuple([hbm] * n),
        scratch_shapes=[pltpu.SemaphoreType.DMA((n, 7)), pltpu.SemaphoreType.DMA((n, 7)),
                        pltpu.SemaphoreType.DMA((n,))],
        name=name)(*arrays)


def _adamw(pieces, w, m, v, name, tr=128):
    r, lanes = w.shape
    tr = min(tr, r)
    c1 = 1.0 / (1.0 - ADAM_B1 ** ADAM_STEP)
    c2 = 1.0 / (1.0 - ADAM_B2 ** ADAM_STEP)

    def body(p_ref, w_ref, m_ref, v_ref, g_ref, d_ref, nm_ref, nv_ref):
        g = p_ref[0].astype(F32)
        for k in range(1, N_DEV):
            g = g + p_ref[k].astype(F32)
        nm = ADAM_B1 * m_ref[...] + (1.0 - ADAM_B1) * g
        nv = ADAM_B2 * v_ref[...] + (1.0 - ADAM_B2) * (g * g)
        g_ref[...] = g
        nm_ref[...] = nm
        nv_ref[...] = nv
        d_ref[...] = -ADAM_LR * ((nm * c1) / (jnp.sqrt(nv * c2) + ADAM_EPS) + ADAM_WD * w_ref[...])

    tile = pl.BlockSpec((tr, lanes), lambda i: (i, 0))
    out = jax.ShapeDtypeStruct((r, lanes), F32)
    return pl.pallas_call(
        body, out_shape=(out, out, out, out), grid=(r // tr,),
        in_specs=[pl.BlockSpec((N_DEV, tr, lanes), lambda i: (0, i, 0)), tile, tile, tile],
        out_specs=(tile, tile, tile, tile), name=name, compiler_params=_params(1))(pieces, w, m, v)


def _rows_of(numel, row_tile):
    rows = -(-numel // LANES)
    return -(-rows // row_tile) * row_tile


def _pack(flat_list, row_tile, lead=()):
    parts = []
    for a in flat_list:
        numel = a.shape[-1]
        rows = _rows_of(numel, row_tile)
        pad = [(0, 0)] * len(lead) + [(0, rows * LANES - numel)]
        parts.append(jnp.pad(a, pad).reshape(*lead, rows, LANES))
    return jnp.concatenate(parts, axis=len(lead))


def _unpack(buf, shapes, row_tile, lead=()):
    out, r0 = [], 0
    for shape in shapes:
        numel = 1
        for s in shape:
            numel *= s
        rows = _rows_of(numel, row_tile)
        part = lax.slice_in_dim(buf, r0, r0 + rows, axis=len(lead)).reshape(*lead, rows * LANES)
        out.append(lax.slice_in_dim(part, 0, numel, axis=len(lead)).reshape(*lead, *shape))
        r0 += rows
    return out


def _to_shards(full, axis):
    shape = full.shape
    cut = full.reshape(shape[:axis] + (N_DEV, shape[axis] // N_DEV) + shape[axis + 1:])
    return jnp.moveaxis(cut, axis, 0)


def _step(x, target, w, m, v):
    t, d = x.shape[1], x.shape[2]
    x2 = x.reshape(t, d)
    tgt2 = target.reshape(t, d)

    big = _pack([w[k].astype(BF16).reshape(-1) for k in GATHER_BF16], 16)
    small = _pack([w[k].reshape(-1) for k in GATHER_F32], 8)
    big_all, small_all = _all_gather([big, small], "all_gather_weights")
    gw = dict(zip(GATHER_BF16, _unpack(big_all, [w[k].shape for k in GATHER_BF16], 16, (N_DEV,))))
    gw.update(zip(GATHER_F32, _unpack(small_all, [w[k].shape for k in GATHER_F32], 8, (N_DEV,))))

    def full_cols(name, layer):
        a = gw[name][:, layer]
        if a.ndim == 2:
            return a.reshape(1, -1)
        return jnp.moveaxis(a, 0, 1).reshape(a.shape[1], -1)

    ones = jnp.ones((1, d), F32)
    zeros = jnp.zeros((1, d), F32)

    saved = []
    res, res_g, res_b = x2, ones, zeros
    xin = x2.astype(BF16)
    for i in range(DEPTH):
        kind, j = i % 3, i // 3
        sv = {"xin": xin, "kind": kind, "j": j}
        if kind == 0:
            w_in = gw["a_w_in"][:, j]
            w_out = gw["a_w_out"][:, j].reshape(1, d, d)
            dw, dwb = full_cols("a_dw", j), full_cols("a_dw_b", j)
            lg, lb = full_cols("a_ln_g", j), full_cols("a_ln_b", j)
            h1 = _mm_nn(xin, w_in, BF16, f"conv_in_{i}")
            s_act, q = _conv_fwd(h1, dw, dwb, lg, lb, f"conv_mix_{i}")
            sv.update(h1=h1, q=q, w_in=w_in, dw=dw, lg=lg, lb=lb)
        elif kind == 1:
            w_in = gw["b_w_in"][:, j]
            w_out = gw["b_w_out"][:, j].reshape(1, d, d)
            lg, lb = w["b_ln_g"][j].reshape(1, d), w["b_ln_b"][j].reshape(1, d)
            ws, bst = w["b_ws"][j], w["b_bs"][j].T
            h1 = _mm_nn(xin, w_in, BF16, f"sgu_in_{i}")
            s_act = _sgu_fwd(h1, lg, lb, ws, bst, f"sgu_mix_{i}")
            sv.update(h1=h1, w_in=w_in, lg=lg, lb=lb, ws=ws, bst=bst)
        else:
            w_in = gw["c_w_in"][:, j].reshape(1, d, d)
            w_out = gw["c_w_out"][:, j].reshape(1, d, d)
            wgrp = jnp.moveaxis(gw["c_w_grp"][:, j], 0, 1).reshape(4, d // 4, d // 4)
            scale = full_cols("c_scale", j)
            yp = _mm_nn(xin, w_in, F32, f"pool_in_{i}")[0]
            s_act = _pool_fwd(yp, wgrp, scale, f"pool_mix_{i}")
            sv.update(yp=yp, w_in=w_in, wgrp=wgrp, scale=scale)
        g1, b1 = w["ln1_g"][i].reshape(1, d), w["ln1_b"][i].reshape(1, d)
        xhat1, y1, rstd1 = _mm_res_ln(s_act.reshape(1, t, d), w_out, res, res_g, res_b, g1, b1, f"mix_out_ln_{i}")
        w_up = gw["f_w_up"][:, i]
        fdw = gw["f_dw"][:, i]
        n_ff = w_up.shape[-1]
        w_down = gw["f_w_down"][:, i].reshape(4, n_ff, d)
        g2, b2 = w["ln2_g"][i].reshape(1, d), w["ln2_b"][i].reshape(1, d)
        u, a_act, xhat2, y2, rstd2 = _ffn_fwd(y1, w_up, fdw, w_down, xhat1, g1, b1, g2, b2, f"ffn_fwd_{i}")
        sv.update(s_act=s_act, w_out=w_out, xhat1=xhat1, y1=y1, rstd1=rstd1, g1=g1, u=u, a_act=a_act, w_up=w_up,
                  fdw=fdw, w_down=w_down, xhat2=xhat2, rstd2=rstd2, g2=g2, b2=b2)
        saved.append(sv)
        res, res_g, res_b, xin = xhat2, g2, b2, y2

    last = saved[-1]
    dr2, dr2b, dg2, db2, sq = _loss_head(last["xhat2"], last["rstd2"], last["g2"], last["b2"], tgt2, "loss_head")
    loss = lax.psum((0.5 / d) * jnp.sum(sq), ("x", "y", "c"))

    grads = {k: [None] * w[k].shape[0] for k in WEIGHTS}
    grad_x = None
    for i in reversed(range(DEPTH)):
        sv = saved[i]
        kind, j = sv["kind"], sv["j"]
        grads["ln2_g"][i], grads["ln2_b"][i] = dg2, db2
        da = _mm_nt_out(dr2b, sv["w_down"], f"ffn_da_{i}")
        grads["f_w_down"][i] = _to_shards(_mm_tn(sv["a_act"], dr2b.reshape(1, t, d), f"ffn_dwdown_{i}")
                                          .reshape(-1, d), 0)
        du, dfdw = _ffn_gate_bwd(sv["u"], da, sv["fdw"], f"ffn_gate_bwd_{i}")
        grads["f_dw"][i] = dfdw
        grads["f_w_up"][i] = _mm_tn(sv["y1"].reshape(1, t, d), du, f"ffn_dwup_{i}")
        dr1, dr1b, dg1, db1 = _mm_nt_lnb(du, sv["w_up"], dr2, sv["xhat1"], sv["rstd1"], sv["g1"], f"ffn_dx_ln_{i}")
        grads["ln1_g"][i], grads["ln1_b"][i] = dg1, db1
        ds = _mm_nt_out(dr1b, sv["w_out"], f"mix_ds_{i}")[0]
        dw_out = _to_shards(_mm_tn(sv["s_act"].reshape(1, t, d), dr1b.reshape(1, t, d), f"mix_dwout_{i}")[0], 0)
        xin3 = sv["xin"].reshape(1, t, d)
        if kind == 0:
            dh1, ddw, ddwb, dlg, dlb = _conv_bwd(ds, sv["q"], sv["h1"], sv["dw"], sv["lg"], sv["lb"],
                                                  f"conv_mix_bwd_{i}")
            grads["a_w_out"][j] = dw_out
            grads["a_dw"][j] = _to_shards(jnp.sum(ddw, axis=1), 1)
            grads["a_dw_b"][j] = _to_shards(ddwb[0], 0)
            grads["a_ln_g"][j] = _to_shards(dlg[0], 0)
            grads["a_ln_b"][j] = _to_shards(dlb[0], 0)
            grads["a_w_in"][j] = _mm_tn(xin3, dh1, f"conv_dwin_{i}")
            dh_in, w_in = dh1, sv["w_in"]
        elif kind == 1:
            dh1, dlg, dlb, dws, dbias = _sgu_bwd(sv["h1"], ds, sv["lg"], sv["lb"], sv["ws"], sv["bst"],
                                                  f"sgu_mix_bwd_{i}")
            grads["b_w_out"][j] = dw_out
            grads["b_ln_g"][j], grads["b_ln_b"][j] = dlg[0], dlb[0]
            grads["b_ws"][j] = dws
            grads["b_bs"][j] = jnp.sum(dbias.reshape(CHUNK, HEADS, CHUNK), axis=-1).T
            grads["b_w_in"][j] = _mm_tn(xin3, dh1, f"sgu_dwin_{i}")
            dh_in, w_in = dh1, sv["w_in"]
        else:
            dyp, dscale, dwgrp = _pool_bwd(sv["yp"], ds, sv["wgrp"], sv["scale"], f"pool_mix_bwd_{i}")
            grads["c_w_out"][j] = dw_out
            grads["c_scale"][j] = _to_shards(dscale[0], 0)
            grads["c_w_grp"][j] = _to_shards(dwgrp, 1)
            dh_in, w_in = dyp.reshape(1, t, d), sv["w_in"]
            grads["c_w_in"][j] = _to_shards(_mm_tn(xin3, dh_in, f"pool_dwin_{i}")[0], 0)
        if i > 0:
            prev = saved[i - 1]
            dr2, dr2b, dg2, db2 = _mm_nt_lnb(dh_in, w_in, dr1, prev["xhat2"], prev["rstd2"], prev["g2"],
                                              f"mix_dx_ln_{i}")
        else:
            grad_x = _mm_nt_res(dh_in, w_in, dr1, "mix_dx_0").reshape(x.shape)

    flat = {}
    for k in WEIGHTS:
        if k in REPLICATED:
            full = jnp.stack([gk.reshape(w[k].shape[1:]) for gk in grads[k]], axis=0)
            flat[k] = jnp.broadcast_to(full.reshape(1, -1), (N_DEV, full.size))
        else:
            flat[k] = jnp.stack(grads[k], axis=1).reshape(N_DEV, -1)
    groups = ([k for k in WEIGHTS if k in GATHER_BF16], [k for k in WEIGHTS if k not in GATHER_BF16])
    row_tiles, dtypes = (16, 8), (BF16, F32)
    pieces, pads = [], []
    for names, row_tile, dtype in zip(groups, row_tiles, dtypes):
        buf = _pack([flat[k].astype(dtype) for k in names], row_tile, (N_DEV,))
        pads.append(-(-buf.shape[1] // 128) * 128 - buf.shape[1])
        pieces.append(jnp.pad(buf, ((0, 0), (0, pads[-1]), (0, 0))))
    received = _exchange_pieces(pieces, "exchange_gradients")

    result = {}
    for names, row_tile, pad_rows, got, tag in zip(groups, row_tiles, pads, received, ("matrices", "vectors")):
        def packed(tree):
            return jnp.pad(_pack([tree[k].reshape(-1) for k in names], row_tile), ((0, pad_rows), (0, 0)))

        bufs = _adamw(got, packed(w), packed(m), packed(v), f"adamw_{tag}")
        shapes = [w[k].shape for k in names]
        for kind, buf in zip(("grad", "delta", "new_m", "new_v"), bufs):
            result.update({(kind, k): a for k, a in zip(names, _unpack(buf, shapes, row_tile))})
    outs = [result[(kind, k)] for kind in ("grad", "delta", "new_m", "new_v") for k in WEIGHTS]
    return (loss, grad_x, *outs)


def kernel(x, a_w_in, a_dw, a_dw_b, a_ln_g, a_ln_b, a_w_out, b_w_in, b_ln_g, b_ln_b, b_ws, b_bs, b_w_out, c_w_in, c_w_grp, c_scale, c_w_out, f_w_up, f_dw, f_w_down, ln1_g, ln1_b, ln2_g, ln2_b, loss_target, m_a_w_in, m_a_dw, m_a_dw_b, m_a_ln_g, m_a_ln_b, m_a_w_out, m_b_w_in, m_b_ln_g, m_b_ln_b, m_b_ws, m_b_bs, m_b_w_out, m_c_w_in, m_c_w_grp, m_c_scale, m_c_w_out, m_f_w_up, m_f_dw, m_f_w_down, m_ln1_g, m_ln1_b, m_ln2_g, m_ln2_b, v_a_w_in, v_a_dw, v_a_dw_b, v_a_ln_g, v_a_ln_b, v_a_w_out, v_b_w_in, v_b_ln_g, v_b_ln_b, v_b_ws, v_b_bs, v_b_w_out, v_c_w_in, v_c_w_grp, v_c_scale, v_c_w_out, v_f_w_up, v_f_dw, v_f_w_down, v_ln1_g, v_ln1_b, v_ln2_g, v_ln2_b):
    w = dict(zip(WEIGHTS, (a_w_in, a_dw, a_dw_b, a_ln_g, a_ln_b, a_w_out, b_w_in, b_ln_g, b_ln_b, b_ws, b_bs, b_w_out,
                           c_w_in, c_w_grp, c_scale, c_w_out, f_w_up, f_dw, f_w_down, ln1_g, ln1_b, ln2_g, ln2_b)))
    m = dict(zip(WEIGHTS, (m_a_w_in, m_a_dw, m_a_dw_b, m_a_ln_g, m_a_ln_b, m_a_w_out, m_b_w_in, m_b_ln_g, m_b_ln_b,
                           m_b_ws, m_b_bs, m_b_w_out, m_c_w_in, m_c_w_grp, m_c_scale, m_c_w_out, m_f_w_up, m_f_dw,
                           m_f_w_down, m_ln1_g, m_ln1_b, m_ln2_g, m_ln2_b)))
    v = dict(zip(WEIGHTS, (v_a_w_in, v_a_dw, v_a_dw_b, v_a_ln_g, v_a_ln_b, v_a_w_out, v_b_w_in, v_b_ln_g, v_b_ln_b,
                           v_b_ws, v_b_bs, v_b_w_out, v_c_w_in, v_c_w_grp, v_c_scale, v_c_w_out, v_f_w_up, v_f_dw,
                           v_f_w_down, v_ln1_g, v_ln1_b, v_ln2_g, v_ln2_b)))
    return _step(x, loss_target, w, m, v)
```

```python
import functools

import jax
import jax.numpy as jnp
from jax import lax
from jax.experimental import pallas as pl
from jax.experimental.pallas import tpu as pltpu

N_DEV = 8
DEPTH = 4
ALPHA = float((2 * DEPTH) ** 0.25)
LN_EPS = 1e-5
CONV_W = 31
CONV_HALO = 32
FFN_HALO = 16
POOL_WINDOWS = (2, 4, 8, 16)
CHUNK = 128
HEADS = 8
LANES = 1024
ADAM_LR, ADAM_B1, ADAM_B2, ADAM_EPS, ADAM_WD, ADAM_STEP = 0.001, 0.9, 0.999, 1e-08, 0.01, 10
VMEM_LIMIT = 56 * 1024 * 1024
F32, BF16 = jnp.float32, jnp.bfloat16
MESH = pl.DeviceIdType.MESH

WEIGHTS = ['a_w_in', 'a_dw', 'a_dw_b', 'a_ln_g', 'a_ln_b', 'a_w_out', 'b_w_in', 'b_ln_g', 'b_ln_b', 'b_ws', 'b_bs',
           'b_w_out', 'c_w_in', 'c_w_grp', 'c_scale', 'c_w_out', 'f_w_up', 'f_dw', 'f_w_down', 'ln1_g', 'ln1_b',
           'ln2_g', 'ln2_b']
REPLICATED = ('b_ln_g', 'b_ln_b', 'b_ws', 'b_bs', 'ln1_g', 'ln1_b', 'ln2_g', 'ln2_b')
GATHER_BF16 = ('a_w_in', 'a_w_out', 'b_w_in', 'b_w_out', 'c_w_in', 'c_w_grp', 'c_w_out', 'f_w_up', 'f_w_down')
GATHER_F32 = ('a_dw', 'a_dw_b', 'a_ln_g', 'a_ln_b', 'c_scale', 'f_dw')


def _params(n_axes):
    return pltpu.CompilerParams(dimension_semantics=("arbitrary",) * n_axes, vmem_limit_bytes=VMEM_LIMIT)


def _resident(shape):
    zeros = (0,) * len(shape)
    return pl.BlockSpec(shape, lambda i: zeros, pipeline_mode=pl.Buffered(1))


def _sigmoid(x):
    return 1.0 / (1.0 + jnp.exp(-x))


def _gelu(x):
    return 0.5 * x * (1.0 + lax.erf(x * 0.7071067811865476))


def _gelu_grad(x):
    return 0.5 * (1.0 + lax.erf(x * 0.7071067811865476)) + x * jnp.exp(-0.5 * x * x) * 0.3989422804014327


def _ln_stats(r):
    mu = jnp.mean(r, axis=-1, keepdims=True)
    xc = r - mu
    var = jnp.mean(xc * xc, axis=-1, keepdims=True)
    rstd = lax.rsqrt(var + LN_EPS)
    return xc * rstd, rstd


def _ln_bwd(dy, xhat, rstd, g):
    dxhat = dy * g
    m1 = jnp.mean(dxhat, axis=-1, keepdims=True)
    m2 = jnp.mean(dxhat * xhat, axis=-1, keepdims=True)
    dr = rstd * (dxhat - m1 - xhat * m2)
    return dr, jnp.sum(dy * xhat, axis=0, keepdims=True), jnp.sum(dy, axis=0, keepdims=True)


def _accumulate(ref, value, first):
    @pl.when(first)
    def _():
        ref[...] = value

    @pl.when(jnp.logical_not(first))
    def _():
        ref[...] += value


def _dot(a, b):
    return jnp.dot(a, b, preferred_element_type=F32)


def _dot_nt(a, b):
    return lax.dot_general(a, b, (((1,), (1,)), ((), ())), preferred_element_type=F32)


def _dot_tn(a, b):
    return lax.dot_general(a, b, (((0,), (0,)), ((), ())), preferred_element_type=F32)


def _cat_lanes(h, lo, hi):
    return jnp.concatenate([h[s] for s in range(lo, hi)], axis=-1)


def _mm_nn(x, w, out_dtype, name, tm=512):
    t, k = x.shape
    s_n, _, n = w.shape
    tm = min(tm, t)

    def body(x_ref, w_ref, o_ref):
        x_tile = x_ref[...]
        for s in range(s_n):
            o_ref[s] = _dot(x_tile, w_ref[s]).astype(o_ref.dtype)

    return pl.pallas_call(
        body, out_shape=jax.ShapeDtypeStruct((s_n, t, n), out_dtype), grid=(t // tm,),
        in_specs=[pl.BlockSpec((tm, k), lambda i: (i, 0)), _resident((s_n, k, n))],
        out_specs=pl.BlockSpec((s_n, tm, n), lambda i: (0, i, 0)),
        name=name, compiler_params=_params(1))(x, w)


def _mm_res_ln(a, w, res, gp, bp, g, b, name, tm=512):
    s_n, t, ka = a.shape
    d = w.shape[-1]
    tm = min(tm, t)

    def body(a_ref, w_ref, res_ref, gp_ref, bp_ref, g_ref, b_ref, xhat_ref, y_ref, rstd_ref):
        acc = _dot(a_ref[0], w_ref[0])
        for s in range(1, s_n):
            acc += _dot(a_ref[s], w_ref[s])
        r = ALPHA * (res_ref[...] * gp_ref[...] + bp_ref[...]) + acc
        xhat, rstd = _ln_stats(r)
        xhat_ref[...] = xhat
        y_ref[...] = (xhat * g_ref[...] + b_ref[...]).astype(BF16)
        rstd_ref[...] = rstd

    row = pl.BlockSpec((1, d), lambda i: (0, 0))
    tile = pl.BlockSpec((tm, d), lambda i: (i, 0))
    return pl.pallas_call(
        body,
        out_shape=(jax.ShapeDtypeStruct((t, d), F32), jax.ShapeDtypeStruct((t, d), BF16),
                   jax.ShapeDtypeStruct((t, 1), F32)),
        grid=(t // tm,),
        in_specs=[pl.BlockSpec((s_n, tm, ka), lambda i: (0, i, 0)), _resident((s_n, ka, d)),
                  tile, row, row, row, row],
        out_specs=(tile, tile, pl.BlockSpec((tm, 1), lambda i: (i, 0))),
        name=name, compiler_params=_params(1))(a, w, res, gp, bp, g, b)


def _mm_nt_out(x, w, name, tm=512):
    t, n = x.shape
    s_n, k, _ = w.shape
    tm = min(tm, t)

    def body(x_ref, w_ref, o_ref):
        x_tile = x_ref[...]
        for s in range(s_n):
            o_ref[s] = _dot_nt(x_tile, w_ref[s]).astype(o_ref.dtype)

    return pl.pallas_call(
        body, out_shape=jax.ShapeDtypeStruct((s_n, t, k), BF16), grid=(t // tm,),
        in_specs=[pl.BlockSpec((tm, n), lambda i: (i, 0)), _resident((s_n, k, n))],
        out_specs=pl.BlockSpec((s_n, tm, k), lambda i: (0, i, 0)),
        name=name, compiler_params=_params(1))(x, w)


def _mm_nt_lnb(dh, w, drn, xhat, rstd, g, name, tm=512):
    s_n, t, n = dh.shape
    k = w.shape[1]
    tm = min(tm, t)

    def body(dh_ref, w_ref, drn_ref, xhat_ref, rstd_ref, g_ref, dr_ref, drb_ref, dg_ref, db_ref):
        first = pl.program_id(0) == 0
        acc = _dot_nt(dh_ref[0], w_ref[0])
        for s in range(1, s_n):
            acc += _dot_nt(dh_ref[s], w_ref[s])
        dy = acc + ALPHA * drn_ref[...]
        dr, dg, db = _ln_bwd(dy, xhat_ref[...], rstd_ref[...], g_ref[...])
        dr_ref[...] = dr
        drb_ref[...] = dr.astype(BF16)
        _accumulate(dg_ref, dg, first)
        _accumulate(db_ref, db, first)

    tile = pl.BlockSpec((tm, k), lambda i: (i, 0))
    row = pl.BlockSpec((1, k), lambda i: (0, 0))
    return pl.pallas_call(
        body,
        out_shape=(jax.ShapeDtypeStruct((t, k), F32), jax.ShapeDtypeStruct((t, k), BF16),
                   jax.ShapeDtypeStruct((1, k), F32), jax.ShapeDtypeStruct((1, k), F32)),
        grid=(t // tm,),
        in_specs=[pl.BlockSpec((s_n, tm, n), lambda i: (0, i, 0)), _resident((s_n, k, n)),
                  tile, tile, pl.BlockSpec((tm, 1), lambda i: (i, 0)), row],
        out_specs=(tile, tile, row, row),
        name=name, compiler_params=_params(1))(dh, w, drn, xhat, rstd, g)


def _mm_nt_res(dh, w, drn, name, tm=512):
    s_n, t, n = dh.shape
    k = w.shape[1]
    tm = min(tm, t)

    def body(dh_ref, w_ref, drn_ref, o_ref):
        acc = _dot_nt(dh_ref[0], w_ref[0])
        for s in range(1, s_n):
            acc += _dot_nt(dh_ref[s], w_ref[s])
        o_ref[...] = acc + ALPHA * drn_ref[...]

    tile = pl.BlockSpec((tm, k), lambda i: (i, 0))
    return pl.pallas_call(
        body, out_shape=jax.ShapeDtypeStruct((t, k), F32), grid=(t // tm,),
        in_specs=[pl.BlockSpec((s_n, tm, n), lambda i: (0, i, 0)), _resident((s_n, k, n)), tile],
        out_specs=tile, name=name, compiler_params=_params(1))(dh, w, drn)


def _mm_tn(lhs, rhs, name, tm=2048):
    sl, t, kl = lhs.shape
    sr, _, n = rhs.shape
    s_n = max(sl, sr)
    tm = min(tm, t)

    def body(l_ref, r_ref, o_ref):
        _accumulate(o_ref, _dot_tn(l_ref[...], r_ref[...]), pl.program_id(1) == 0)

    return pl.pallas_call(
        body, out_shape=jax.ShapeDtypeStruct((s_n, kl, n), F32), grid=(s_n, t // tm),
        in_specs=[pl.BlockSpec((None, tm, kl), (lambda s, i: (s, i, 0)) if sl > 1 else (lambda s, i: (0, i, 0))),
                  pl.BlockSpec((None, tm, n), (lambda s, i: (s, i, 0)) if sr > 1 else (lambda s, i: (0, i, 0)))],
        out_specs=pl.BlockSpec((None, kl, n), lambda s, i: (s, 0, 0)),
        name=name, compiler_params=_params(2))(lhs, rhs)


def _glu(h):
    return _cat_lanes(h, 0, 4).astype(F32) * _sigmoid(_cat_lanes(h, 4, 8).astype(F32))


def _shifted_windows(src_ref, r0, c0, row_block, offsets):
    span = row_block + CONV_HALO
    big = src_ref[pl.ds(r0, span), pl.ds(c0, 128)]
    for sub in range(8):
        taps = [k for k, o in enumerate(offsets) if o % 8 == sub]
        if not taps:
            continue
        rolled = big if sub == 0 else pltpu.roll(big, span - sub, 0)
        for k in taps:
            lo = offsets[k] - sub
            yield k, rolled[lo:lo + row_block]


def _tap_loop(src_ref, dst_ref, weight_ref, rows, offsets, weight_rows, row_block=64):
    d = dst_ref.shape[-1]

    def block(cb, carry):
        c0 = pl.multiple_of(cb * 128, 128)
        for r0 in range(0, rows, row_block):
            acc = jnp.zeros((row_block, 128), F32)
            for k, window in _shifted_windows(src_ref, r0, c0, row_block, offsets):
                acc += weight_ref[pl.ds(weight_rows[k], 1), pl.ds(c0, 128)] * window
            dst_ref[pl.ds(r0, row_block), pl.ds(c0, 128)] = acc
        return carry

    lax.fori_loop(0, d // 128, block, 0)


def _conv_fwd(h1, dw, dwb, g, b, name, tm=256):
    _, t, _ = h1.shape
    d = dw.shape[-1]
    tm = min(tm, t)
    hb = tm // CONV_HALO

    def body(h_ref, halo_ref, dw_ref, dwb_ref, g_ref, b_ref, s_ref, q_ref, ext_ref):
        i = pl.program_id(0)
        ext_ref[pl.ds(0, CONV_HALO), :] = _glu(halo_ref[...]) * (i > 0).astype(F32)
        ext_ref[pl.ds(CONV_HALO, tm), :] = _glu(h_ref[...])
        _tap_loop(ext_ref, q_ref, dw_ref, tm, [2 + k for k in range(CONV_W)], list(range(CONV_W)))
        q = q_ref[...] + dwb_ref[...]
        q_ref[...] = q
        qhat, _ = _ln_stats(q)
        z = qhat * g_ref[...] + b_ref[...]
        s_ref[...] = (z * _sigmoid(z)).astype(BF16)

    row = pl.BlockSpec((1, d), lambda i: (0, 0))
    tile = pl.BlockSpec((tm, d), lambda i: (i, 0))
    return pl.pallas_call(
        body, out_shape=(jax.ShapeDtypeStruct((t, d), BF16), jax.ShapeDtypeStruct((t, d), F32)), grid=(t // tm,),
        in_specs=[pl.BlockSpec((8, tm, 256), lambda i: (0, i, 0)),
                  pl.BlockSpec((8, CONV_HALO, 256), lambda i: (0, jnp.maximum(i * hb - 1, 0), 0)),
                  pl.BlockSpec((CONV_W, d), lambda i: (0, 0)), row, row, row],
        out_specs=(tile, tile), scratch_shapes=[pltpu.VMEM((tm + CONV_HALO, d), F32)],
        name=name, compiler_params=_params(1))(h1, h1, dw, dwb, g, b)


def _conv_bwd(ds, q, h1, dw, g, b, name, tm=256):
    _, t, _ = h1.shape
    d = dw.shape[-1]
    tm = min(tm, t)
    hb = tm // CONV_HALO
    n_t = t // tm
    last_halo = t // CONV_HALO - 1

    def body(ds_ref, dsn_ref, q_ref, qn_ref, h_ref, hp_ref, dw_ref, g_ref, b_ref,
             dh_ref, ddw_ref, ddwb_ref, dg_ref, db_ref, dq_ref, p_ref, dp_ref):
        i = pl.program_id(0)
        first = i == 0
        valid = (i < n_t - 1).astype(F32)

        def dq_rows(ds_rows, q_rows, scale):
            qhat, rstd = _ln_stats(q_rows)
            z = qhat * g_ref[...] + b_ref[...]
            sg = _sigmoid(z)
            dz = ds_rows.astype(F32) * (sg * (1.0 + z * (1.0 - sg))) * scale
            dq, dg, db = _ln_bwd(dz, qhat, rstd, g_ref[...])
            return dq, dg, db

        dq, dg, db = dq_rows(ds_ref[...], q_ref[...], 1.0)
        dq_ref[pl.ds(0, tm), :] = dq
        dq_ref[pl.ds(tm, CONV_HALO), :] = dq_rows(dsn_ref[...], qn_ref[...], valid)[0]
        _accumulate(dg_ref, dg, first)
        _accumulate(db_ref, db, first)
        _accumulate(ddwb_ref, jnp.sum(dq, axis=0, keepdims=True), first)

        p_ref[pl.ds(0, CONV_HALO), :] = _glu(hp_ref[...]) * (i > 0).astype(F32)
        p_ref[pl.ds(CONV_HALO, tm), :] = _glu(h_ref[...])

        _tap_loop(dq_ref, dp_ref, dw_ref, tm, list(range(CONV_W)), [CONV_W - 1 - o for o in range(CONV_W)])

        @pl.when(first)
        def _():
            ddw_ref[...] = jnp.zeros_like(ddw_ref)

        row_block = 64

        def block(cb, carry):
            c0 = pl.multiple_of(cb * 128, 128)
            for r0 in range(0, tm, row_block):
                dqb = dq_ref[pl.ds(r0, row_block), pl.ds(c0, 128)]
                for k, window in _shifted_windows(p_ref, r0, c0, row_block, [2 + k for k in range(CONV_W)]):
                    prod = dqb * window
                    ddw_ref[k, :, pl.ds(c0, 128)] += jnp.sum(prod.reshape(row_block // 8, 8, 128), axis=0)
            return carry

        lax.fori_loop(0, d // 128, block, 0)

        h = h_ref[...]
        a = _cat_lanes(h, 0, 4).astype(F32)
        sg = _sigmoid(_cat_lanes(h, 4, 8).astype(F32))
        dp = dp_ref[...]
        da = (dp * sg).astype(BF16)
        dgate = (dp * a * sg * (1.0 - sg)).astype(BF16)
        for s in range(4):
            dh_ref[s] = da[:, s * 256:(s + 1) * 256]
            dh_ref[4 + s] = dgate[:, s * 256:(s + 1) * 256]

    row = pl.BlockSpec((1, d), lambda i: (0, 0))
    tile = pl.BlockSpec((tm, d), lambda i: (i, 0))
    nxt = pl.BlockSpec((CONV_HALO, d), lambda i: (jnp.minimum((i + 1) * hb, last_halo), 0))
    return pl.pallas_call(
        body,
        out_shape=(jax.ShapeDtypeStruct((8, t, 256), BF16), jax.ShapeDtypeStruct((CONV_W, 8, d), F32),
                   jax.ShapeDtypeStruct((1, d), F32), jax.ShapeDtypeStruct((1, d), F32),
                   jax.ShapeDtypeStruct((1, d), F32)),
        grid=(n_t,),
        in_specs=[tile, nxt, tile, nxt,
                  pl.BlockSpec((8, tm, 256), lambda i: (0, i, 0)),
                  pl.BlockSpec((8, CONV_HALO, 256), lambda i: (0, jnp.maximum(i * hb - 1, 0), 0)),
                  pl.BlockSpec((CONV_W, d), lambda i: (0, 0)), row, row],
        out_specs=(pl.BlockSpec((8, tm, 256), lambda i: (0, i, 0)),
                   pl.BlockSpec((CONV_W, 8, d), lambda i: (0, 0, 0)), row, row, row),
        scratch_shapes=[pltpu.VMEM((tm + CONV_HALO, d), F32), pltpu.VMEM((tm + CONV_HALO, d), F32),
                        pltpu.VMEM((tm, d), F32)],
        name=name, compiler_params=_params(1))(ds, ds, q, q, h1, h1, dw, g, b)


def _conv3(ext, dw):
    e1 = pltpu.roll(ext, 1, 0)
    e2 = pltpu.roll(ext, 2, 0)
    return dw[2:3] * ext + dw[1:2] * e1 + dw[0:1] * e2, e1, e2


def _ffn_fwd(x, w_up, fdw, w_down, res, gp, bp, g, b, name, tm=256):
    t, d = x.shape
    _, _, n = w_up.shape
    tm = min(tm, t)
    carry_rows = 8

    def body(x_ref, wu_ref, dw_ref, wd_ref, res_ref, gp_ref, bp_ref, g_ref, b_ref,
             u_ref, h_ref, a_ref, xhat_ref, y_ref, rstd_ref, carry_ref):
        @pl.when(pl.program_id(0) == 0)
        def _():
            carry_ref[...] = jnp.zeros_like(carry_ref)

        x_tile = x_ref[...]
        acc = None
        for j in range(4):
            h = []
            for s in (j, 4 + j):
                ub = _dot(x_tile, wu_ref[s]).astype(BF16)
                u_ref[s] = ub
                uf = ub.astype(F32)
                ext = jnp.concatenate([carry_ref[s], uf], axis=0)
                carry_ref[s] = uf[tm - carry_rows:]
                hb = _conv3(ext, dw_ref[s])[0][carry_rows:].astype(BF16)
                h_ref[s] = hb
                h.append(hb.astype(F32))
            a = (h[0] * _sigmoid(h[0]) * h[1]).astype(BF16)
            a_ref[j] = a
            part = _dot(a, wd_ref[j])
            acc = part if acc is None else acc + part
        r = ALPHA * (res_ref[...] * gp_ref[...] + bp_ref[...]) + acc
        xhat, rstd = _ln_stats(r)
        xhat_ref[...] = xhat
        y_ref[...] = (xhat * g_ref[...] + b_ref[...]).astype(BF16)
        rstd_ref[...] = rstd

    row = pl.BlockSpec((1, d), lambda i: (0, 0))
    tile = pl.BlockSpec((tm, d), lambda i: (i, 0))
    return pl.pallas_call(
        body,
        out_shape=(jax.ShapeDtypeStruct((8, t, n), BF16), jax.ShapeDtypeStruct((8, t, n), BF16),
                   jax.ShapeDtypeStruct((4, t, n), BF16), jax.ShapeDtypeStruct((t, d), F32),
                   jax.ShapeDtypeStruct((t, d), BF16), jax.ShapeDtypeStruct((t, 1), F32)),
        grid=(t // tm,),
        in_specs=[tile, _resident((8, d, n)), pl.BlockSpec((8, 3, n), lambda i: (0, 0, 0)), _resident((4, n, d)),
                  tile, row, row, row, row],
        out_specs=(pl.BlockSpec((8, tm, n), lambda i: (0, i, 0)), pl.BlockSpec((8, tm, n), lambda i: (0, i, 0)),
                   pl.BlockSpec((4, tm, n), lambda i: (0, i, 0)),
                   tile, tile, pl.BlockSpec((tm, 1), lambda i: (i, 0))),
        scratch_shapes=[pltpu.VMEM((8, carry_rows, n), F32)],
        name=name, compiler_params=_params(1))(x, w_up, fdw, w_down, res, gp, bp, g, b)


def _ffn_gate_bwd(h, u, da, fdw, name, tm=256):
    _, t, n = u.shape
    tm = min(tm, t)
    hb = tm // FFN_HALO
    n_t = t // tm
    last_halo = t // FFN_HALO - 1
    h4, u4 = h.reshape(2, 4, t, n), u.reshape(2, 4, t, n)
    fdw4 = fdw.reshape(2, 4, 3, n)
    rows = tm + FFN_HALO

    def body(h_ref, hn_ref, u_ref, da_ref, dan_ref, dw_ref, du_ref, ddw_ref):
        i = pl.program_id(1)
        keep_next = (i < n_t - 1).astype(F32)
        hg = jnp.concatenate([h_ref[0], hn_ref[0]], axis=0).astype(F32)
        hv = jnp.concatenate([h_ref[1], hn_ref[1]], axis=0).astype(F32)
        da_ext = jnp.concatenate([da_ref[...].astype(F32), dan_ref[...].astype(F32) * keep_next], axis=0)
        sg = _sigmoid(hg)
        silu = hg * sg
        dh = (da_ext * hv * (sg + silu * (1.0 - sg)), da_ext * silu)
        for p in range(2):
            dwp = dw_ref[p]
            d1 = pltpu.roll(dh[p], rows - 1, 0)
            d2 = pltpu.roll(dh[p], rows - 2, 0)
            du_ref[p] = (dwp[2:3] * dh[p] + dwp[1:2] * d1 + dwp[0:1] * d2)[:tm].astype(BF16)
            up = u_ref[p].astype(F32)
            part = jnp.concatenate([jnp.sum(d[:tm] * up, axis=0, keepdims=True) for d in (d2, d1, dh[p])], axis=0)
            _accumulate(ddw_ref.at[p], part, i == 0)

    tile = pl.BlockSpec((2, None, tm, n), lambda j, i: (0, j, i, 0))
    nxt = pl.BlockSpec((2, None, FFN_HALO, n), lambda j, i: (0, j, jnp.minimum((i + 1) * hb, last_halo), 0))
    du, ddw = pl.pallas_call(
        body, out_shape=(jax.ShapeDtypeStruct((2, 4, t, n), BF16), jax.ShapeDtypeStruct((2, 4, 3, n), F32)),
        grid=(4, n_t),
        in_specs=[tile, nxt, tile,
                  pl.BlockSpec((None, tm, n), lambda j, i: (j, i, 0)),
                  pl.BlockSpec((None, FFN_HALO, n), lambda j, i: (j, jnp.minimum((i + 1) * hb, last_halo), 0)),
                  pl.BlockSpec((2, None, 3, n), lambda j, i: (0, j, 0, 0))],
        out_specs=(tile, pl.BlockSpec((2, None, 3, n), lambda j, i: (0, j, 0, 0))),
        name=name, compiler_params=_params(2))(h4, h4, u4, da, da, fdw4)
    return du.reshape(8, t, n), ddw.reshape(8, 3, n)


def _tril_mask():
    r = lax.broadcasted_iota(jnp.int32, (CHUNK, CHUNK), 0)
    c = lax.broadcasted_iota(jnp.int32, (CHUNK, CHUNK), 1)
    return (r >= c).astype(F32)


def _sgu_fwd(h1, g, b, ws, bst, name, tm=256):
    _, t, _ = h1.shape
    d = g.shape[-1]
    tm = min(tm, t)

    def body(h_ref, g_ref, b_ref, ws_ref, bst_ref, m_ref):
        h = h_ref[...]
        u = _gelu(_cat_lanes(h, 0, 4).astype(F32))
        v = _gelu(_cat_lanes(h, 4, 8).astype(F32))
        vn = (_ln_stats(v)[0] * g_ref[...] + b_ref[...]).astype(BF16)
        mask = _tril_mask()
        for hh in range(HEADS):
            cols = slice(hh * CHUNK, (hh + 1) * CHUNK)
            wm = (ws_ref[hh] * mask).astype(BF16)
            bias = bst_ref[:, hh:hh + 1]
            for c in range(tm // CHUNK):
                rows = slice(c * CHUNK, (c + 1) * CHUNK)
                sblk = _dot(wm, vn[rows, cols]) + bias
                m_ref[rows, cols] = (u[rows, cols] * sblk).astype(BF16)

    row = pl.BlockSpec((1, d), lambda i: (0, 0))
    return pl.pallas_call(
        body, out_shape=jax.ShapeDtypeStruct((t, d), BF16), grid=(t // tm,),
        in_specs=[pl.BlockSpec((8, tm, 256), lambda i: (0, i, 0)), row, row,
                  pl.BlockSpec((HEADS, CHUNK, CHUNK), lambda i: (0, 0, 0)),
                  pl.BlockSpec((CHUNK, HEADS), lambda i: (0, 0))],
        out_specs=pl.BlockSpec((tm, d), lambda i: (i, 0)),
        name=name, compiler_params=_params(1))(h1, g, b, ws, bst)


def _sgu_bwd(h1, dm, g, b, ws, bst, name, tm=256):
    _, t, _ = h1.shape
    d = g.shape[-1]
    tm = min(tm, t)

    def body(h_ref, dm_ref, g_ref, b_ref, ws_ref, bst_ref, dh_ref, dg_ref, db_ref, dws_ref, dbias_ref,
             du_ref, dvn_ref):
        first = pl.program_id(0) == 0
        h = h_ref[...]
        zu = _cat_lanes(h, 0, 4).astype(F32)
        zv = _cat_lanes(h, 4, 8).astype(F32)
        u = _gelu(zu)
        vhat, rstd = _ln_stats(_gelu(zv))
        vn = (vhat * g_ref[...] + b_ref[...]).astype(BF16)
        dm = dm_ref[...].astype(F32)
        mask = _tril_mask()

        @pl.when(first)
        def _():
            dws_ref[...] = jnp.zeros_like(dws_ref)
            dbias_ref[...] = jnp.zeros_like(dbias_ref)

        for hh in range(HEADS):
            cols = slice(hh * CHUNK, (hh + 1) * CHUNK)
            wm = (ws_ref[hh] * mask).astype(BF16)
            bias = bst_ref[:, hh:hh + 1]
            for c in range(tm // CHUNK):
                rows = slice(c * CHUNK, (c + 1) * CHUNK)
                vb = vn[rows, cols]
                sblk = _dot(wm, vb) + bias
                dmb = dm[rows, cols]
                du_ref[rows, cols] = dmb * sblk
                dsb = dmb * u[rows, cols]
                dbias_ref[:, cols] += dsb
                dsb16 = dsb.astype(BF16)
                dws_ref[hh] += _dot_nt(dsb16, vb) * mask
                dvn_ref[rows, cols] = _dot_tn(wm, dsb16)

        dvn = dvn_ref[...]
        dv, dg, db = _ln_bwd(dvn, vhat, rstd, g_ref[...])
        _accumulate(dg_ref, dg, first)
        _accumulate(db_ref, db, first)
        dzu = (du_ref[...] * _gelu_grad(zu)).astype(BF16)
        dzv = (dv * _gelu_grad(zv)).astype(BF16)
        for s in range(4):
            dh_ref[s] = dzu[:, s * 256:(s + 1) * 256]
            dh_ref[4 + s] = dzv[:, s * 256:(s + 1) * 256]

    row = pl.BlockSpec((1, d), lambda i: (0, 0))
    tile = pl.BlockSpec((tm, d), lambda i: (i, 0))
    h_tile = pl.BlockSpec((8, tm, 256), lambda i: (0, i, 0))
    return pl.pallas_call(
        body,
        out_shape=(jax.ShapeDtypeStruct((8, t, 256), BF16), jax.ShapeDtypeStruct((1, d), F32),
                   jax.ShapeDtypeStruct((1, d), F32), jax.ShapeDtypeStruct((HEADS, CHUNK, CHUNK), F32),
                   jax.ShapeDtypeStruct((CHUNK, d), F32)),
        grid=(t // tm,),
        in_specs=[h_tile, tile, row, row, pl.BlockSpec((HEADS, CHUNK, CHUNK), lambda i: (0, 0, 0)),
                  pl.BlockSpec((CHUNK, HEADS), lambda i: (0, 0))],
        out_specs=(h_tile, row, row, pl.BlockSpec((HEADS, CHUNK, CHUNK), lambda i: (0, 0, 0)),
                   pl.BlockSpec((CHUNK, d), lambda i: (0, 0))),
        scratch_shapes=[pltpu.VMEM((tm, d), F32), pltpu.VMEM((tm, d), F32)],
        name=name, compiler_params=_params(1))(h1, dm, g, b, ws, bst)


def _pool_minus_self(ext, first_token, grp):
    s = ext
    for step in range(grp + 1):
        s = s + pltpu.roll(s, 1 << step, 0)
    rows = ext.shape[0] - FFN_HALO
    tok = first_token + lax.broadcasted_iota(jnp.int32, (rows, 1), 0)
    count = jnp.minimum(tok + 1, POOL_WINDOWS[grp]).astype(F32)
    return s[FFN_HALO:] / count - ext[FFN_HALO:]


def _pool_fwd(y, wgrp, scale, name, tm=256):
    t, d = y.shape
    tm = min(tm, t)
    hb = tm // FFN_HALO
    gd = d // len(POOL_WINDOWS)

    def body(y_ref, yp_ref, w_ref, sc_ref, z_ref):
        i = pl.program_id(0)
        ext = jnp.concatenate([yp_ref[...] * (i > 0).astype(F32), y_ref[...]], axis=0)
        for grp in range(len(POOL_WINDOWS)):
            cols = slice(grp * gd, (grp + 1) * gd)
            p = _pool_minus_self(ext[:, cols], i * tm, grp)
            z_ref[:, cols] = (_dot(p.astype(BF16), w_ref[grp]) * sc_ref[:, cols]).astype(BF16)

    return pl.pallas_call(
        body, out_shape=jax.ShapeDtypeStruct((t, d), BF16), grid=(t // tm,),
        in_specs=[pl.BlockSpec((tm, d), lambda i: (i, 0)),
                  pl.BlockSpec((FFN_HALO, d), lambda i: (jnp.maximum(i * hb - 1, 0), 0)),
                  pl.BlockSpec((len(POOL_WINDOWS), gd, gd), lambda i: (0, 0, 0)),
                  pl.BlockSpec((1, d), lambda i: (0, 0))],
        out_specs=pl.BlockSpec((tm, d), lambda i: (i, 0)),
        name=name, compiler_params=_params(1))(y, y, wgrp, scale)


def _pool_bwd(y, dz, wgrp, scale, name, tm=256):
    t, d = y.shape
    tm = min(tm, t)
    hb = tm // FFN_HALO
    n_t = t // tm
    last_halo = t // FFN_HALO - 1
    gd = d // len(POOL_WINDOWS)
    rows = tm + FFN_HALO

    def body(y_ref, yp_ref, dz_ref, dzn_ref, w_ref, sc_ref, dy_ref, dsc_ref, dw_ref):
        i = pl.program_id(0)
        first = i == 0
        ext = jnp.concatenate([yp_ref[...] * (i > 0).astype(F32), y_ref[...]], axis=0)
        dz_ext = jnp.concatenate([dz_ref[...].astype(F32), dzn_ref[...].astype(F32) * (i < n_t - 1).astype(F32)],
                                 axis=0)
        tok = i * tm + lax.broadcasted_iota(jnp.int32, (rows, 1), 0)
        dsc = []
        for grp in range(len(POOL_WINDOWS)):
            cols = slice(grp * gd, (grp + 1) * gd)
            p16 = _pool_minus_self(ext[:, cols], i * tm, grp).astype(BF16)
            zg = _dot(p16, w_ref[grp])
            dsc.append(jnp.sum(dz_ext[:tm, cols] * zg, axis=0, keepdims=True))
            dzg = (dz_ext[:, cols] * sc_ref[:, cols]).astype(BF16)
            _accumulate(dw_ref.at[grp], _dot_tn(p16, dzg[:tm]), first)
            dp = _dot_nt(dzg, w_ref[grp])
            s = dp / jnp.minimum(tok + 1, POOL_WINDOWS[grp]).astype(F32)
            for step in range(grp + 1):
                s = s + pltpu.roll(s, rows - (1 << step), 0)
            dy_ref[:, cols] = (s[:tm] - dp[:tm]).astype(BF16)
        _accumulate(dsc_ref, jnp.concatenate(dsc, axis=-1), first)

    tile = pl.BlockSpec((tm, d), lambda i: (i, 0))
    return pl.pallas_call(
        body,
        out_shape=(jax.ShapeDtypeStruct((t, d), BF16), jax.ShapeDtypeStruct((1, d), F32),
                   jax.ShapeDtypeStruct((len(POOL_WINDOWS), gd, gd), F32)),
        grid=(n_t,),
        in_specs=[tile, pl.BlockSpec((FFN_HALO, d), lambda i: (jnp.maximum(i * hb - 1, 0), 0)),
                  tile, pl.BlockSpec((FFN_HALO, d), lambda i: (jnp.minimum((i + 1) * hb, last_halo), 0)),
                  pl.BlockSpec((len(POOL_WINDOWS), gd, gd), lambda i: (0, 0, 0)),
                  pl.BlockSpec((1, d), lambda i: (0, 0))],
        out_specs=(tile, pl.BlockSpec((1, d), lambda i: (0, 0)),
                   pl.BlockSpec((len(POOL_WINDOWS), gd, gd), lambda i: (0, 0, 0))),
        name=name, compiler_params=_params(1))(y, y, dz, dz, wgrp, scale)


def _loss_head(xhat, rstd, g, b, target, name, tm=512):
    t, d = xhat.shape
    tm = min(tm, t)

    def body(xhat_ref, rstd_ref, g_ref, b_ref, tgt_ref, dr_ref, drb_ref, dg_ref, db_ref, sq_ref):
        first = pl.program_id(0) == 0
        xhat_t = xhat_ref[...]
        diff = xhat_t * g_ref[...] + b_ref[...] - tgt_ref[...]
        dr, dg, db = _ln_bwd(diff * (1.0 / d), xhat_t, rstd_ref[...], g_ref[...])
        dr_ref[...] = dr
        drb_ref[...] = dr.astype(BF16)
        _accumulate(dg_ref, dg, first)
        _accumulate(db_ref, db, first)
        _accumulate(sq_ref, jnp.sum(diff * diff, axis=0, keepdims=True), first)

    row = pl.BlockSpec((1, d), lambda i: (0, 0))
    tile = pl.BlockSpec((tm, d), lambda i: (i, 0))
    return pl.pallas_call(
        body,
        out_shape=(jax.ShapeDtypeStruct((t, d), F32), jax.ShapeDtypeStruct((t, d), BF16),
                   jax.ShapeDtypeStruct((1, d), F32), jax.ShapeDtypeStruct((1, d), F32),
                   jax.ShapeDtypeStruct((1, d), F32)),
        grid=(t // tm,),
        in_specs=[tile, pl.BlockSpec((tm, 1), lambda i: (i, 0)), row, row, tile],
        out_specs=(tile, tile, row, row, row),
        name=name, compiler_params=_params(1))(xhat, rstd, g, b, target)


def _my_place():
    return lax.axis_index("x"), lax.axis_index("y"), lax.axis_index("c")


def _flip(coord, bit):
    return 1 - coord if bit else coord


def _all_gather(arrays, name):
    n = len(arrays)

    def body(*refs):
        ins, outs = refs[:n], refs[n:2 * n]
        send_sems, recv_sems, local_sems = refs[2 * n:]
        x, y, c = _my_place()
        me, sibling = (x, y, c), (x, y, 1 - c)
        chips = [(1 - x, y), (x, 1 - y), (1 - x, 1 - y)]

        def copy(a, k, block, to, src=None):
            idx = 4 * block[0] + 2 * block[1] + block[2]
            return pltpu.make_async_remote_copy(
                src_ref=outs[a].at[idx] if src is None else src, dst_ref=outs[a].at[idx],
                send_sem=send_sems.at[a, k], recv_sem=recv_sems.at[a, k], device_id=to, device_id_type=MESH)

        mine, first, passed = [], [], []
        for a in range(n):
            cp = pltpu.make_async_copy(ins[a], outs[a].at[4 * x + 2 * y + c], local_sems.at[a])
            cp.start()
            mine.append(cp)
            first.append(copy(a, 0, me, sibling, src=ins[a]))
            first += [copy(a, 1 + j, me, (*chip, c), src=ins[a]) for j, chip in enumerate(chips)]
        for cp in first:
            cp.start()
        for j, chip in enumerate(chips):
            for a in range(n):
                copy(a, 1 + j, (*chip, c), me).wait_recv()
                cp = copy(a, 4 + j, (*chip, c), sibling)
                cp.start()
                passed.append(cp)
        for a in range(n):
            copy(a, 0, sibling, me).wait_recv()
            for j, chip in enumerate(chips):
                copy(a, 4 + j, (*chip, 1 - c), me).wait_recv()
        for cp in first + passed:
            cp.wait_send()
        for cp in mine:
            cp.wait()

    hbm = pl.BlockSpec(memory_space=pltpu.HBM)
    return pl.pallas_call(
        body, out_shape=tuple(jax.ShapeDtypeStruct((N_DEV,) + a.shape, a.dtype) for a in arrays),
        in_specs=[hbm] * n, out_specs=tuple([hbm] * n),
        scratch_shapes=[pltpu.SemaphoreType.DMA((n, 7)), pltpu.SemaphoreType.DMA((n, 7)),
                        pltpu.SemaphoreType.DMA((n,))],
        name=name)(*arrays)


def _exchange_pieces(arrays, name):
    n = len(arrays)

    def body(*refs):
        ins, outs = refs[:n], refs[n:2 * n]
        send_sems, recv_sems, local_sems = refs[2 * n:]
        x, y, c = _my_place()
        me = 4 * x + 2 * y + c
        peers = [(_flip(x, k & 4), _flip(y, k & 2), _flip(c, k & 1)) for k in range(1, N_DEV)]
        slots = [4 * p[0] + 2 * p[1] + p[2] for p in peers]
        local, sent = [], []
        for a in range(n):
            cp = pltpu.make_async_copy(ins[a].at[me], outs[a].at[me], local_sems.at[a])
            cp.start()
            local.append(cp)
        for k, peer in enumerate(peers):
            for a in range(n):
                cp = pltpu.make_async_remote_copy(
                    src_ref=ins[a].at[slots[k]], dst_ref=outs[a].at[me], send_sem=send_sems.at[a, k],
                    recv_sem=recv_sems.at[a, k], device_id=peer, device_id_type=MESH)
                cp.start()
                sent.append(cp)
        for k, peer in enumerate(peers):
            for a in range(n):
                pltpu.make_async_remote_copy(
                    src_ref=ins[a].at[slots[k]], dst_ref=outs[a].at[slots[k]], send_sem=send_sems.at[a, k],
                    recv_sem=recv_sems.at[a, k], device_id=peer, device_id_type=MESH).wait_recv()
        for cp in sent:
            cp.wait_send()
        for cp in local:
            cp.wait()

    hbm = pl.BlockSpec(memory_space=pltpu.HBM)
    return pl.pallas_call(
        body, out_shape=tuple(jax.ShapeDtypeStruct(a.shape, a.dtype) for a in arrays),
        in_specs=[hbm] * n, out_specs=tuple([hbm] * n),
        scratch_shapes=[pltpu.SemaphoreType.DMA((n, 7)), pltpu.SemaphoreType.DMA((n, 7)),
                        pltpu.SemaphoreType.DMA((n,))],
        name=name)(*arrays)


def _adamw(pieces, w, m, v, name, tr=128):
    r, lanes = w.shape
    tr = min(tr, r)
    c1 = 1.0 / (1.0 - ADAM_B1 ** ADAM_STEP)
    c2 = 1.0 / (1.0 - ADAM_B2 ** ADAM_STEP)

    def body(p_ref, w_ref, m_ref, v_ref, g_ref, d_ref, nm_ref, nv_ref):
        g = p_ref[0].astype(F32)
        for k in range(1, N_DEV):
            g = g + p_ref[k].astype(F32)
        nm = ADAM_B1 * m_ref[...] + (1.0 - ADAM_B1) * g
        nv = ADAM_B2 * v_ref[...] + (1.0 - ADAM_B2) * (g * g)
        g_ref[...] = g
        nm_ref[...] = nm
        nv_ref[...] = nv
        d_ref[...] = -ADAM_LR * ((nm * c1) / (jnp.sqrt(nv * c2) + ADAM_EPS) + ADAM_WD * w_ref[...])

    tile = pl.BlockSpec((tr, lanes), lambda i: (i, 0))
    out = jax.ShapeDtypeStruct((r, lanes), F32)
    return pl.pallas_call(
        body, out_shape=(out, out, out, out), grid=(r // tr,),
        in_specs=[pl.BlockSpec((N_DEV, tr, lanes), lambda i: (0, i, 0)), tile, tile, tile],
        out_specs=(tile, tile, tile, tile), name=name, compiler_params=_params(1))(pieces, w, m, v)


def _rows_of(numel, row_tile):
    rows = -(-numel // LANES)
    return -(-rows // row_tile) * row_tile


def _pack(flat_list, row_tile, lead=()):
    parts = []
    for a in flat_list:
        numel = a.shape[-1]
        rows = _rows_of(numel, row_tile)
        pad = [(0, 0)] * len(lead) + [(0, rows * LANES - numel)]
        parts.append(jnp.pad(a, pad).reshape(*lead, rows, LANES))
    return jnp.concatenate(parts, axis=len(lead))


def _unpack(buf, shapes, row_tile, lead=()):
    out, r0 = [], 0
    for shape in shapes:
        numel = 1
        for s in shape:
            numel *= s
        rows = _rows_of(numel, row_tile)
        part = lax.slice_in_dim(buf, r0, r0 + rows, axis=len(lead)).reshape(*lead, rows * LANES)
        out.append(lax.slice_in_dim(part, 0, numel, axis=len(lead)).reshape(*lead, *shape))
        r0 += rows
    return out


def _to_shards(full, axis):
    shape = full.shape
    cut = full.reshape(shape[:axis] + (N_DEV, shape[axis] // N_DEV) + shape[axis + 1:])
    return jnp.moveaxis(cut, axis, 0)


def _step(x, target, w, m, v):
    t, d = x.shape[1], x.shape[2]
    x2 = x.reshape(t, d)
    tgt2 = target.reshape(t, d)

    big = _pack([w[k].astype(BF16).reshape(-1) for k in GATHER_BF16], 16)
    small = _pack([w[k].reshape(-1) for k in GATHER_F32], 8)
    big_all, small_all = _all_gather([big, small], "all_gather_weights")
    gw = dict(zip(GATHER_BF16, _unpack(big_all, [w[k].shape for k in GATHER_BF16], 16, (N_DEV,))))
    gw.update(zip(GATHER_F32, _unpack(small_all, [w[k].shape for k in GATHER_F32], 8, (N_DEV,))))

    def full_cols(name, layer):
        a = gw[name][:, layer]
        if a.ndim == 2:
            return a.reshape(1, -1)
        return jnp.moveaxis(a, 0, 1).reshape(a.shape[1], -1)

    ones = jnp.ones((1, d), F32)
    zeros = jnp.zeros((1, d), F32)

    saved = []
    res, res_g, res_b = x2, ones, zeros
    xin = x2.astype(BF16)
    for i in range(DEPTH):
        kind, j = i % 3, i // 3
        sv = {"xin": xin, "kind": kind, "j": j}
        if kind == 0:
            w_in = gw["a_w_in"][:, j]
            w_out = gw["a_w_out"][:, j].reshape(1, d, d)
            dw, dwb = full_cols("a_dw", j), full_cols("a_dw_b", j)
            lg, lb = full_cols("a_ln_g", j), full_cols("a_ln_b", j)
            h1 = _mm_nn(xin, w_in, BF16, f"conv_in_{i}")
            s_act, q = _conv_fwd(h1, dw, dwb, lg, lb, f"conv_mix_{i}")
            sv.update(h1=h1, q=q, w_in=w_in, dw=dw, lg=lg, lb=lb)
        elif kind == 1:
            w_in = gw["b_w_in"][:, j]
            w_out = gw["b_w_out"][:, j].reshape(1, d, d)
            lg, lb = w["b_ln_g"][j].reshape(1, d), w["b_ln_b"][j].reshape(1, d)
            ws, bst = w["b_ws"][j], w["b_bs"][j].T
            h1 = _mm_nn(xin, w_in, BF16, f"sgu_in_{i}")
            s_act = _sgu_fwd(h1, lg, lb, ws, bst, f"sgu_mix_{i}")
            sv.update(h1=h1, w_in=w_in, lg=lg, lb=lb, ws=ws, bst=bst)
        else:
            w_in = gw["c_w_in"][:, j].reshape(1, d, d)
            w_out = gw["c_w_out"][:, j].reshape(1, d, d)
            wgrp = jnp.moveaxis(gw["c_w_grp"][:, j], 0, 1).reshape(4, d // 4, d // 4)
            scale = full_cols("c_scale", j)
            yp = _mm_nn(xin, w_in, F32, f"pool_in_{i}")[0]
            s_act = _pool_fwd(yp, wgrp, scale, f"pool_mix_{i}")
            sv.update(yp=yp, w_in=w_in, wgrp=wgrp, scale=scale)
        g1, b1 = w["ln1_g"][i].reshape(1, d), w["ln1_b"][i].reshape(1, d)
        xhat1, y1, rstd1 = _mm_res_ln(s_act.reshape(1, t, d), w_out, res, res_g, res_b, g1, b1, f"mix_out_ln_{i}")
        w_up = gw["f_w_up"][:, i]
        fdw = gw["f_dw"][:, i]
        n_ff = w_up.shape[-1]
        w_down = gw["f_w_down"][:, i].reshape(4, n_ff, d)
        g2, b2 = w["ln2_g"][i].reshape(1, d), w["ln2_b"][i].reshape(1, d)
        u, h_ffn, a_act, xhat2, y2, rstd2 = _ffn_fwd(y1, w_up, fdw, w_down, xhat1, g1, b1, g2, b2, f"ffn_fwd_{i}")
        sv.update(s_act=s_act, w_out=w_out, xhat1=xhat1, y1=y1, rstd1=rstd1, g1=g1, u=u, h_ffn=h_ffn, a_act=a_act,
                  w_up=w_up,
                  fdw=fdw, w_down=w_down, xhat2=xhat2, rstd2=rstd2, g2=g2, b2=b2)
        saved.append(sv)
        res, res_g, res_b, xin = xhat2, g2, b2, y2

    last = saved[-1]
    dr2, dr2b, dg2, db2, sq = _loss_head(last["xhat2"], last["rstd2"], last["g2"], last["b2"], tgt2, "loss_head")
    loss = lax.psum((0.5 / d) * jnp.sum(sq), ("x", "y", "c"))

    grads = {k: [None] * w[k].shape[0] for k in WEIGHTS}
    grad_x = None
    for i in reversed(range(DEPTH)):
        sv = saved[i]
        kind, j = sv["kind"], sv["j"]
        grads["ln2_g"][i], grads["ln2_b"][i] = dg2, db2
        da = _mm_nt_out(dr2b, sv["w_down"], f"ffn_da_{i}")
        grads["f_w_down"][i] = _to_shards(_mm_tn(sv["a_act"], dr2b.reshape(1, t, d), f"ffn_dwdown_{i}")
                                          .reshape(-1, d), 0)
        du, dfdw = _ffn_gate_bwd(sv["h_ffn"], sv["u"], da, sv["fdw"], f"ffn_gate_bwd_{i}")
        grads["f_dw"][i] = dfdw
        grads["f_w_up"][i] = _mm_tn(sv["y1"].reshape(1, t, d), du, f"ffn_dwup_{i}")
        dr1, dr1b, dg1, db1 = _mm_nt_lnb(du, sv["w_up"], dr2, sv["xhat1"], sv["rstd1"], sv["g1"], f"ffn_dx_ln_{i}")
        grads["ln1_g"][i], grads["ln1_b"][i] = dg1, db1
        ds = _mm_nt_out(dr1b, sv["w_out"], f"mix_ds_{i}")[0]
        dw_out = _to_shards(_mm_tn(sv["s_act"].reshape(1, t, d), dr1b.reshape(1, t, d), f"mix_dwout_{i}")[0], 0)
        xin3 = sv["xin"].reshape(1, t, d)
        if kind == 0:
            dh1, ddw, ddwb, dlg, dlb = _conv_bwd(ds, sv["q"], sv["h1"], sv["dw"], sv["lg"], sv["lb"],
                                                  f"conv_mix_bwd_{i}")
            grads["a_w_out"][j] = dw_out
            grads["a_dw"][j] = _to_shards(jnp.sum(ddw, axis=1), 1)
            grads["a_dw_b"][j] = _to_shards(ddwb[0], 0)
            grads["a_ln_g"][j] = _to_shards(dlg[0], 0)
            grads["a_ln_b"][j] = _to_shards(dlb[0], 0)
            grads["a_w_in"][j] = _mm_tn(xin3, dh1, f"conv_dwin_{i}")
            dh_in, w_in = dh1, sv["w_in"]
        elif kind == 1:
            dh1, dlg, dlb, dws, dbias = _sgu_bwd(sv["h1"], ds, sv["lg"], sv["lb"], sv["ws"], sv["bst"],
                                                  f"sgu_mix_bwd_{i}")
            grads["b_w_out"][j] = dw_out
            grads["b_ln_g"][j], grads["b_ln_b"][j] = dlg[0], dlb[0]
            grads["b_ws"][j] = dws
            grads["b_bs"][j] = jnp.sum(dbias.reshape(CHUNK, HEADS, CHUNK), axis=-1).T
            grads["b_w_in"][j] = _mm_tn(xin3, dh1, f"sgu_dwin_{i}")
            dh_in, w_in = dh1, sv["w_in"]
        else:
            dyp, dscale, dwgrp = _pool_bwd(sv["yp"], ds, sv["wgrp"], sv["scale"], f"pool_mix_bwd_{i}")
            grads["c_w_out"][j] = dw_out
            grads["c_scale"][j] = _to_shards(dscale[0], 0)
            grads["c_w_grp"][j] = _to_shards(dwgrp, 1)
            dh_in, w_in = dyp.reshape(1, t, d), sv["w_in"]
            grads["c_w_in"][j] = _to_shards(_mm_tn(xin3, dh_in, f"pool_dwin_{i}")[0], 0)
        if i > 0:
            prev = saved[i - 1]
            dr2, dr2b, dg2, db2 = _mm_nt_lnb(dh_in, w_in, dr1, prev["xhat2"], prev["rstd2"], prev["g2"],
                                              f"mix_dx_ln_{i}")
        else:
            grad_x = _mm_nt_res(dh_in, w_in, dr1, "mix_dx_0").reshape(x.shape)

    flat = {}
    for k in WEIGHTS:
        if k in REPLICATED:
            full = jnp.stack([gk.reshape(w[k].shape[1:]) for gk in grads[k]], axis=0)
            flat[k] = jnp.broadcast_to(full.reshape(1, -1), (N_DEV, full.size))
        else:
            flat[k] = jnp.stack(grads[k], axis=1).reshape(N_DEV, -1)
    groups = ([k for k in WEIGHTS if k in GATHER_BF16], [k for k in WEIGHTS if k not in GATHER_BF16])
    row_tiles, dtypes = (16, 8), (BF16, F32)
    pieces, pads = [], []
    for names, row_tile, dtype in zip(groups, row_tiles, dtypes):
        buf = _pack([flat[k].astype(dtype) for k in names], row_tile, (N_DEV,))
        pads.append(-(-buf.shape[1] // 128) * 128 - buf.shape[1])
        pieces.append(jnp.pad(buf, ((0, 0), (0, pads[-1]), (0, 0))))
    received = _exchange_pieces(pieces, "exchange_gradients")

    result = {}
    for names, row_tile, pad_rows, got, tag in zip(groups, row_tiles, pads, received, ("matrices", "vectors")):
        def packed(tree):
            return jnp.pad(_pack([tree[k].reshape(-1) for k in names], row_tile), ((0, pad_rows), (0, 0)))

        bufs = _adamw(got, packed(w), packed(m), packed(v), f"adamw_{tag}")
        shapes = [w[k].shape for k in names]
        for kind, buf in zip(("grad", "delta", "new_m", "new_v"), bufs):
            result.update({(kind, k): a for k, a in zip(names, _unpack(buf, shapes, row_tile))})
    outs = [result[(kind, k)] for kind in ("grad", "delta", "new_m", "new_v") for k in WEIGHTS]
    return (loss, grad_x, *outs)


def kernel(x, a_w_in, a_dw, a_dw_b, a_ln_g, a_ln_b, a_w_out, b_w_in, b_ln_g, b_ln_b, b_ws, b_bs, b_w_out, c_w_in, c_w_grp, c_scale, c_w_out, f_w_up, f_dw, f_w_down, ln1_g, ln1_b, ln2_g, ln2_b, loss_target, m_a_w_in, m_a_dw, m_a_dw_b, m_a_ln_g, m_a_ln_b, m_a_w_out, m_b_w_in, m_b_ln_g, m_b_ln_b, m_b_ws, m_b_bs, m_b_w_out, m_c_w_in, m_c_w_grp, m_c_scale, m_c_w_out, m_f_w_up, m_f_dw, m_f_w_down, m_ln1_g, m_ln1_b, m_ln2_g, m_ln2_b, v_a_w_in, v_a_dw, v_a_dw_b, v_a_ln_g, v_a_ln_b, v_a_w_out, v_b_w_in, v_b_ln_g, v_b_ln_b, v_b_ws, v_b_bs, v_b_w_out, v_c_w_in, v_c_w_grp, v_c_scale, v_c_w_out, v_f_w_up, v_f_dw, v_f_w_down, v_ln1_g, v_ln1_b, v_ln2_g, v_ln2_b):
    w = dict(zip(WEIGHTS, (a_w_in, a_dw, a_dw_b, a_ln_g, a_ln_b, a_w_out, b_w_in, b_ln_g, b_ln_b, b_ws, b_bs, b_w_out,
                           c_w_in, c_w_grp, c_scale, c_w_out, f_w_up, f_dw, f_w_down, ln1_g, ln1_b, ln2_g, ln2_b)))
    m = dict(zip(WEIGHTS, (m_a_w_in, m_a_dw, m_a_dw_b, m_a_ln_g, m_a_ln_b, m_a_w_out, m_b_w_in, m_b_ln_g, m_b_ln_b,
                           m_b_ws, m_b_bs, m_b_w_out, m_c_w_in, m_c_w_grp, m_c_scale, m_c_w_out, m_f_w_up, m_f_dw,
                           m_f_w_down, m_ln1_g, m_ln1_b, m_ln2_g, m_ln2_b)))
    v = dict(zip(WEIGHTS, (v_a_w_in, v_a_dw, v_a_dw_b, v_a_ln_g, v_a_ln_b, v_a_w_out, v_b_w_in, v_b_ln_g, v_b_ln_b,
                           v_b_ws, v_b_bs, v_b_w_out, v_c_w_in, v_c_w_grp, v_c_scale, v_c_w_out, v_f_w_up, v_f_dw,
                           v_f_w_down, v_ln1_g, v_ln1_b, v_ln2_g, v_ln2_b)))
    return _step(x, loss_target, w, m, v)
```

```python
import functools

import jax
import jax.numpy as jnp
from jax import lax
from jax.experimental import pallas as pl
from jax.experimental.pallas import tpu as pltpu

N_DEV = 8
DEPTH = 4
ALPHA = float((2 * DEPTH) ** 0.25)
LN_EPS = 1e-5
CONV_W = 31
CONV_HALO = 32
FFN_HALO = 16
POOL_WINDOWS = (2, 4, 8, 16)
CHUNK = 128
HEADS = 8
LANES = 1024
ADAM_LR, ADAM_B1, ADAM_B2, ADAM_EPS, ADAM_WD, ADAM_STEP = 0.001, 0.9, 0.999, 1e-08, 0.01, 10
VMEM_LIMIT = 56 * 1024 * 1024
F32, BF16 = jnp.float32, jnp.bfloat16
MESH = pl.DeviceIdType.MESH

WEIGHTS = ['a_w_in', 'a_dw', 'a_dw_b', 'a_ln_g', 'a_ln_b', 'a_w_out', 'b_w_in', 'b_ln_g', 'b_ln_b', 'b_ws', 'b_bs',
           'b_w_out', 'c_w_in', 'c_w_grp', 'c_scale', 'c_w_out', 'f_w_up', 'f_dw', 'f_w_down', 'ln1_g', 'ln1_b',
           'ln2_g', 'ln2_b']
REPLICATED = ('b_ln_g', 'b_ln_b', 'b_ws', 'b_bs', 'ln1_g', 'ln1_b', 'ln2_g', 'ln2_b')
GATHER_BF16 = ('a_w_in', 'a_w_out', 'b_w_in', 'b_w_out', 'c_w_in', 'c_w_grp', 'c_w_out', 'f_w_up', 'f_w_down')
GATHER_F32 = ('a_dw', 'a_dw_b', 'a_ln_g', 'a_ln_b', 'c_scale', 'f_dw')


def _params(n_axes):
    return pltpu.CompilerParams(dimension_semantics=("arbitrary",) * n_axes, vmem_limit_bytes=VMEM_LIMIT)


def _resident(shape):
    zeros = (0,) * len(shape)
    return pl.BlockSpec(shape, lambda i: zeros, pipeline_mode=pl.Buffered(1))


def _sigmoid(x):
    return 1.0 / (1.0 + jnp.exp(-x))


def _gelu(x):
    return 0.5 * x * (1.0 + lax.erf(x * 0.7071067811865476))


def _gelu_grad(x):
    return 0.5 * (1.0 + lax.erf(x * 0.7071067811865476)) + x * jnp.exp(-0.5 * x * x) * 0.3989422804014327


def _ln_stats(r):
    mu = jnp.mean(r, axis=-1, keepdims=True)
    xc = r - mu
    var = jnp.mean(xc * xc, axis=-1, keepdims=True)
    rstd = lax.rsqrt(var + LN_EPS)
    return xc * rstd, rstd


def _ln_bwd(dy, xhat, rstd, g):
    dxhat = dy * g
    m1 = jnp.mean(dxhat, axis=-1, keepdims=True)
    m2 = jnp.mean(dxhat * xhat, axis=-1, keepdims=True)
    dr = rstd * (dxhat - m1 - xhat * m2)
    return dr, jnp.sum(dy * xhat, axis=0, keepdims=True), jnp.sum(dy, axis=0, keepdims=True)


def _accumulate(ref, value, first):
    @pl.when(first)
    def _():
        ref[...] = value

    @pl.when(jnp.logical_not(first))
    def _():
        ref[...] += value


def _dot(a, b):
    return jnp.dot(a, b, preferred_element_type=F32)


def _dot_nt(a, b):
    return lax.dot_general(a, b, (((1,), (1,)), ((), ())), preferred_element_type=F32)


def _dot_tn(a, b):
    return lax.dot_general(a, b, (((0,), (0,)), ((), ())), preferred_element_type=F32)


def _cat_lanes(h, lo, hi):
    return jnp.concatenate([h[s] for s in range(lo, hi)], axis=-1)


def _mm_nn(x, w, out_dtype, name, tm=512):
    t, k = x.shape
    s_n, _, n = w.shape
    tm = min(tm, t)

    def body(x_ref, w_ref, o_ref):
        x_tile = x_ref[...]
        for s in range(s_n):
            o_ref[s] = _dot(x_tile, w_ref[s]).astype(o_ref.dtype)

    return pl.pallas_call(
        body, out_shape=jax.ShapeDtypeStruct((s_n, t, n), out_dtype), grid=(t // tm,),
        in_specs=[pl.BlockSpec((tm, k), lambda i: (i, 0)), _resident((s_n, k, n))],
        out_specs=pl.BlockSpec((s_n, tm, n), lambda i: (0, i, 0)),
        name=name, compiler_params=_params(1))(x, w)


def _mm_res_ln(a, w, res, gp, bp, g, b, name, tm=512):
    s_n, t, ka = a.shape
    d = w.shape[-1]
    tm = min(tm, t)

    def body(a_ref, w_ref, res_ref, gp_ref, bp_ref, g_ref, b_ref, xhat_ref, y_ref, rstd_ref):
        acc = _dot(a_ref[0], w_ref[0])
        for s in range(1, s_n):
            acc += _dot(a_ref[s], w_ref[s])
        r = ALPHA * (res_ref[...] * gp_ref[...] + bp_ref[...]) + acc
        xhat, rstd = _ln_stats(r)
        xhat_ref[...] = xhat
        y_ref[...] = (xhat * g_ref[...] + b_ref[...]).astype(BF16)
        rstd_ref[...] = rstd

    row = pl.BlockSpec((1, d), lambda i: (0, 0))
    tile = pl.BlockSpec((tm, d), lambda i: (i, 0))
    return pl.pallas_call(
        body,
        out_shape=(jax.ShapeDtypeStruct((t, d), F32), jax.ShapeDtypeStruct((t, d), BF16),
                   jax.ShapeDtypeStruct((t, 1), F32)),
        grid=(t // tm,),
        in_specs=[pl.BlockSpec((s_n, tm, ka), lambda i: (0, i, 0)), _resident((s_n, ka, d)),
                  tile, row, row, row, row],
        out_specs=(tile, tile, pl.BlockSpec((tm, 1), lambda i: (i, 0))),
        name=name, compiler_params=_params(1))(a, w, res, gp, bp, g, b)


def _mm_nt_out(x, w, name, tm=512):
    t, n = x.shape
    s_n, k, _ = w.shape
    tm = min(tm, t)

    def body(x_ref, w_ref, o_ref):
        x_tile = x_ref[...]
        for s in range(s_n):
            o_ref[s] = _dot_nt(x_tile, w_ref[s]).astype(o_ref.dtype)

    return pl.pallas_call(
        body, out_shape=jax.ShapeDtypeStruct((s_n, t, k), BF16), grid=(t // tm,),
        in_specs=[pl.BlockSpec((tm, n), lambda i: (i, 0)), _resident((s_n, k, n))],
        out_specs=pl.BlockSpec((s_n, tm, k), lambda i: (0, i, 0)),
        name=name, compiler_params=_params(1))(x, w)


def _mm_nt_lnb(dh, w, drn, xhat, rstd, g, name, tm=512):
    s_n, t, n = dh.shape
    k = w.shape[1]
    tm = min(tm, t)

    def body(dh_ref, w_ref, drn_ref, xhat_ref, rstd_ref, g_ref, dr_ref, drb_ref, dg_ref, db_ref):
        first = pl.program_id(0) == 0
        acc = _dot_nt(dh_ref[0], w_ref[0])
        for s in range(1, s_n):
            acc += _dot_nt(dh_ref[s], w_ref[s])
        dy = acc + ALPHA * drn_ref[...]
        dr, dg, db = _ln_bwd(dy, xhat_ref[...], rstd_ref[...], g_ref[...])
        dr_ref[...] = dr
        drb_ref[...] = dr.astype(BF16)
        _accumulate(dg_ref, dg, first)
        _accumulate(db_ref, db, first)

    tile = pl.BlockSpec((tm, k), lambda i: (i, 0))
    row = pl.BlockSpec((1, k), lambda i: (0, 0))
    return pl.pallas_call(
        body,
        out_shape=(jax.ShapeDtypeStruct((t, k), F32), jax.ShapeDtypeStruct((t, k), BF16),
                   jax.ShapeDtypeStruct((1, k), F32), jax.ShapeDtypeStruct((1, k), F32)),
        grid=(t // tm,),
        in_specs=[pl.BlockSpec((s_n, tm, n), lambda i: (0, i, 0)), _resident((s_n, k, n)),
                  tile, tile, pl.BlockSpec((tm, 1), lambda i: (i, 0)), row],
        out_specs=(tile, tile, row, row),
        name=name, compiler_params=_params(1))(dh, w, drn, xhat, rstd, g)


def _mm_nt_res(dh, w, drn, name, tm=512):
    s_n, t, n = dh.shape
    k = w.shape[1]
    tm = min(tm, t)

    def body(dh_ref, w_ref, drn_ref, o_ref):
        acc = _dot_nt(dh_ref[0], w_ref[0])
        for s in range(1, s_n):
            acc += _dot_nt(dh_ref[s], w_ref[s])
        o_ref[...] = acc + ALPHA * drn_ref[...]

    tile = pl.BlockSpec((tm, k), lambda i: (i, 0))
    return pl.pallas_call(
        body, out_shape=jax.ShapeDtypeStruct((t, k), F32), grid=(t // tm,),
        in_specs=[pl.BlockSpec((s_n, tm, n), lambda i: (0, i, 0)), _resident((s_n, k, n)), tile],
        out_specs=tile, name=name, compiler_params=_params(1))(dh, w, drn)


def _mm_tn(lhs, rhs, name, tm=2048):
    sl, t, kl = lhs.shape
    sr, _, n = rhs.shape
    s_n = max(sl, sr)
    tm = min(tm, t)

    def body(l_ref, r_ref, o_ref):
        _accumulate(o_ref, _dot_tn(l_ref[...], r_ref[...]), pl.program_id(1) == 0)

    return pl.pallas_call(
        body, out_shape=jax.ShapeDtypeStruct((s_n, kl, n), F32), grid=(s_n, t // tm),
        in_specs=[pl.BlockSpec((None, tm, kl), (lambda s, i: (s, i, 0)) if sl > 1 else (lambda s, i: (0, i, 0))),
                  pl.BlockSpec((None, tm, n), (lambda s, i: (s, i, 0)) if sr > 1 else (lambda s, i: (0, i, 0)))],
        out_specs=pl.BlockSpec((None, kl, n), lambda s, i: (s, 0, 0)),
        name=name, compiler_params=_params(2))(lhs, rhs)


def _glu(h):
    return _cat_lanes(h, 0, 4).astype(F32) * _sigmoid(_cat_lanes(h, 4, 8).astype(F32))


def _shifted_windows(src_ref, r0, c0, row_block, offsets):
    span = row_block + CONV_HALO
    big = src_ref[pl.ds(r0, span), pl.ds(c0, 128)]
    for sub in range(8):
        taps = [k for k, o in enumerate(offsets) if o % 8 == sub]
        if not taps:
            continue
        rolled = big if sub == 0 else pltpu.roll(big, span - sub, 0)
        for k in taps:
            lo = offsets[k] - sub
            yield k, rolled[lo:lo + row_block]


def _tap_loop(src_ref, dst_ref, weight_ref, rows, offsets, weight_rows, row_block=64):
    d = dst_ref.shape[-1]

    def block(cb, carry):
        c0 = pl.multiple_of(cb * 128, 128)
        for r0 in range(0, rows, row_block):
            acc = jnp.zeros((row_block, 128), F32)
            for k, window in _shifted_windows(src_ref, r0, c0, row_block, offsets):
                acc += weight_ref[pl.ds(weight_rows[k], 1), pl.ds(c0, 128)] * window
            dst_ref[pl.ds(r0, row_block), pl.ds(c0, 128)] = acc
        return carry

    lax.fori_loop(0, d // 128, block, 0)


def _conv_fwd(h1, dw, dwb, g, b, name, tm=256):
    _, t, _ = h1.shape
    d = dw.shape[-1]
    tm = min(tm, t)
    hb = tm // CONV_HALO

    def body(h_ref, halo_ref, dw_ref, dwb_ref, g_ref, b_ref, s_ref, q_ref, ext_ref):
        i = pl.program_id(0)
        ext_ref[pl.ds(0, CONV_HALO), :] = _glu(halo_ref[...]) * (i > 0).astype(F32)
        ext_ref[pl.ds(CONV_HALO, tm), :] = _glu(h_ref[...])
        _tap_loop(ext_ref, q_ref, dw_ref, tm, [2 + k for k in range(CONV_W)], list(range(CONV_W)))
        q = q_ref[...] + dwb_ref[...]
        q_ref[...] = q
        qhat, _ = _ln_stats(q)
        z = qhat * g_ref[...] + b_ref[...]
        s_ref[...] = (z * _sigmoid(z)).astype(BF16)

    row = pl.BlockSpec((1, d), lambda i: (0, 0))
    tile = pl.BlockSpec((tm, d), lambda i: (i, 0))
    return pl.pallas_call(
        body, out_shape=(jax.ShapeDtypeStruct((t, d), BF16), jax.ShapeDtypeStruct((t, d), F32)), grid=(t // tm,),
        in_specs=[pl.BlockSpec((8, tm, 256), lambda i: (0, i, 0)),
                  pl.BlockSpec((8, CONV_HALO, 256), lambda i: (0, jnp.maximum(i * hb - 1, 0), 0)),
                  pl.BlockSpec((CONV_W, d), lambda i: (0, 0)), row, row, row],
        out_specs=(tile, tile), scratch_shapes=[pltpu.VMEM((tm + CONV_HALO, d), F32)],
        name=name, compiler_params=_params(1))(h1, h1, dw, dwb, g, b)


def _conv_bwd(ds, q, h1, dw, g, b, name, tm=256):
    _, t, _ = h1.shape
    d = dw.shape[-1]
    tm = min(tm, t)
    hb = tm // CONV_HALO
    n_t = t // tm
    last_halo = t // CONV_HALO - 1

    def body(ds_ref, dsn_ref, q_ref, qn_ref, h_ref, hp_ref, dw_ref, g_ref, b_ref,
             dh_ref, ddw_ref, ddwb_ref, dg_ref, db_ref, dq_ref, p_ref, dp_ref):
        i = pl.program_id(0)
        first = i == 0
        valid = (i < n_t - 1).astype(F32)

        def dq_rows(ds_rows, q_rows, scale):
            qhat, rstd = _ln_stats(q_rows)
            z = qhat * g_ref[...] + b_ref[...]
            sg = _sigmoid(z)
            dz = ds_rows.astype(F32) * (sg * (1.0 + z * (1.0 - sg))) * scale
            dq, dg, db = _ln_bwd(dz, qhat, rstd, g_ref[...])
            return dq, dg, db

        dq, dg, db = dq_rows(ds_ref[...], q_ref[...], 1.0)
        dq_ref[pl.ds(0, tm), :] = dq
        dq_ref[pl.ds(tm, CONV_HALO), :] = dq_rows(dsn_ref[...], qn_ref[...], valid)[0]
        _accumulate(dg_ref, dg, first)
        _accumulate(db_ref, db, first)
        _accumulate(ddwb_ref, jnp.sum(dq, axis=0, keepdims=True), first)

        p_ref[pl.ds(0, CONV_HALO), :] = _glu(hp_ref[...]) * (i > 0).astype(F32)
        p_ref[pl.ds(CONV_HALO, tm), :] = _glu(h_ref[...])

        _tap_loop(dq_ref, dp_ref, dw_ref, tm, list(range(CONV_W)), [CONV_W - 1 - o for o in range(CONV_W)])

        @pl.when(first)
        def _():
            ddw_ref[...] = jnp.zeros_like(ddw_ref)

        row_block = 64

        def block(cb, carry):
            c0 = pl.multiple_of(cb * 128, 128)
            for r0 in range(0, tm, row_block):
                dqb = dq_ref[pl.ds(r0, row_block), pl.ds(c0, 128)]
                for k, window in _shifted_windows(p_ref, r0, c0, row_block, [2 + k for k in range(CONV_W)]):
                    prod = dqb * window
                    ddw_ref[k, :, pl.ds(c0, 128)] += jnp.sum(prod.reshape(row_block // 8, 8, 128), axis=0)
            return carry

        lax.fori_loop(0, d // 128, block, 0)

        h = h_ref[...]
        a = _cat_lanes(h, 0, 4).astype(F32)
        sg = _sigmoid(_cat_lanes(h, 4, 8).astype(F32))
        dp = dp_ref[...]
        da = (dp * sg).astype(BF16)
        dgate = (dp * a * sg * (1.0 - sg)).astype(BF16)
        for s in range(4):
            dh_ref[s] = da[:, s * 256:(s + 1) * 256]
            dh_ref[4 + s] = dgate[:, s * 256:(s + 1) * 256]

    row = pl.BlockSpec((1, d), lambda i: (0, 0))
    tile = pl.BlockSpec((tm, d), lambda i: (i, 0))
    nxt = pl.BlockSpec((CONV_HALO, d), lambda i: (jnp.minimum((i + 1) * hb, last_halo), 0))
    return pl.pallas_call(
        body,
        out_shape=(jax.ShapeDtypeStruct((8, t, 256), BF16), jax.ShapeDtypeStruct((CONV_W, 8, d), F32),
                   jax.ShapeDtypeStruct((1, d), F32), jax.ShapeDtypeStruct((1, d), F32),
                   jax.ShapeDtypeStruct((1, d), F32)),
        grid=(n_t,),
        in_specs=[tile, nxt, tile, nxt,
                  pl.BlockSpec((8, tm, 256), lambda i: (0, i, 0)),
                  pl.BlockSpec((8, CONV_HALO, 256), lambda i: (0, jnp.maximum(i * hb - 1, 0), 0)),
                  pl.BlockSpec((CONV_W, d), lambda i: (0, 0)), row, row],
        out_specs=(pl.BlockSpec((8, tm, 256), lambda i: (0, i, 0)),
                   pl.BlockSpec((CONV_W, 8, d), lambda i: (0, 0, 0)), row, row, row),
        scratch_shapes=[pltpu.VMEM((tm + CONV_HALO, d), F32), pltpu.VMEM((tm + CONV_HALO, d), F32),
                        pltpu.VMEM((tm, d), F32)],
        name=name, compiler_params=_params(1))(ds, ds, q, q, h1, h1, dw, g, b)


def _conv3(ext, dw):
    e1 = pltpu.roll(ext, 1, 0)
    e2 = pltpu.roll(ext, 2, 0)
    return dw[2:3] * ext + dw[1:2] * e1 + dw[0:1] * e2, e1, e2


def _ffn_fwd(x, w_up, fdw, w_down, res, gp, bp, g, b, name, tm=256):
    t, d = x.shape
    _, _, n = w_up.shape
    tm = min(tm, t)
    carry_rows = 8

    def body(x_ref, wu_ref, dw_ref, wd_ref, res_ref, gp_ref, bp_ref, g_ref, b_ref,
             u_ref, h_ref, a_ref, xhat_ref, y_ref, rstd_ref, carry_ref):
        @pl.when(pl.program_id(0) == 0)
        def _():
            carry_ref[...] = jnp.zeros_like(carry_ref)

        x_tile = x_ref[...]
        acc = None
        for j in range(4):
            h = []
            for s in (j, 4 + j):
                ub = _dot(x_tile, wu_ref[s]).astype(BF16)
                u_ref[s] = ub
                uf = ub.astype(F32)
                ext = jnp.concatenate([carry_ref[s], uf], axis=0)
                carry_ref[s] = uf[tm - carry_rows:]
                hb = _conv3(ext, dw_ref[s])[0][carry_rows:].astype(BF16)
                h_ref[s] = hb
                h.append(hb.astype(F32))
            a = (h[0] * _sigmoid(h[0]) * h[1]).astype(BF16)
            a_ref[j] = a
            part = _dot(a, wd_ref[j])
            acc = part if acc is None else acc + part
        r = ALPHA * (res_ref[...] * gp_ref[...] + bp_ref[...]) + acc
        xhat, rstd = _ln_stats(r)
        xhat_ref[...] = xhat
        y_ref[...] = (xhat * g_ref[...] + b_ref[...]).astype(BF16)
        rstd_ref[...] = rstd

    row = pl.BlockSpec((1, d), lambda i: (0, 0))
    tile = pl.BlockSpec((tm, d), lambda i: (i, 0))
    return pl.pallas_call(
        body,
        out_shape=(jax.ShapeDtypeStruct((8, t, n), BF16), jax.ShapeDtypeStruct((8, t, n), BF16),
                   jax.ShapeDtypeStruct((4, t, n), BF16), jax.ShapeDtypeStruct((t, d), F32),
                   jax.ShapeDtypeStruct((t, d), BF16), jax.ShapeDtypeStruct((t, 1), F32)),
        grid=(t // tm,),
        in_specs=[tile, _resident((8, d, n)), pl.BlockSpec((8, 3, n), lambda i: (0, 0, 0)), _resident((4, n, d)),
                  tile, row, row, row, row],
        out_specs=(pl.BlockSpec((8, tm, n), lambda i: (0, i, 0)), pl.BlockSpec((8, tm, n), lambda i: (0, i, 0)),
                   pl.BlockSpec((4, tm, n), lambda i: (0, i, 0)),
                   tile, tile, pl.BlockSpec((tm, 1), lambda i: (i, 0))),
        scratch_shapes=[pltpu.VMEM((8, carry_rows, n), F32)],
        name=name, compiler_params=_params(1))(x, w_up, fdw, w_down, res, gp, bp, g, b)


def _ffn_gate_bwd(h, u, da, fdw, name, tm=256):
    _, t, n = u.shape
    tm = min(tm, t)
    hb = tm // FFN_HALO
    n_t = t // tm
    last_halo = t // FFN_HALO - 1
    h4, u4 = h.reshape(2, 4, t, n), u.reshape(2, 4, t, n)
    fdw4 = fdw.reshape(2, 4, 3, n)
    rows = tm + FFN_HALO

    def body(h_ref, hn_ref, u_ref, da_ref, dan_ref, dw_ref, du_ref, ddw_ref):
        i = pl.program_id(1)
        keep_next = (i < n_t - 1).astype(F32)
        hg = jnp.concatenate([h_ref[0], hn_ref[0]], axis=0).astype(F32)
        hv = jnp.concatenate([h_ref[1], hn_ref[1]], axis=0).astype(F32)
        da_ext = jnp.concatenate([da_ref[...].astype(F32), dan_ref[...].astype(F32) * keep_next], axis=0)
        sg = _sigmoid(hg)
        silu = hg * sg
        dh = (da_ext * hv * (sg + silu * (1.0 - sg)), da_ext * silu)
        for p in range(2):
            dwp = dw_ref[p]
            d1 = pltpu.roll(dh[p], rows - 1, 0)
            d2 = pltpu.roll(dh[p], rows - 2, 0)
            du_ref[p] = (dwp[2:3] * dh[p] + dwp[1:2] * d1 + dwp[0:1] * d2)[:tm].astype(BF16)
            up = u_ref[p].astype(F32)
            part = jnp.concatenate([jnp.sum(d[:tm] * up, axis=0, keepdims=True) for d in (d2, d1, dh[p])], axis=0)
            _accumulate(ddw_ref.at[p], part, i == 0)

    tile = pl.BlockSpec((2, None, tm, n), lambda j, i: (0, j, i, 0))
    nxt = pl.BlockSpec((2, None, FFN_HALO, n), lambda j, i: (0, j, jnp.minimum((i + 1) * hb, last_halo), 0))
    du, ddw = pl.pallas_call(
        body, out_shape=(jax.ShapeDtypeStruct((2, 4, t, n), BF16), jax.ShapeDtypeStruct((2, 4, 3, n), F32)),
        grid=(4, n_t),
        in_specs=[tile, nxt, tile,
                  pl.BlockSpec((None, tm, n), lambda j, i: (j, i, 0)),
                  pl.BlockSpec((None, FFN_HALO, n), lambda j, i: (j, jnp.minimum((i + 1) * hb, last_halo), 0)),
                  pl.BlockSpec((2, None, 3, n), lambda j, i: (0, j, 0, 0))],
        out_specs=(tile, pl.BlockSpec((2, None, 3, n), lambda j, i: (0, j, 0, 0))),
        name=name, compiler_params=_params(2))(h4, h4, u4, da, da, fdw4)
    return du.reshape(8, t, n), ddw.reshape(8, 3, n)


def _tril_mask():
    r = lax.broadcasted_iota(jnp.int32, (CHUNK, CHUNK), 0)
    c = lax.broadcasted_iota(jnp.int32, (CHUNK, CHUNK), 1)
    return (r >= c).astype(F32)


def _sgu_fwd(h1, g, b, ws, bst, name, tm=256):
    _, t, _ = h1.shape
    d = g.shape[-1]
    tm = min(tm, t)

    def body(h_ref, g_ref, b_ref, ws_ref, bst_ref, m_ref):
        h = h_ref[...]
        u = _gelu(_cat_lanes(h, 0, 4).astype(F32))
        v = _gelu(_cat_lanes(h, 4, 8).astype(F32))
        vn = (_ln_stats(v)[0] * g_ref[...] + b_ref[...]).astype(BF16)
        mask = _tril_mask()
        for hh in range(HEADS):
            cols = slice(hh * CHUNK, (hh + 1) * CHUNK)
            wm = (ws_ref[hh] * mask).astype(BF16)
            bias = bst_ref[:, hh:hh + 1]
            for c in range(tm // CHUNK):
                rows = slice(c * CHUNK, (c + 1) * CHUNK)
                sblk = _dot(wm, vn[rows, cols]) + bias
                m_ref[rows, cols] = (u[rows, cols] * sblk).astype(BF16)

    row = pl.BlockSpec((1, d), lambda i: (0, 0))
    return pl.pallas_call(
        body, out_shape=jax.ShapeDtypeStruct((t, d), BF16), grid=(t // tm,),
        in_specs=[pl.BlockSpec((8, tm, 256), lambda i: (0, i, 0)), row, row,
                  pl.BlockSpec((HEADS, CHUNK, CHUNK), lambda i: (0, 0, 0)),
                  pl.BlockSpec((CHUNK, HEADS), lambda i: (0, 0))],
        out_specs=pl.BlockSpec((tm, d), lambda i: (i, 0)),
        name=name, compiler_params=_params(1))(h1, g, b, ws, bst)


def _sgu_bwd(h1, dm, g, b, ws, bst, name, tm=256):
    _, t, _ = h1.shape
    d = g.shape[-1]
    tm = min(tm, t)

    def body(h_ref, dm_ref, g_ref, b_ref, ws_ref, bst_ref, dh_ref, dg_ref, db_ref, dws_ref, dbias_ref,
             du_ref, dvn_ref):
        first = pl.program_id(0) == 0
        h = h_ref[...]
        zu = _cat_lanes(h, 0, 4).astype(F32)
        zv = _cat_lanes(h, 4, 8).astype(F32)
        u = _gelu(zu)
        vhat, rstd = _ln_stats(_gelu(zv))
        vn = (vhat * g_ref[...] + b_ref[...]).astype(BF16)
        dm = dm_ref[...].astype(F32)
        mask = _tril_mask()

        @pl.when(first)
        def _():
            dws_ref[...] = jnp.zeros_like(dws_ref)
            dbias_ref[...] = jnp.zeros_like(dbias_ref)

        for hh in range(HEADS):
            cols = slice(hh * CHUNK, (hh + 1) * CHUNK)
            wm = (ws_ref[hh] * mask).astype(BF16)
            bias = bst_ref[:, hh:hh + 1]
            for c in range(tm // CHUNK):
                rows = slice(c * CHUNK, (c + 1) * CHUNK)
                vb = vn[rows, cols]
                sblk = _dot(wm, vb) + bias
                dmb = dm[rows, cols]
                du_ref[rows, cols] = dmb * sblk
                dsb = dmb * u[rows, cols]
                dbias_ref[:, cols] += dsb
                dsb16 = dsb.astype(BF16)
                dws_ref[hh] += _dot_nt(dsb16, vb) * mask
                dvn_ref[rows, cols] = _dot_tn(wm, dsb16)

        dvn = dvn_ref[...]
        dv, dg, db = _ln_bwd(dvn, vhat, rstd, g_ref[...])
        _accumulate(dg_ref, dg, first)
        _accumulate(db_ref, db, first)
        dzu = (du_ref[...] * _gelu_grad(zu)).astype(BF16)
        dzv = (dv * _gelu_grad(zv)).astype(BF16)
        for s in range(4):
            dh_ref[s] = dzu[:, s * 256:(s + 1) * 256]
            dh_ref[4 + s] = dzv[:, s * 256:(s + 1) * 256]

    row = pl.BlockSpec((1, d), lambda i: (0, 0))
    tile = pl.BlockSpec((tm, d), lambda i: (i, 0))
    h_tile = pl.BlockSpec((8, tm, 256), lambda i: (0, i, 0))
    return pl.pallas_call(
        body,
        out_shape=(jax.ShapeDtypeStruct((8, t, 256), BF16), jax.ShapeDtypeStruct((1, d), F32),
                   jax.ShapeDtypeStruct((1, d), F32), jax.ShapeDtypeStruct((HEADS, CHUNK, CHUNK), F32),
                   jax.ShapeDtypeStruct((CHUNK, d), F32)),
        grid=(t // tm,),
        in_specs=[h_tile, tile, row, row, pl.BlockSpec((HEADS, CHUNK, CHUNK), lambda i: (0, 0, 0)),
                  pl.BlockSpec((CHUNK, HEADS), lambda i: (0, 0))],
        out_specs=(h_tile, row, row, pl.BlockSpec((HEADS, CHUNK, CHUNK), lambda i: (0, 0, 0)),
                   pl.BlockSpec((CHUNK, d), lambda i: (0, 0))),
        scratch_shapes=[pltpu.VMEM((tm, d), F32), pltpu.VMEM((tm, d), F32)],
        name=name, compiler_params=_params(1))(h1, dm, g, b, ws, bst)


def _pool_minus_self(ext, first_token, grp):
    s = ext
    for step in range(grp + 1):
        s = s + pltpu.roll(s, 1 << step, 0)
    rows = ext.shape[0] - FFN_HALO
    tok = first_token + lax.broadcasted_iota(jnp.int32, (rows, 1), 0)
    count = jnp.minimum(tok + 1, POOL_WINDOWS[grp]).astype(F32)
    return s[FFN_HALO:] / count - ext[FFN_HALO:]


def _pool_fwd(y, wgrp, scale, name, tm=256):
    t, d = y.shape
    tm = min(tm, t)
    hb = tm // FFN_HALO
    gd = d // len(POOL_WINDOWS)

    def body(y_ref, yp_ref, w_ref, sc_ref, z_ref):
        i = pl.program_id(0)
        ext = jnp.concatenate([yp_ref[...] * (i > 0).astype(F32), y_ref[...]], axis=0)
        for grp in range(len(POOL_WINDOWS)):
            cols = slice(grp * gd, (grp + 1) * gd)
            p = _pool_minus_self(ext[:, cols], i * tm, grp)
            z_ref[:, cols] = (_dot(p.astype(BF16), w_ref[grp]) * sc_ref[:, cols]).astype(BF16)

    return pl.pallas_call(
        body, out_shape=jax.ShapeDtypeStruct((t, d), BF16), grid=(t // tm,),
        in_specs=[pl.BlockSpec((tm, d), lambda i: (i, 0)),
                  pl.BlockSpec((FFN_HALO, d), lambda i: (jnp.maximum(i * hb - 1, 0), 0)),
                  pl.BlockSpec((len(POOL_WINDOWS), gd, gd), lambda i: (0, 0, 0)),
                  pl.BlockSpec((1, d), lambda i: (0, 0))],
        out_specs=pl.BlockSpec((tm, d), lambda i: (i, 0)),
        name=name, compiler_params=_params(1))(y, y, wgrp, scale)


def _pool_bwd(y, dz, wgrp, scale, name, tm=256):
    t, d = y.shape
    tm = min(tm, t)
    hb = tm // FFN_HALO
    n_t = t // tm
    last_halo = t // FFN_HALO - 1
    gd = d // len(POOL_WINDOWS)
    rows = tm + FFN_HALO

    def body(y_ref, yp_ref, dz_ref, dzn_ref, w_ref, sc_ref, dy_ref, dsc_ref, dw_ref):
        i = pl.program_id(0)
        first = i == 0
        ext = jnp.concatenate([yp_ref[...] * (i > 0).astype(F32), y_ref[...]], axis=0)
        dz_ext = jnp.concatenate([dz_ref[...].astype(F32), dzn_ref[...].astype(F32) * (i < n_t - 1).astype(F32)],
                                 axis=0)
        tok = i * tm + lax.broadcasted_iota(jnp.int32, (rows, 1), 0)
        dsc = []
        for grp in range(len(POOL_WINDOWS)):
            cols = slice(grp * gd, (grp + 1) * gd)
            p16 = _pool_minus_self(ext[:, cols], i * tm, grp).astype(BF16)
            zg = _dot(p16, w_ref[grp])
            dsc.append(jnp.sum(dz_ext[:tm, cols] * zg, axis=0, keepdims=True))
            dzg = (dz_ext[:, cols] * sc_ref[:, cols]).astype(BF16)
            _accumulate(dw_ref.at[grp], _dot_tn(p16, dzg[:tm]), first)
            dp = _dot_nt(dzg, w_ref[grp])
            s = dp / jnp.minimum(tok + 1, POOL_WINDOWS[grp]).astype(F32)
            for step in range(grp + 1):
                s = s + pltpu.roll(s, rows - (1 << step), 0)
            dy_ref[:, cols] = (s[:tm] - dp[:tm]).astype(BF16)
        _accumulate(dsc_ref, jnp.concatenate(dsc, axis=-1), first)

    tile = pl.BlockSpec((tm, d), lambda i: (i, 0))
    return pl.pallas_call(
        body,
        out_shape=(jax.ShapeDtypeStruct((t, d), BF16), jax.ShapeDtypeStruct((1, d), F32),
                   jax.ShapeDtypeStruct((len(POOL_WINDOWS), gd, gd), F32)),
        grid=(n_t,),
        in_specs=[tile, pl.BlockSpec((FFN_HALO, d), lambda i: (jnp.maximum(i * hb - 1, 0), 0)),
                  tile, pl.BlockSpec((FFN_HALO, d), lambda i: (jnp.minimum((i + 1) * hb, last_halo), 0)),
                  pl.BlockSpec((len(POOL_WINDOWS), gd, gd), lambda i: (0, 0, 0)),
                  pl.BlockSpec((1, d), lambda i: (0, 0))],
        out_specs=(tile, pl.BlockSpec((1, d), lambda i: (0, 0)),
                   pl.BlockSpec((len(POOL_WINDOWS), gd, gd), lambda i: (0, 0, 0))),
        name=name, compiler_params=_params(1))(y, y, dz, dz, wgrp, scale)


def _loss_head(xhat, rstd, g, b, target, name, tm=512):
    t, d = xhat.shape
    tm = min(tm, t)

    def body(xhat_ref, rstd_ref, g_ref, b_ref, tgt_ref, dr_ref, drb_ref, dg_ref, db_ref, sq_ref):
        first = pl.program_id(0) == 0
        xhat_t = xhat_ref[...]
        diff = xhat_t * g_ref[...] + b_ref[...] - tgt_ref[...]
        dr, dg, db = _ln_bwd(diff * (1.0 / d), xhat_t, rstd_ref[...], g_ref[...])
        dr_ref[...] = dr
        drb_ref[...] = dr.astype(BF16)
        _accumulate(dg_ref, dg, first)
        _accumulate(db_ref, db, first)
        _accumulate(sq_ref, jnp.sum(diff * diff, axis=0, keepdims=True), first)

    row = pl.BlockSpec((1, d), lambda i: (0, 0))
    tile = pl.BlockSpec((tm, d), lambda i: (i, 0))
    return pl.pallas_call(
        body,
        out_shape=(jax.ShapeDtypeStruct((t, d), F32), jax.ShapeDtypeStruct((t, d), BF16),
                   jax.ShapeDtypeStruct((1, d), F32), jax.ShapeDtypeStruct((1, d), F32),
                   jax.ShapeDtypeStruct((1, d), F32)),
        grid=(t // tm,),
        in_specs=[tile, pl.BlockSpec((tm, 1), lambda i: (i, 0)), row, row, tile],
        out_specs=(tile, tile, row, row, row),
        name=name, compiler_params=_params(1))(xhat, rstd, g, b, target)


def _my_place():
    return lax.axis_index("x"), lax.axis_index("y"), lax.axis_index("c")


def _flip(coord, bit):
    return 1 - coord if bit else coord


def _all_gather(arrays, name):
    n = len(arrays)

    def body(*refs):
        ins, outs = refs[:n], refs[n:2 * n]
        send_sems, recv_sems, local_sems = refs[2 * n:]
        x, y, c = _my_place()
        me, sibling = (x, y, c), (x, y, 1 - c)
        chips = [(1 - x, y), (x, 1 - y), (1 - x, 1 - y)]

        def copy(a, k, block, to, src=None):
            idx = 4 * block[0] + 2 * block[1] + block[2]
            return pltpu.make_async_remote_copy(
                src_ref=outs[a].at[idx] if src is None else src, dst_ref=outs[a].at[idx],
                send_sem=send_sems.at[a, k], recv_sem=recv_sems.at[a, k], device_id=to, device_id_type=MESH)

        mine, first, passed = [], [], []
        for a in range(n):
            cp = pltpu.make_async_copy(ins[a], outs[a].at[4 * x + 2 * y + c], local_sems.at[a])
            cp.start()
            mine.append(cp)
            first.append(copy(a, 0, me, sibling, src=ins[a]))
            first += [copy(a, 1 + j, me, (*chip, c), src=ins[a]) for j, chip in enumerate(chips)]
        for cp in first:
            cp.start()
        for j, chip in enumerate(chips):
            for a in range(n):
                copy(a, 1 + j, (*chip, c), me).wait_recv()
                cp = copy(a, 4 + j, (*chip, c), sibling)
                cp.start()
                passed.append(cp)
        for a in range(n):
            copy(a, 0, sibling, me).wait_recv()
            for j, chip in enumerate(chips):
                copy(a, 4 + j, (*chip, 1 - c), me).wait_recv()
        for cp in first + passed:
            cp.wait_send()
        for cp in mine:
            cp.wait()

    hbm = pl.BlockSpec(memory_space=pltpu.HBM)
    return pl.pallas_call(
        body, out_shape=tuple(jax.ShapeDtypeStruct((N_DEV,) + a.shape, a.dtype) for a in arrays),
        in_specs=[hbm] * n, out_specs=tuple([hbm] * n),
        scratch_shapes=[pltpu.SemaphoreType.DMA((n, 7)), pltpu.SemaphoreType.DMA((n, 7)),
                        pltpu.SemaphoreType.DMA((n,))],
        name=name)(*arrays)


def _exchange_pieces(arrays, name):
    n = len(arrays)

    def body(*refs):
        ins, outs = refs[:n], refs[n:2 * n]
        send_sems, recv_sems, local_sems = refs[2 * n:]
        x, y, c = _my_place()
        me = 4 * x + 2 * y + c
        peers = [(_flip(x, k & 4), _flip(y, k & 2), _flip(c, k & 1)) for k in range(1, N_DEV)]
        slots = [4 * p[0] + 2 * p[1] + p[2] for p in peers]
        local, sent = [], []
        for a in range(n):
            cp = pltpu.make_async_copy(ins[a].at[me], outs[a].at[me], local_sems.at[a])
            cp.start()
            local.append(cp)
        for k, peer in enumerate(peers):
            for a in range(n):
                cp = pltpu.make_async_remote_copy(
                    src_ref=ins[a].at[slots[k]], dst_ref=outs[a].at[me], send_sem=send_sems.at[a, k],
                    recv_sem=recv_sems.at[a, k], device_id=peer, device_id_type=MESH)
                cp.start()
                sent.append(cp)
        for k, peer in enumerate(peers):
            for a in range(n):
                pltpu.make_async_remote_copy(
                    src_ref=ins[a].at[slots[k]], dst_ref=outs[a].at[slots[k]], send_sem=send_sems.at[a, k],
                    recv_sem=recv_sems.at[a, k], device_id=peer, device_id_type=MESH).wait_recv()
        for cp in sent:
            cp.wait_send()
        for cp in local:
            cp.wait()

    hbm = pl.BlockSpec(memory_space=pltpu.HBM)
    return pl.pallas_call(
        body, out_shape=tuple(jax.ShapeDtypeStruct(a.shape, a.dtype) for a in arrays),
        in_specs=[hbm] * n, out_specs=tuple([hbm] * n),
        scratch_shapes=[pltpu.SemaphoreType.DMA((n, 7)), pltpu.SemaphoreType.DMA((n, 7)),
                        pltpu.SemaphoreType.DMA((n,))],
        name=name)(*arrays)


def _adamw(pieces, w, m, v, name, tr=128):
    r, lanes = w.shape
    tr = min(tr, r)
    c1 = 1.0 / (1.0 - ADAM_B1 ** ADAM_STEP)
    c2 = 1.0 / (1.0 - ADAM_B2 ** ADAM_STEP)

    def body(p_ref, w_ref, m_ref, v_ref, g_ref, d_ref, nm_ref, nv_ref):
        g = p_ref[0].astype(F32)
        for k in range(1, N_DEV):
            g = g + p_ref[k].astype(F32)
        nm = ADAM_B1 * m_ref[...] + (1.0 - ADAM_B1) * g
        nv = ADAM_B2 * v_ref[...] + (1.0 - ADAM_B2) * (g * g)
        g_ref[...] = g
        nm_ref[...] = nm
        nv_ref[...] = nv
        d_ref[...] = -ADAM_LR * ((nm * c1) / (jnp.sqrt(nv * c2) + ADAM_EPS) + ADAM_WD * w_ref[...])

    tile = pl.BlockSpec((tr, lanes), lambda i: (i, 0))
    out = jax.ShapeDtypeStruct((r, lanes), F32)
    return pl.pallas_call(
        body, out_shape=(out, out, out, out), grid=(r // tr,),
        in_specs=[pl.BlockSpec((N_DEV, tr, lanes), lambda i: (0, i, 0)), tile, tile, tile],
        out_specs=(tile, tile, tile, tile), name=name, compiler_params=_params(1))(pieces, w, m, v)


def _rows_of(numel, row_tile):
    rows = -(-numel // LANES)
    return -(-rows // row_tile) * row_tile


def _pack(flat_list, row_tile, lead=()):
    parts = []
    for a in flat_list:
        numel = a.shape[-1]
        rows = _rows_of(numel, row_tile)
        pad = [(0, 0)] * len(lead) + [(0, rows * LANES - numel)]
        parts.append(jnp.pad(a, pad).reshape(*lead, rows, LANES))
    return jnp.concatenate(parts, axis=len(lead))


def _unpack(buf, shapes, row_tile, lead=()):
    out, r0 = [], 0
    for shape in shapes:
        numel = 1
        for s in shape:
            numel *= s
        rows = _rows_of(numel, row_tile)
        part = lax.slice_in_dim(buf, r0, r0 + rows, axis=len(lead)).reshape(*lead, rows * LANES)
        out.append(lax.slice_in_dim(part, 0, numel, axis=len(lead)).reshape(*lead, *shape))
        r0 += rows
    return out


def _to_shards(full, axis):
    shape = full.shape
    cut = full.reshape(shape[:axis] + (N_DEV, shape[axis] // N_DEV) + shape[axis + 1:])
    return jnp.moveaxis(cut, axis, 0)


def _step(x, target, w, m, v):
    t, d = x.shape[1], x.shape[2]
    x2 = x.reshape(t, d)
    tgt2 = target.reshape(t, d)

    small = _pack([w[k].reshape(-1) for k in GATHER_F32], 8)
    gathered = _all_gather([w[k].astype(BF16) for k in GATHER_BF16] + [small], "all_gather_weights")
    gw = dict(zip(GATHER_BF16, gathered[:-1]))
    gw.update(zip(GATHER_F32, _unpack(gathered[-1], [w[k].shape for k in GATHER_F32], 8, (N_DEV,))))

    def full_cols(name, layer):
        a = gw[name][:, layer]
        if a.ndim == 2:
            return a.reshape(1, -1)
        return jnp.moveaxis(a, 0, 1).reshape(a.shape[1], -1)

    ones = jnp.ones((1, d), F32)
    zeros = jnp.zeros((1, d), F32)

    saved = []
    res, res_g, res_b = x2, ones, zeros
    xin = x2.astype(BF16)
    for i in range(DEPTH):
        kind, j = i % 3, i // 3
        sv = {"xin": xin, "kind": kind, "j": j}
        if kind == 0:
            w_in = gw["a_w_in"][:, j]
            w_out = gw["a_w_out"][:, j].reshape(1, d, d)
            dw, dwb = full_cols("a_dw", j), full_cols("a_dw_b", j)
            lg, lb = full_cols("a_ln_g", j), full_cols("a_ln_b", j)
            h1 = _mm_nn(xin, w_in, BF16, f"conv_in_{i}")
            s_act, q = _conv_fwd(h1, dw, dwb, lg, lb, f"conv_mix_{i}")
            sv.update(h1=h1, q=q, w_in=w_in, dw=dw, lg=lg, lb=lb)
        elif kind == 1:
            w_in = gw["b_w_in"][:, j]
            w_out = gw["b_w_out"][:, j].reshape(1, d, d)
            lg, lb = w["b_ln_g"][j].reshape(1, d), w["b_ln_b"][j].reshape(1, d)
            ws, bst = w["b_ws"][j], w["b_bs"][j].T
            h1 = _mm_nn(xin, w_in, BF16, f"sgu_in_{i}")
            s_act = _sgu_fwd(h1, lg, lb, ws, bst, f"sgu_mix_{i}")
            sv.update(h1=h1, w_in=w_in, lg=lg, lb=lb, ws=ws, bst=bst)
        else:
            w_in = gw["c_w_in"][:, j].reshape(1, d, d)
            w_out = gw["c_w_out"][:, j].reshape(1, d, d)
            wgrp = jnp.moveaxis(gw["c_w_grp"][:, j], 0, 1).reshape(4, d // 4, d // 4)
            scale = full_cols("c_scale", j)
            yp = _mm_nn(xin, w_in, F32, f"pool_in_{i}")[0]
            s_act = _pool_fwd(yp, wgrp, scale, f"pool_mix_{i}")
            sv.update(yp=yp, w_in=w_in, wgrp=wgrp, scale=scale)
        g1, b1 = w["ln1_g"][i].reshape(1, d), w["ln1_b"][i].reshape(1, d)
        xhat1, y1, rstd1 = _mm_res_ln(s_act.reshape(1, t, d), w_out, res, res_g, res_b, g1, b1, f"mix_out_ln_{i}")
        w_up = gw["f_w_up"][:, i]
        fdw = gw["f_dw"][:, i]
        n_ff = w_up.shape[-1]
        w_down = gw["f_w_down"][:, i].reshape(4, n_ff, d)
        g2, b2 = w["ln2_g"][i].reshape(1, d), w["ln2_b"][i].reshape(1, d)
        u, h_ffn, a_act, xhat2, y2, rstd2 = _ffn_fwd(y1, w_up, fdw, w_down, xhat1, g1, b1, g2, b2, f"ffn_fwd_{i}")
        sv.update(s_act=s_act, w_out=w_out, xhat1=xhat1, y1=y1, rstd1=rstd1, g1=g1, u=u, h_ffn=h_ffn, a_act=a_act,
                  w_up=w_up,
                  fdw=fdw, w_down=w_down, xhat2=xhat2, rstd2=rstd2, g2=g2, b2=b2)
        saved.append(sv)
        res, res_g, res_b, xin = xhat2, g2, b2, y2

    last = saved[-1]
    dr2, dr2b, dg2, db2, sq = _loss_head(last["xhat2"], last["rstd2"], last["g2"], last["b2"], tgt2, "loss_head")
    loss = lax.psum((0.5 / d) * jnp.sum(sq), ("x", "y", "c"))

    grads = {k: [None] * w[k].shape[0] for k in WEIGHTS}
    grad_x = None
    for i in reversed(range(DEPTH)):
        sv = saved[i]
        kind, j = sv["kind"], sv["j"]
        grads["ln2_g"][i], grads["ln2_b"][i] = dg2, db2
        da = _mm_nt_out(dr2b, sv["w_down"], f"ffn_da_{i}")
        grads["f_w_down"][i] = _to_shards(_mm_tn(sv["a_act"], dr2b.reshape(1, t, d), f"ffn_dwdown_{i}")
                                          .reshape(-1, d), 0)
        du, dfdw = _ffn_gate_bwd(sv["h_ffn"], sv["u"], da, sv["fdw"], f"ffn_gate_bwd_{i}")
        grads["f_dw"][i] = dfdw
        grads["f_w_up"][i] = _mm_tn(sv["y1"].reshape(1, t, d), du, f"ffn_dwup_{i}")
        dr1, dr1b, dg1, db1 = _mm_nt_lnb(du, sv["w_up"], dr2, sv["xhat1"], sv["rstd1"], sv["g1"], f"ffn_dx_ln_{i}")
        grads["ln1_g"][i], grads["ln1_b"][i] = dg1, db1
        ds = _mm_nt_out(dr1b, sv["w_out"], f"mix_ds_{i}")[0]
        dw_out = _to_shards(_mm_tn(sv["s_act"].reshape(1, t, d), dr1b.reshape(1, t, d), f"mix_dwout_{i}")[0], 0)
        xin3 = sv["xin"].reshape(1, t, d)
        if kind == 0:
            dh1, ddw, ddwb, dlg, dlb = _conv_bwd(ds, sv["q"], sv["h1"], sv["dw"], sv["lg"], sv["lb"],
                                                  f"conv_mix_bwd_{i}")
            grads["a_w_out"][j] = dw_out
            grads["a_dw"][j] = _to_shards(jnp.sum(ddw, axis=1), 1)
            grads["a_dw_b"][j] = _to_shards(ddwb[0], 0)
            grads["a_ln_g"][j] = _to_shards(dlg[0], 0)
            grads["a_ln_b"][j] = _to_shards(dlb[0], 0)
            grads["a_w_in"][j] = _mm_tn(xin3, dh1, f"conv_dwin_{i}")
            dh_in, w_in = dh1, sv["w_in"]
        elif kind == 1:
            dh1, dlg, dlb, dws, dbias = _sgu_bwd(sv["h1"], ds, sv["lg"], sv["lb"], sv["ws"], sv["bst"],
                                                  f"sgu_mix_bwd_{i}")
            grads["b_w_out"][j] = dw_out
            grads["b_ln_g"][j], grads["b_ln_b"][j] = dlg[0], dlb[0]
            grads["b_ws"][j] = dws
            grads["b_bs"][j] = jnp.sum(dbias.reshape(CHUNK, HEADS, CHUNK), axis=-1).T
            grads["b_w_in"][j] = _mm_tn(xin3, dh1, f"sgu_dwin_{i}")
            dh_in, w_in = dh1, sv["w_in"]
        else:
            dyp, dscale, dwgrp = _pool_bwd(sv["yp"], ds, sv["wgrp"], sv["scale"], f"pool_mix_bwd_{i}")
            grads["c_w_out"][j] = dw_out
            grads["c_scale"][j] = _to_shards(dscale[0], 0)
            grads["c_w_grp"][j] = _to_shards(dwgrp, 1)
            dh_in, w_in = dyp.reshape(1, t, d), sv["w_in"]
            grads["c_w_in"][j] = _to_shards(_mm_tn(xin3, dh_in, f"pool_dwin_{i}")[0], 0)
        if i > 0:
            prev = saved[i - 1]
            dr2, dr2b, dg2, db2 = _mm_nt_lnb(dh_in, w_in, dr1, prev["xhat2"], prev["rstd2"], prev["g2"],
                                              f"mix_dx_ln_{i}")
        else:
            grad_x = _mm_nt_res(dh_in, w_in, dr1, "mix_dx_0").reshape(x.shape)

    kinds = ("grad", "delta", "new_m", "new_v")
    small_names = [k for k in WEIGHTS if k not in GATHER_BF16]
    flat = []
    for k in small_names:
        if k in REPLICATED:
            full = jnp.stack([gk.reshape(w[k].shape[1:]) for gk in grads[k]], axis=0)
            flat.append(jnp.broadcast_to(full.reshape(1, -1), (N_DEV, full.size)))
        else:
            flat.append(jnp.stack(grads[k], axis=1).reshape(N_DEV, -1))
    small_pieces = _pack(flat, 8, (N_DEV,))
    pad_rows = -(-small_pieces.shape[1] // 128) * 128 - small_pieces.shape[1]
    small_pieces = jnp.pad(small_pieces, ((0, 0), (0, pad_rows), (0, 0)))
    mat_pieces = [jnp.stack(grads[k], axis=1).astype(BF16) for k in GATHER_BF16]
    received = _exchange_pieces(mat_pieces + [small_pieces], "exchange_gradients")

    result = {}
    for k, got in zip(GATHER_BF16, received[:-1]):
        cols = w[k].shape[-1]
        bufs = _adamw(got.reshape(N_DEV, -1, cols), w[k].reshape(-1, cols), m[k].reshape(-1, cols),
                      v[k].reshape(-1, cols), f"adamw_{k}")
        result.update({(kind, k): buf.reshape(w[k].shape) for kind, buf in zip(kinds, bufs)})

    def packed(tree):
        return jnp.pad(_pack([tree[k].reshape(-1) for k in small_names], 8), ((0, pad_rows), (0, 0)))

    bufs = _adamw(received[-1], packed(w), packed(m), packed(v), "adamw_small")
    shapes = [w[k].shape for k in small_names]
    for kind, buf in zip(kinds, bufs):
        result.update({(kind, k): a for k, a in zip(small_names, _unpack(buf, shapes, 8))})
    outs = [result[(kind, k)] for kind in kinds for k in WEIGHTS]
    return (loss, grad_x, *outs)


def kernel(x, a_w_in, a_dw, a_dw_b, a_ln_g, a_ln_b, a_w_out, b_w_in, b_ln_g, b_ln_b, b_ws, b_bs, b_w_out, c_w_in, c_w_grp, c_scale, c_w_out, f_w_up, f_dw, f_w_down, ln1_g, ln1_b, ln2_g, ln2_b, loss_target, m_a_w_in, m_a_dw, m_a_dw_b, m_a_ln_g, m_a_ln_b, m_a_w_out, m_b_w_in, m_b_ln_g, m_b_ln_b, m_b_ws, m_b_bs, m_b_w_out, m_c_w_in, m_c_w_grp, m_c_scale, m_c_w_out, m_f_w_up, m_f_dw, m_f_w_down, m_ln1_g, m_ln1_b, m_ln2_g, m_ln2_b, v_a_w_in, v_a_dw, v_a_dw_b, v_a_ln_g, v_a_ln_b, v_a_w_out, v_b_w_in, v_b_ln_g, v_b_ln_b, v_b_ws, v_b_bs, v_b_w_out, v_c_w_in, v_c_w_grp, v_c_scale, v_c_w_out, v_f_w_up, v_f_dw, v_f_w_down, v_ln1_g, v_ln1_b, v_ln2_g, v_ln2_b):
    w = dict(zip(WEIGHTS, (a_w_in, a_dw, a_dw_b, a_ln_g, a_ln_b, a_w_out, b_w_in, b_ln_g, b_ln_b, b_ws, b_bs, b_w_out,
                           c_w_in, c_w_grp, c_scale, c_w_out, f_w_up, f_dw, f_w_down, ln1_g, ln1_b, ln2_g, ln2_b)))
    m = dict(zip(WEIGHTS, (m_a_w_in, m_a_dw, m_a_dw_b, m_a_ln_g, m_a_ln_b, m_a_w_out, m_b_w_in, m_b_ln_g, m_b_ln_b,
                           m_b_ws, m_b_bs, m_b_w_out, m_c_w_in, m_c_w_grp, m_c_scale, m_c_w_out, m_f_w_up, m_f_dw,
                           m_f_w_down, m_ln1_g, m_ln1_b, m_ln2_g, m_ln2_b)))
    v = dict(zip(WEIGHTS, (v_a_w_in, v_a_dw, v_a_dw_b, v_a_ln_g, v_a_ln_b, v_a_w_out, v_b_w_in, v_b_ln_g, v_b_ln_b,
                           v_b_ws, v_b_bs, v_b_w_out, v_c_w_in, v_c_w_grp, v_c_scale, v_c_w_out, v_f_w_up, v_f_dw,
                           v_f_w_down, v_ln1_g, v_ln1_b, v_ln2_g, v_ln2_b)))
    return _step(x, loss_target, w, m, v)
```

```python
import functools

import jax
import jax.numpy as jnp
from jax import lax
from jax.experimental import pallas as pl
from jax.experimental.pallas import tpu as pltpu

N_DEV = 8
DEPTH = 4
ALPHA = float((2 * DEPTH) ** 0.25)
LN_EPS = 1e-5
CONV_W = 31
CONV_HALO = 32
FFN_HALO = 16
POOL_WINDOWS = (2, 4, 8, 16)
CHUNK = 128
HEADS = 8
LANES = 1024
ADAM_LR, ADAM_B1, ADAM_B2, ADAM_EPS, ADAM_WD, ADAM_STEP = 0.001, 0.9, 0.999, 1e-08, 0.01, 10
VMEM_LIMIT = 56 * 1024 * 1024
F32, BF16 = jnp.float32, jnp.bfloat16
MESH = pl.DeviceIdType.MESH

WEIGHTS = ['a_w_in', 'a_dw', 'a_dw_b', 'a_ln_g', 'a_ln_b', 'a_w_out', 'b_w_in', 'b_ln_g', 'b_ln_b', 'b_ws', 'b_bs',
           'b_w_out', 'c_w_in', 'c_w_grp', 'c_scale', 'c_w_out', 'f_w_up', 'f_dw', 'f_w_down', 'ln1_g', 'ln1_b',
           'ln2_g', 'ln2_b']
REPLICATED = ('b_ln_g', 'b_ln_b', 'b_ws', 'b_bs', 'ln1_g', 'ln1_b', 'ln2_g', 'ln2_b')
GATHER_BF16 = ('a_w_in', 'a_w_out', 'b_w_in', 'b_w_out', 'c_w_in', 'c_w_grp', 'c_w_out', 'f_w_up', 'f_w_down')
GATHER_F32 = ('a_dw', 'a_dw_b', 'a_ln_g', 'a_ln_b', 'c_scale', 'f_dw')


def _params(n_axes):
    return pltpu.CompilerParams(dimension_semantics=("arbitrary",) * n_axes, vmem_limit_bytes=VMEM_LIMIT)


def _resident(shape):
    zeros = (0,) * len(shape)
    return pl.BlockSpec(shape, lambda i: zeros, pipeline_mode=pl.Buffered(1))


def _sigmoid(x):
    return 1.0 / (1.0 + jnp.exp(-x))


def _gelu(x):
    return 0.5 * x * (1.0 + lax.erf(x * 0.7071067811865476))


def _gelu_grad(x):
    return 0.5 * (1.0 + lax.erf(x * 0.7071067811865476)) + x * jnp.exp(-0.5 * x * x) * 0.3989422804014327


def _ln_stats(r):
    mu = jnp.mean(r, axis=-1, keepdims=True)
    xc = r - mu
    var = jnp.mean(xc * xc, axis=-1, keepdims=True)
    rstd = lax.rsqrt(var + LN_EPS)
    return xc * rstd, rstd


def _ln_bwd(dy, xhat, rstd, g):
    dxhat = dy * g
    m1 = jnp.mean(dxhat, axis=-1, keepdims=True)
    m2 = jnp.mean(dxhat * xhat, axis=-1, keepdims=True)
    dr = rstd * (dxhat - m1 - xhat * m2)
    return dr, jnp.sum(dy * xhat, axis=0, keepdims=True), jnp.sum(dy, axis=0, keepdims=True)


def _accumulate(ref, value, first):
    @pl.when(first)
    def _():
        ref[...] = value

    @pl.when(jnp.logical_not(first))
    def _():
        ref[...] += value


def _dot(a, b):
    return jnp.dot(a, b, preferred_element_type=F32)


def _dot_nt(a, b):
    return lax.dot_general(a, b, (((1,), (1,)), ((), ())), preferred_element_type=F32)


def _dot_tn(a, b):
    return lax.dot_general(a, b, (((0,), (0,)), ((), ())), preferred_element_type=F32)


def _cat_lanes(h, lo, hi):
    return jnp.concatenate([h[s] for s in range(lo, hi)], axis=-1)


def _mm_nn(x, w, out_dtype, name, tm=512):
    t, k = x.shape
    s_n, _, n = w.shape
    tm = min(tm, t)

    def body(x_ref, w_ref, o_ref):
        x_tile = x_ref[...]
        for s in range(s_n):
            o_ref[s] = _dot(x_tile, w_ref[s]).astype(o_ref.dtype)

    return pl.pallas_call(
        body, out_shape=jax.ShapeDtypeStruct((s_n, t, n), out_dtype), grid=(t // tm,),
        in_specs=[pl.BlockSpec((tm, k), lambda i: (i, 0)), _resident((s_n, k, n))],
        out_specs=pl.BlockSpec((s_n, tm, n), lambda i: (0, i, 0)),
        name=name, compiler_params=_params(1))(x, w)


def _mm_res_ln(a, w, res, gp, bp, g, b, name, tm=512):
    s_n, t, ka = a.shape
    d = w.shape[-1]
    tm = min(tm, t)

    def body(a_ref, w_ref, res_ref, gp_ref, bp_ref, g_ref, b_ref, xhat_ref, y_ref, rstd_ref):
        acc = _dot(a_ref[0], w_ref[0])
        for s in range(1, s_n):
            acc += _dot(a_ref[s], w_ref[s])
        r = ALPHA * (res_ref[...] * gp_ref[...] + bp_ref[...]) + acc
        xhat, rstd = _ln_stats(r)
        xhat_ref[...] = xhat
        y_ref[...] = (xhat * g_ref[...] + b_ref[...]).astype(BF16)
        rstd_ref[...] = rstd

    row = pl.BlockSpec((1, d), lambda i: (0, 0))
    tile = pl.BlockSpec((tm, d), lambda i: (i, 0))
    return pl.pallas_call(
        body,
        out_shape=(jax.ShapeDtypeStruct((t, d), F32), jax.ShapeDtypeStruct((t, d), BF16),
                   jax.ShapeDtypeStruct((t, 1), F32)),
        grid=(t // tm,),
        in_specs=[pl.BlockSpec((s_n, tm, ka), lambda i: (0, i, 0)), _resident((s_n, ka, d)),
                  tile, row, row, row, row],
        out_specs=(tile, tile, pl.BlockSpec((tm, 1), lambda i: (i, 0))),
        name=name, compiler_params=_params(1))(a, w, res, gp, bp, g, b)


def _mm_nt_out(x, w, name, tm=512):
    t, n = x.shape
    s_n, k, _ = w.shape
    tm = min(tm, t)

    def body(x_ref, w_ref, o_ref):
        x_tile = x_ref[...]
        for s in range(s_n):
            o_ref[s] = _dot_nt(x_tile, w_ref[s]).astype(o_ref.dtype)

    return pl.pallas_call(
        body, out_shape=jax.ShapeDtypeStruct((s_n, t, k), BF16), grid=(t // tm,),
        in_specs=[pl.BlockSpec((tm, n), lambda i: (i, 0)), _resident((s_n, k, n))],
        out_specs=pl.BlockSpec((s_n, tm, k), lambda i: (0, i, 0)),
        name=name, compiler_params=_params(1))(x, w)


def _mm_nt_lnb(dh, w, drn, xhat, rstd, g, name, tm=512):
    s_n, t, n = dh.shape
    k = w.shape[1]
    tm = min(tm, t)

    def body(dh_ref, w_ref, drn_ref, xhat_ref, rstd_ref, g_ref, dr_ref, drb_ref, dg_ref, db_ref):
        first = pl.program_id(0) == 0
        acc = _dot_nt(dh_ref[0], w_ref[0])
        for s in range(1, s_n):
            acc += _dot_nt(dh_ref[s], w_ref[s])
        dy = acc + ALPHA * drn_ref[...]
        dr, dg, db = _ln_bwd(dy, xhat_ref[...], rstd_ref[...], g_ref[...])
        dr_ref[...] = dr
        drb_ref[...] = dr.astype(BF16)
        _accumulate(dg_ref, dg, first)
        _accumulate(db_ref, db, first)

    tile = pl.BlockSpec((tm, k), lambda i: (i, 0))
    row = pl.BlockSpec((1, k), lambda i: (0, 0))
    return pl.pallas_call(
        body,
        out_shape=(jax.ShapeDtypeStruct((t, k), F32), jax.ShapeDtypeStruct((t, k), BF16),
                   jax.ShapeDtypeStruct((1, k), F32), jax.ShapeDtypeStruct((1, k), F32)),
        grid=(t // tm,),
        in_specs=[pl.BlockSpec((s_n, tm, n), lambda i: (0, i, 0)), _resident((s_n, k, n)),
                  tile, tile, pl.BlockSpec((tm, 1), lambda i: (i, 0)), row],
        out_specs=(tile, tile, row, row),
        name=name, compiler_params=_params(1))(dh, w, drn, xhat, rstd, g)


def _mm_nt_res(dh, w, drn, name, tm=512):
    s_n, t, n = dh.shape
    k = w.shape[1]
    tm = min(tm, t)

    def body(dh_ref, w_ref, drn_ref, o_ref):
        acc = _dot_nt(dh_ref[0], w_ref[0])
        for s in range(1, s_n):
            acc += _dot_nt(dh_ref[s], w_ref[s])
        o_ref[...] = acc + ALPHA * drn_ref[...]

    tile = pl.BlockSpec((tm, k), lambda i: (i, 0))
    return pl.pallas_call(
        body, out_shape=jax.ShapeDtypeStruct((t, k), F32), grid=(t // tm,),
        in_specs=[pl.BlockSpec((s_n, tm, n), lambda i: (0, i, 0)), _resident((s_n, k, n)), tile],
        out_specs=tile, name=name, compiler_params=_params(1))(dh, w, drn)


def _mm_tn(lhs, rhs, name, tm=2048):
    sl, t, kl = lhs.shape
    sr, _, n = rhs.shape
    s_n = max(sl, sr)
    tm = min(tm, t)

    def body(l_ref, r_ref, o_ref):
        _accumulate(o_ref, _dot_tn(l_ref[...], r_ref[...]), pl.program_id(1) == 0)

    return pl.pallas_call(
        body, out_shape=jax.ShapeDtypeStruct((s_n, kl, n), F32), grid=(s_n, t // tm),
        in_specs=[pl.BlockSpec((None, tm, kl), (lambda s, i: (s, i, 0)) if sl > 1 else (lambda s, i: (0, i, 0))),
                  pl.BlockSpec((None, tm, n), (lambda s, i: (s, i, 0)) if sr > 1 else (lambda s, i: (0, i, 0)))],
        out_specs=pl.BlockSpec((None, kl, n), lambda s, i: (s, 0, 0)),
        name=name, compiler_params=_params(2))(lhs, rhs)


def _glu(h):
    return _cat_lanes(h, 0, 4).astype(F32) * _sigmoid(_cat_lanes(h, 4, 8).astype(F32))


def _shifted_windows(src_ref, r0, c0, row_block, offsets):
    span = row_block + CONV_HALO
    big = src_ref[pl.ds(r0, span), pl.ds(c0, 128)]
    for sub in range(8):
        taps = [k for k, o in enumerate(offsets) if o % 8 == sub]
        if not taps:
            continue
        rolled = big if sub == 0 else pltpu.roll(big, span - sub, 0)
        for k in taps:
            lo = offsets[k] - sub
            yield k, rolled[lo:lo + row_block]


def _tap_loop(src_ref, dst_ref, weight_ref, rows, offsets, weight_rows, row_block=64):
    d = dst_ref.shape[-1]

    def block(cb, carry):
        c0 = pl.multiple_of(cb * 128, 128)
        for r0 in range(0, rows, row_block):
            acc = jnp.zeros((row_block, 128), F32)
            for k, window in _shifted_windows(src_ref, r0, c0, row_block, offsets):
                acc += weight_ref[pl.ds(weight_rows[k], 1), pl.ds(c0, 128)] * window
            dst_ref[pl.ds(r0, row_block), pl.ds(c0, 128)] = acc
        return carry

    lax.fori_loop(0, d // 128, block, 0)


def _conv_fwd(h1, dw, dwb, g, b, name, tm=256):
    _, t, _ = h1.shape
    d = dw.shape[-1]
    tm = min(tm, t)
    hb = tm // CONV_HALO

    def body(h_ref, halo_ref, dw_ref, dwb_ref, g_ref, b_ref, s_ref, q_ref, ext_ref):
        i = pl.program_id(0)
        ext_ref[pl.ds(0, CONV_HALO), :] = _glu(halo_ref[...]) * (i > 0).astype(F32)
        ext_ref[pl.ds(CONV_HALO, tm), :] = _glu(h_ref[...])
        _tap_loop(ext_ref, q_ref, dw_ref, tm, [2 + k for k in range(CONV_W)], list(range(CONV_W)))
        q = q_ref[...] + dwb_ref[...]
        q_ref[...] = q
        qhat, _ = _ln_stats(q)
        z = qhat * g_ref[...] + b_ref[...]
        s_ref[...] = (z * _sigmoid(z)).astype(BF16)

    row = pl.BlockSpec((1, d), lambda i: (0, 0))
    tile = pl.BlockSpec((tm, d), lambda i: (i, 0))
    return pl.pallas_call(
        body, out_shape=(jax.ShapeDtypeStruct((t, d), BF16), jax.ShapeDtypeStruct((t, d), F32)), grid=(t // tm,),
        in_specs=[pl.BlockSpec((8, tm, 256), lambda i: (0, i, 0)),
                  pl.BlockSpec((8, CONV_HALO, 256), lambda i: (0, jnp.maximum(i * hb - 1, 0), 0)),
                  pl.BlockSpec((CONV_W, d), lambda i: (0, 0)), row, row, row],
        out_specs=(tile, tile), scratch_shapes=[pltpu.VMEM((tm + CONV_HALO, d), F32)],
        name=name, compiler_params=_params(1))(h1, h1, dw, dwb, g, b)


def _conv_bwd(ds, q, h1, dw, g, b, name, tm=256):
    _, t, _ = h1.shape
    d = dw.shape[-1]
    tm = min(tm, t)
    hb = tm // CONV_HALO
    n_t = t // tm
    last_halo = t // CONV_HALO - 1

    def body(ds_ref, dsn_ref, q_ref, qn_ref, h_ref, hp_ref, dw_ref, g_ref, b_ref,
             dh_ref, ddw_ref, ddwb_ref, dg_ref, db_ref, dq_ref, p_ref, dp_ref):
        i = pl.program_id(0)
        first = i == 0
        valid = (i < n_t - 1).astype(F32)

        def dq_rows(ds_rows, q_rows, scale):
            qhat, rstd = _ln_stats(q_rows)
            z = qhat * g_ref[...] + b_ref[...]
            sg = _sigmoid(z)
            dz = ds_rows.astype(F32) * (sg * (1.0 + z * (1.0 - sg))) * scale
            dq, dg, db = _ln_bwd(dz, qhat, rstd, g_ref[...])
            return dq, dg, db

        dq, dg, db = dq_rows(ds_ref[...], q_ref[...], 1.0)
        dq_ref[pl.ds(0, tm), :] = dq
        dq_ref[pl.ds(tm, CONV_HALO), :] = dq_rows(dsn_ref[...], qn_ref[...], valid)[0]
        _accumulate(dg_ref, dg, first)
        _accumulate(db_ref, db, first)
        _accumulate(ddwb_ref, jnp.sum(dq, axis=0, keepdims=True), first)

        p_ref[pl.ds(0, CONV_HALO), :] = _glu(hp_ref[...]) * (i > 0).astype(F32)
        p_ref[pl.ds(CONV_HALO, tm), :] = _glu(h_ref[...])

        _tap_loop(dq_ref, dp_ref, dw_ref, tm, list(range(CONV_W)), [CONV_W - 1 - o for o in range(CONV_W)])

        @pl.when(first)
        def _():
            ddw_ref[...] = jnp.zeros_like(ddw_ref)

        row_block = 64

        def block(cb, carry):
            c0 = pl.multiple_of(cb * 128, 128)
            for r0 in range(0, tm, row_block):
                dqb = dq_ref[pl.ds(r0, row_block), pl.ds(c0, 128)]
                for k, window in _shifted_windows(p_ref, r0, c0, row_block, [2 + k for k in range(CONV_W)]):
                    prod = dqb * window
                    ddw_ref[k, :, pl.ds(c0, 128)] += jnp.sum(prod.reshape(row_block // 8, 8, 128), axis=0)
            return carry

        lax.fori_loop(0, d // 128, block, 0)

        h = h_ref[...]
        a = _cat_lanes(h, 0, 4).astype(F32)
        sg = _sigmoid(_cat_lanes(h, 4, 8).astype(F32))
        dp = dp_ref[...]
        da = (dp * sg).astype(BF16)
        dgate = (dp * a * sg * (1.0 - sg)).astype(BF16)
        for s in range(4):
            dh_ref[s] = da[:, s * 256:(s + 1) * 256]
            dh_ref[4 + s] = dgate[:, s * 256:(s + 1) * 256]

    row = pl.BlockSpec((1, d), lambda i: (0, 0))
    tile = pl.BlockSpec((tm, d), lambda i: (i, 0))
    nxt = pl.BlockSpec((CONV_HALO, d), lambda i: (jnp.minimum((i + 1) * hb, last_halo), 0))
    return pl.pallas_call(
        body,
        out_shape=(jax.ShapeDtypeStruct((8, t, 256), BF16), jax.ShapeDtypeStruct((CONV_W, 8, d), F32),
                   jax.ShapeDtypeStruct((1, d), F32), jax.ShapeDtypeStruct((1, d), F32),
                   jax.ShapeDtypeStruct((1, d), F32)),
        grid=(n_t,),
        in_specs=[tile, nxt, tile, nxt,
                  pl.BlockSpec((8, tm, 256), lambda i: (0, i, 0)),
                  pl.BlockSpec((8, CONV_HALO, 256), lambda i: (0, jnp.maximum(i * hb - 1, 0), 0)),
                  pl.BlockSpec((CONV_W, d), lambda i: (0, 0)), row, row],
        out_specs=(pl.BlockSpec((8, tm, 256), lambda i: (0, i, 0)),
                   pl.BlockSpec((CONV_W, 8, d), lambda i: (0, 0, 0)), row, row, row),
        scratch_shapes=[pltpu.VMEM((tm + CONV_HALO, d), F32), pltpu.VMEM((tm + CONV_HALO, d), F32),
                        pltpu.VMEM((tm, d), F32)],
        name=name, compiler_params=_params(1))(ds, ds, q, q, h1, h1, dw, g, b)


def _conv3(ext, dw):
    e1 = pltpu.roll(ext, 1, 0)
    e2 = pltpu.roll(ext, 2, 0)
    return dw[2:3] * ext + dw[1:2] * e1 + dw[0:1] * e2, e1, e2


def _ffn_fwd(x, w_up, fdw, w_down, res, gp, bp, g, b, name, tm=256):
    t, d = x.shape
    _, _, n = w_up.shape
    tm = min(tm, t)
    carry_rows = 8

    def body(x_ref, wu_ref, dw_ref, wd_ref, res_ref, gp_ref, bp_ref, g_ref, b_ref,
             u_ref, h_ref, a_ref, xhat_ref, y_ref, rstd_ref, carry_ref):
        @pl.when(pl.program_id(0) == 0)
        def _():
            carry_ref[...] = jnp.zeros_like(carry_ref)

        x_tile = x_ref[...]
        acc = None
        for j in range(4):
            h = []
            for s in (j, 4 + j):
                ub = _dot(x_tile, wu_ref[s]).astype(BF16)
                u_ref[s] = ub
                uf = ub.astype(F32)
                ext = jnp.concatenate([carry_ref[s], uf], axis=0)
                carry_ref[s] = uf[tm - carry_rows:]
                hb = _conv3(ext, dw_ref[s])[0][carry_rows:].astype(BF16)
                h_ref[s] = hb
                h.append(hb.astype(F32))
            a = (h[0] * _sigmoid(h[0]) * h[1]).astype(BF16)
            a_ref[j] = a
            part = _dot(a, wd_ref[j])
            acc = part if acc is None else acc + part
        r = ALPHA * (res_ref[...] * gp_ref[...] + bp_ref[...]) + acc
        xhat, rstd = _ln_stats(r)
        xhat_ref[...] = xhat
        y_ref[...] = (xhat * g_ref[...] + b_ref[...]).astype(BF16)
        rstd_ref[...] = rstd

    row = pl.BlockSpec((1, d), lambda i: (0, 0))
    tile = pl.BlockSpec((tm, d), lambda i: (i, 0))
    return pl.pallas_call(
        body,
        out_shape=(jax.ShapeDtypeStruct((8, t, n), BF16), jax.ShapeDtypeStruct((8, t, n), BF16),
                   jax.ShapeDtypeStruct((4, t, n), BF16), jax.ShapeDtypeStruct((t, d), F32),
                   jax.ShapeDtypeStruct((t, d), BF16), jax.ShapeDtypeStruct((t, 1), F32)),
        grid=(t // tm,),
        in_specs=[tile, _resident((8, d, n)), pl.BlockSpec((8, 3, n), lambda i: (0, 0, 0)), _resident((4, n, d)),
                  tile, row, row, row, row],
        out_specs=(pl.BlockSpec((8, tm, n), lambda i: (0, i, 0)), pl.BlockSpec((8, tm, n), lambda i: (0, i, 0)),
                   pl.BlockSpec((4, tm, n), lambda i: (0, i, 0)),
                   tile, tile, pl.BlockSpec((tm, 1), lambda i: (i, 0))),
        scratch_shapes=[pltpu.VMEM((8, carry_rows, n), F32)],
        name=name, compiler_params=_params(1))(x, w_up, fdw, w_down, res, gp, bp, g, b)


def _ffn_gate_bwd(h, u, da, fdw, name, tm=256):
    _, t, n = u.shape
    tm = min(tm, t)
    hb = tm // FFN_HALO
    n_t = t // tm
    last_halo = t // FFN_HALO - 1
    h4, u4 = h.reshape(2, 4, t, n), u.reshape(2, 4, t, n)
    fdw4 = fdw.reshape(2, 4, 3, n)
    rows = tm + FFN_HALO

    def body(h_ref, hn_ref, u_ref, da_ref, dan_ref, dw_ref, du_ref, ddw_ref):
        i = pl.program_id(1)
        keep_next = (i < n_t - 1).astype(F32)
        hg = jnp.concatenate([h_ref[0], hn_ref[0]], axis=0).astype(F32)
        hv = jnp.concatenate([h_ref[1], hn_ref[1]], axis=0).astype(F32)
        da_ext = jnp.concatenate([da_ref[...].astype(F32), dan_ref[...].astype(F32) * keep_next], axis=0)
        sg = _sigmoid(hg)
        silu = hg * sg
        dh = (da_ext * hv * (sg + silu * (1.0 - sg)), da_ext * silu)
        for p in range(2):
            dwp = dw_ref[p]
            d1 = pltpu.roll(dh[p], rows - 1, 0)
            d2 = pltpu.roll(dh[p], rows - 2, 0)
            du_ref[p] = (dwp[2:3] * dh[p] + dwp[1:2] * d1 + dwp[0:1] * d2)[:tm].astype(BF16)
            up = u_ref[p].astype(F32)
            part = jnp.concatenate([jnp.sum(d[:tm] * up, axis=0, keepdims=True) for d in (d2, d1, dh[p])], axis=0)
            _accumulate(ddw_ref.at[p], part, i == 0)

    tile = pl.BlockSpec((2, None, tm, n), lambda j, i: (0, j, i, 0))
    nxt = pl.BlockSpec((2, None, FFN_HALO, n), lambda j, i: (0, j, jnp.minimum((i + 1) * hb, last_halo), 0))
    du, ddw = pl.pallas_call(
        body, out_shape=(jax.ShapeDtypeStruct((2, 4, t, n), BF16), jax.ShapeDtypeStruct((2, 4, 3, n), F32)),
        grid=(4, n_t),
        in_specs=[tile, nxt, tile,
                  pl.BlockSpec((None, tm, n), lambda j, i: (j, i, 0)),
                  pl.BlockSpec((None, FFN_HALO, n), lambda j, i: (j, jnp.minimum((i + 1) * hb, last_halo), 0)),
                  pl.BlockSpec((2, None, 3, n), lambda j, i: (0, j, 0, 0))],
        out_specs=(tile, pl.BlockSpec((2, None, 3, n), lambda j, i: (0, j, 0, 0))),
        name=name, compiler_params=_params(2))(h4, h4, u4, da, da, fdw4)
    return du.reshape(8, t, n), ddw.reshape(8, 3, n)


def _tril_mask():
    r = lax.broadcasted_iota(jnp.int32, (CHUNK, CHUNK), 0)
    c = lax.broadcasted_iota(jnp.int32, (CHUNK, CHUNK), 1)
    return (r >= c).astype(F32)


def _sgu_fwd(h1, g, b, ws, bst, name, tm=256):
    _, t, _ = h1.shape
    d = g.shape[-1]
    tm = min(tm, t)

    def body(h_ref, g_ref, b_ref, ws_ref, bst_ref, m_ref):
        h = h_ref[...]
        u = _gelu(_cat_lanes(h, 0, 4).astype(F32))
        v = _gelu(_cat_lanes(h, 4, 8).astype(F32))
        vn = (_ln_stats(v)[0] * g_ref[...] + b_ref[...]).astype(BF16)
        mask = _tril_mask()
        for hh in range(HEADS):
            cols = slice(hh * CHUNK, (hh + 1) * CHUNK)
            wm = (ws_ref[hh] * mask).astype(BF16)
            bias = bst_ref[:, hh:hh + 1]
            for c in range(tm // CHUNK):
                rows = slice(c * CHUNK, (c + 1) * CHUNK)
                sblk = _dot(wm, vn[rows, cols]) + bias
                m_ref[rows, cols] = (u[rows, cols] * sblk).astype(BF16)

    row = pl.BlockSpec((1, d), lambda i: (0, 0))
    return pl.pallas_call(
        body, out_shape=jax.ShapeDtypeStruct((t, d), BF16), grid=(t // tm,),
        in_specs=[pl.BlockSpec((8, tm, 256), lambda i: (0, i, 0)), row, row,
                  pl.BlockSpec((HEADS, CHUNK, CHUNK), lambda i: (0, 0, 0)),
                  pl.BlockSpec((CHUNK, HEADS), lambda i: (0, 0))],
        out_specs=pl.BlockSpec((tm, d), lambda i: (i, 0)),
        name=name, compiler_params=_params(1))(h1, g, b, ws, bst)


def _sgu_bwd(h1, dm, g, b, ws, bst, name, tm=256):
    _, t, _ = h1.shape
    d = g.shape[-1]
    tm = min(tm, t)

    def body(h_ref, dm_ref, g_ref, b_ref, ws_ref, bst_ref, dh_ref, dg_ref, db_ref, dws_ref, dbias_ref,
             du_ref, dvn_ref):
        first = pl.program_id(0) == 0
        h = h_ref[...]
        zu = _cat_lanes(h, 0, 4).astype(F32)
        zv = _cat_lanes(h, 4, 8).astype(F32)
        u = _gelu(zu)
        vhat, rstd = _ln_stats(_gelu(zv))
        vn = (vhat * g_ref[...] + b_ref[...]).astype(BF16)
        dm = dm_ref[...].astype(F32)
        mask = _tril_mask()

        @pl.when(first)
        def _():
            dws_ref[...] = jnp.zeros_like(dws_ref)
            dbias_ref[...] = jnp.zeros_like(dbias_ref)

        for hh in range(HEADS):
            cols = slice(hh * CHUNK, (hh + 1) * CHUNK)
            wm = (ws_ref[hh] * mask).astype(BF16)
            bias = bst_ref[:, hh:hh + 1]
            for c in range(tm // CHUNK):
                rows = slice(c * CHUNK, (c + 1) * CHUNK)
                vb = vn[rows, cols]
                sblk = _dot(wm, vb) + bias
                dmb = dm[rows, cols]
                du_ref[rows, cols] = dmb * sblk
                dsb = dmb * u[rows, cols]
                dbias_ref[:, cols] += dsb
                dsb16 = dsb.astype(BF16)
                dws_ref[hh] += _dot_nt(dsb16, vb) * mask
                dvn_ref[rows, cols] = _dot_tn(wm, dsb16)

        dvn = dvn_ref[...]
        dv, dg, db = _ln_bwd(dvn, vhat, rstd, g_ref[...])
        _accumulate(dg_ref, dg, first)
        _accumulate(db_ref, db, first)
        dzu = (du_ref[...] * _gelu_grad(zu)).astype(BF16)
        dzv = (dv * _gelu_grad(zv)).astype(BF16)
        for s in range(4):
            dh_ref[s] = dzu[:, s * 256:(s + 1) * 256]
            dh_ref[4 + s] = dzv[:, s * 256:(s + 1) * 256]

    row = pl.BlockSpec((1, d), lambda i: (0, 0))
    tile = pl.BlockSpec((tm, d), lambda i: (i, 0))
    h_tile = pl.BlockSpec((8, tm, 256), lambda i: (0, i, 0))
    return pl.pallas_call(
        body,
        out_shape=(jax.ShapeDtypeStruct((8, t, 256), BF16), jax.ShapeDtypeStruct((1, d), F32),
                   jax.ShapeDtypeStruct((1, d), F32), jax.ShapeDtypeStruct((HEADS, CHUNK, CHUNK), F32),
                   jax.ShapeDtypeStruct((CHUNK, d), F32)),
        grid=(t // tm,),
        in_specs=[h_tile, tile, row, row, pl.BlockSpec((HEADS, CHUNK, CHUNK), lambda i: (0, 0, 0)),
                  pl.BlockSpec((CHUNK, HEADS), lambda i: (0, 0))],
        out_specs=(h_tile, row, row, pl.BlockSpec((HEADS, CHUNK, CHUNK), lambda i: (0, 0, 0)),
                   pl.BlockSpec((CHUNK, d), lambda i: (0, 0))),
        scratch_shapes=[pltpu.VMEM((tm, d), F32), pltpu.VMEM((tm, d), F32)],
        name=name, compiler_params=_params(1))(h1, dm, g, b, ws, bst)


def _pool_minus_self(ext, first_token, grp):
    s = ext
    for step in range(grp + 1):
        s = s + pltpu.roll(s, 1 << step, 0)
    rows = ext.shape[0] - FFN_HALO
    tok = first_token + lax.broadcasted_iota(jnp.int32, (rows, 1), 0)
    count = jnp.minimum(tok + 1, POOL_WINDOWS[grp]).astype(F32)
    return s[FFN_HALO:] / count - ext[FFN_HALO:]


def _pool_fwd(y, wgrp, scale, name, tm=256):
    t, d = y.shape
    tm = min(tm, t)
    hb = tm // FFN_HALO
    gd = d // len(POOL_WINDOWS)

    def body(y_ref, yp_ref, w_ref, sc_ref, z_ref):
        i = pl.program_id(0)
        ext = jnp.concatenate([yp_ref[...] * (i > 0).astype(F32), y_ref[...]], axis=0)
        for grp in range(len(POOL_WINDOWS)):
            cols = slice(grp * gd, (grp + 1) * gd)
            p = _pool_minus_self(ext[:, cols], i * tm, grp)
            z_ref[:, cols] = (_dot(p.astype(BF16), w_ref[grp]) * sc_ref[:, cols]).astype(BF16)

    return pl.pallas_call(
        body, out_shape=jax.ShapeDtypeStruct((t, d), BF16), grid=(t // tm,),
        in_specs=[pl.BlockSpec((tm, d), lambda i: (i, 0)),
                  pl.BlockSpec((FFN_HALO, d), lambda i: (jnp.maximum(i * hb - 1, 0), 0)),
                  pl.BlockSpec((len(POOL_WINDOWS), gd, gd), lambda i: (0, 0, 0)),
                  pl.BlockSpec((1, d), lambda i: (0, 0))],
        out_specs=pl.BlockSpec((tm, d), lambda i: (i, 0)),
        name=name, compiler_params=_params(1))(y, y, wgrp, scale)


def _pool_bwd(y, dz, wgrp, scale, name, tm=256):
    t, d = y.shape
    tm = min(tm, t)
    hb = tm // FFN_HALO
    n_t = t // tm
    last_halo = t // FFN_HALO - 1
    gd = d // len(POOL_WINDOWS)
    rows = tm + FFN_HALO

    def body(y_ref, yp_ref, dz_ref, dzn_ref, w_ref, sc_ref, dy_ref, dsc_ref, dw_ref):
        i = pl.program_id(0)
        first = i == 0
        ext = jnp.concatenate([yp_ref[...] * (i > 0).astype(F32), y_ref[...]], axis=0)
        dz_ext = jnp.concatenate([dz_ref[...].astype(F32), dzn_ref[...].astype(F32) * (i < n_t - 1).astype(F32)],
                                 axis=0)
        tok = i * tm + lax.broadcasted_iota(jnp.int32, (rows, 1), 0)
        dsc = []
        for grp in range(len(POOL_WINDOWS)):
            cols = slice(grp * gd, (grp + 1) * gd)
            p16 = _pool_minus_self(ext[:, cols], i * tm, grp).astype(BF16)
            zg = _dot(p16, w_ref[grp])
            dsc.append(jnp.sum(dz_ext[:tm, cols] * zg, axis=0, keepdims=True))
            dzg = (dz_ext[:, cols] * sc_ref[:, cols]).astype(BF16)
            _accumulate(dw_ref.at[grp], _dot_tn(p16, dzg[:tm]), first)
            dp = _dot_nt(dzg, w_ref[grp])
            s = dp / jnp.minimum(tok + 1, POOL_WINDOWS[grp]).astype(F32)
            for step in range(grp + 1):
                s = s + pltpu.roll(s, rows - (1 << step), 0)
            dy_ref[:, cols] = (s[:tm] - dp[:tm]).astype(BF16)
        _accumulate(dsc_ref, jnp.concatenate(dsc, axis=-1), first)

    tile = pl.BlockSpec((tm, d), lambda i: (i, 0))
    return pl.pallas_call(
        body,
        out_shape=(jax.ShapeDtypeStruct((t, d), BF16), jax.ShapeDtypeStruct((1, d), F32),
                   jax.ShapeDtypeStruct((len(POOL_WINDOWS), gd, gd), F32)),
        grid=(n_t,),
        in_specs=[tile, pl.BlockSpec((FFN_HALO, d), lambda i: (jnp.maximum(i * hb - 1, 0), 0)),
                  tile, pl.BlockSpec((FFN_HALO, d), lambda i: (jnp.minimum((i + 1) * hb, last_halo), 0)),
                  pl.BlockSpec((len(POOL_WINDOWS), gd, gd), lambda i: (0, 0, 0)),
                  pl.BlockSpec((1, d), lambda i: (0, 0))],
        out_specs=(tile, pl.BlockSpec((1, d), lambda i: (0, 0)),
                   pl.BlockSpec((len(POOL_WINDOWS), gd, gd), lambda i: (0, 0, 0))),
        name=name, compiler_params=_params(1))(y, y, dz, dz, wgrp, scale)


def _loss_head(xhat, rstd, g, b, target, name, tm=512):
    t, d = xhat.shape
    tm = min(tm, t)

    def body(xhat_ref, rstd_ref, g_ref, b_ref, tgt_ref, dr_ref, drb_ref, dg_ref, db_ref, sq_ref):
        first = pl.program_id(0) == 0
        xhat_t = xhat_ref[...]
        diff = xhat_t * g_ref[...] + b_ref[...] - tgt_ref[...]
        dr, dg, db = _ln_bwd(diff * (1.0 / d), xhat_t, rstd_ref[...], g_ref[...])
        dr_ref[...] = dr
        drb_ref[...] = dr.astype(BF16)
        _accumulate(dg_ref, dg, first)
        _accumulate(db_ref, db, first)
        _accumulate(sq_ref, jnp.sum(diff * diff, axis=0, keepdims=True), first)

    row = pl.BlockSpec((1, d), lambda i: (0, 0))
    tile = pl.BlockSpec((tm, d), lambda i: (i, 0))
    return pl.pallas_call(
        body,
        out_shape=(jax.ShapeDtypeStruct((t, d), F32), jax.ShapeDtypeStruct((t, d), BF16),
                   jax.ShapeDtypeStruct((1, d), F32), jax.ShapeDtypeStruct((1, d), F32),
                   jax.ShapeDtypeStruct((1, d), F32)),
        grid=(t // tm,),
        in_specs=[tile, pl.BlockSpec((tm, 1), lambda i: (i, 0)), row, row, tile],
        out_specs=(tile, tile, row, row, row),
        name=name, compiler_params=_params(1))(xhat, rstd, g, b, target)


def _my_place():
    return lax.axis_index("x"), lax.axis_index("y"), lax.axis_index("c")


def _flip(coord, bit):
    return 1 - coord if bit else coord


def _all_gather(arrays, name):
    n = len(arrays)

    def body(*refs):
        ins, outs = refs[:n], refs[n:2 * n]
        send_sems, recv_sems, local_sems = refs[2 * n:]
        x, y, c = _my_place()
        me, sibling = (x, y, c), (x, y, 1 - c)
        chips = [(1 - x, y), (x, 1 - y), (1 - x, 1 - y)]

        def copy(a, k, block, to, src=None):
            idx = 4 * block[0] + 2 * block[1] + block[2]
            return pltpu.make_async_remote_copy(
                src_ref=outs[a].at[idx] if src is None else src, dst_ref=outs[a].at[idx],
                send_sem=send_sems.at[a, k], recv_sem=recv_sems.at[a, k], device_id=to, device_id_type=MESH)

        mine, first, passed = [], [], []
        for a in range(n):
            cp = pltpu.make_async_copy(ins[a], outs[a].at[4 * x + 2 * y + c], local_sems.at[a])
            cp.start()
            mine.append(cp)
            first.append(copy(a, 0, me, sibling, src=ins[a]))
            first += [copy(a, 1 + j, me, (*chip, c), src=ins[a]) for j, chip in enumerate(chips)]
        for cp in first:
            cp.start()
        for j, chip in enumerate(chips):
            for a in range(n):
                copy(a, 1 + j, (*chip, c), me).wait_recv()
                cp = copy(a, 4 + j, (*chip, c), sibling)
                cp.start()
                passed.append(cp)
        for a in range(n):
            copy(a, 0, sibling, me).wait_recv()
            for j, chip in enumerate(chips):
                copy(a, 4 + j, (*chip, 1 - c), me).wait_recv()
        for cp in first + passed:
            cp.wait_send()
        for cp in mine:
            cp.wait()

    hbm = pl.BlockSpec(memory_space=pltpu.HBM)
    return pl.pallas_call(
        body, out_shape=tuple(jax.ShapeDtypeStruct((N_DEV,) + a.shape, a.dtype) for a in arrays),
        in_specs=[hbm] * n, out_specs=tuple([hbm] * n),
        scratch_shapes=[pltpu.SemaphoreType.DMA((n, 7)), pltpu.SemaphoreType.DMA((n, 7)),
                        pltpu.SemaphoreType.DMA((n,))],
        name=name)(*arrays)


def _peers_of(x, y, c):
    peers = [(_flip(x, k & 4), _flip(y, k & 2), _flip(c, k & 1)) for k in range(1, N_DEV)]
    return peers, [4 * p[0] + 2 * p[1] + p[2] for p in peers]


def _exchange_copies(pieces, lands, send_sems, recv_sems, local_sems, whole):
    x, y, c = _my_place()
    me = 4 * x + 2 * y + c
    peers, slots = _peers_of(x, y, c)

    def piece(p, slot):
        return p if whole else p.at[slot]

    local = [pltpu.make_async_copy(piece(p, me), z.at[me], local_sems.at[a])
             for a, (p, z) in enumerate(zip(pieces, lands))]
    remote = []
    for k, peer in enumerate(peers):
        for a, (p, z) in enumerate(zip(pieces, lands)):
            sems = dict(send_sem=send_sems.at[7 * a + k], recv_sem=recv_sems.at[7 * a + k], device_id=peer,
                        device_id_type=MESH)
            remote.append((pltpu.make_async_remote_copy(src_ref=piece(p, slots[k]), dst_ref=z.at[me], **sems),
                           pltpu.make_async_remote_copy(src_ref=piece(p, slots[k]), dst_ref=z.at[slots[k]], **sems)))
    return local, remote


def _exchange_start(pieces, name, whole=False):
    n = len(pieces)

    def body(*refs):
        ins, lands = refs[:n], refs[n:2 * n]
        send_sems, recv_sems, local_sems = refs[2 * n:2 * n + 3]
        token_ref = refs[-1]
        local, remote = _exchange_copies(ins, lands, send_sems, recv_sems, local_sems, whole)
        for cp in local:
            cp.start()
        for cp, _ in remote:
            cp.start()
        token_ref[...] = jnp.zeros_like(token_ref)

    hbm = pl.BlockSpec(memory_space=pltpu.HBM)
    sem = pl.BlockSpec(memory_space=pltpu.SEMAPHORE)
    thru = [pltpu.HBM(p.shape, p.dtype) for p in pieces]
    zones = [pltpu.HBM(((N_DEV,) + p.shape) if whole else p.shape, p.dtype) for p in pieces]
    outs = pl.pallas_call(
        body,
        out_shape=(pltpu.SemaphoreType.DMA((7 * n,)), pltpu.SemaphoreType.DMA((7 * n,)), pltpu.SemaphoreType.DMA((n,)),
                   *thru, *zones, jax.ShapeDtypeStruct((8, 128), F32)),
        in_specs=[hbm] * (2 * n), out_specs=(sem, sem, sem, *([hbm] * (2 * n)), pl.BlockSpec(memory_space=pltpu.VMEM)),
        input_output_aliases={i: 3 + i for i in range(2 * n)},
        compiler_params=pltpu.CompilerParams(has_side_effects=pltpu.SideEffectType.DATAFLOW_SIDE_EFFECTING),
        name=name,
    )(*[pltpu.with_memory_space_constraint(p, pltpu.HBM) for p in pieces],
      *[pltpu.with_memory_space_constraint(lax.empty(z.shape, z.dtype), pltpu.HBM) for z in zones])
    return outs[:-1], outs[-1]


def _exchange_wait(handles, after, name, whole=False):
    send_sems, recv_sems, local_sems = handles[:3]
    n = (len(handles) - 3) // 2
    pieces, lands = handles[3:3 + n], handles[3 + n:]

    def body(*refs):
        ins, zones = refs[:n], refs[n:2 * n]
        s_sems, r_sems, l_sems = refs[2 * n:2 * n + 3]
        local, remote = _exchange_copies(ins, zones, s_sems, r_sems, l_sems, whole)
        for cp in local:
            cp.wait()
        for cp, landed in remote:
            cp.wait_send()
            landed.wait_recv()

    hbm = pl.BlockSpec(memory_space=pltpu.HBM)
    sem = pl.BlockSpec(memory_space=pltpu.SEMAPHORE)
    thru = [pltpu.HBM(p.shape, p.dtype) for p in list(pieces) + list(lands)]
    outs = pl.pallas_call(
        body, out_shape=tuple(thru),
        in_specs=[hbm] * (2 * n) + [sem, sem, sem, pl.BlockSpec(memory_space=pl.ANY)], out_specs=tuple([hbm] * (2 * n)),
        input_output_aliases={i: i for i in range(2 * n)},
        compiler_params=pltpu.CompilerParams(has_side_effects=pltpu.SideEffectType.DATAFLOW_SIDE_EFFECTING),
        name=name,
    )(*pieces, *lands, send_sems, recv_sems, local_sems, after)
    return outs[n:]


def _adamw(pieces, w, m, v, name, max_rows=256):
    n_l = len(pieces)
    _, r, cols = pieces[0].shape
    tr = max(rows for rows in range(16, min(max_rows, r) + 1, 16) if r % rows == 0)
    n_r = r // tr
    c1 = 1.0 / (1.0 - ADAM_B1 ** ADAM_STEP)
    c2 = 1.0 / (1.0 - ADAM_B2 ** ADAM_STEP)

    def body(*refs):
        p_refs = refs[:n_l]
        w_ref, m_ref, v_ref, g_ref, d_ref, nm_ref, nv_ref = refs[n_l:]
        for layer, p_ref in enumerate(p_refs):
            @pl.when(pl.program_id(0) == layer)
            def _(p_ref=p_ref):
                g = p_ref[0].astype(F32)
                for k in range(1, N_DEV):
                    g = g + p_ref[k].astype(F32)
                nm = ADAM_B1 * m_ref[...] + (1.0 - ADAM_B1) * g
                nv = ADAM_B2 * v_ref[...] + (1.0 - ADAM_B2) * (g * g)
                g_ref[...] = g
                nm_ref[...] = nm
                nv_ref[...] = nv
                d_ref[...] = -ADAM_LR * ((nm * c1) / (jnp.sqrt(nv * c2) + ADAM_EPS) + ADAM_WD * w_ref[...])

    def piece_spec(layer):
        return pl.BlockSpec((N_DEV, tr, cols), lambda l, i: (0, jnp.where(l == layer, i, 0), 0))

    tile = pl.BlockSpec((tr, cols), lambda l, i: (l * n_r + i, 0))
    out = jax.ShapeDtypeStruct((n_l * r, cols), F32)
    return pl.pallas_call(
        body, out_shape=(out, out, out, out), grid=(n_l, n_r),
        in_specs=[piece_spec(layer) for layer in range(n_l)] + [tile, tile, tile],
        out_specs=(tile, tile, tile, tile), name=name, compiler_params=_params(2))(*pieces, w, m, v)


def _rows_of(numel, row_tile):
    rows = -(-numel // LANES)
    return -(-rows // row_tile) * row_tile


def _pack(flat_list, row_tile, lead=()):
    parts = []
    for a in flat_list:
        numel = a.shape[-1]
        rows = _rows_of(numel, row_tile)
        pad = [(0, 0)] * len(lead) + [(0, rows * LANES - numel)]
        parts.append(jnp.pad(a, pad).reshape(*lead, rows, LANES))
    return jnp.concatenate(parts, axis=len(lead))


def _unpack(buf, shapes, row_tile, lead=()):
    out, r0 = [], 0
    for shape in shapes:
        numel = 1
        for s in shape:
            numel *= s
        rows = _rows_of(numel, row_tile)
        part = lax.slice_in_dim(buf, r0, r0 + rows, axis=len(lead)).reshape(*lead, rows * LANES)
        out.append(lax.slice_in_dim(part, 0, numel, axis=len(lead)).reshape(*lead, *shape))
        r0 += rows
    return out


def _to_shards(full, axis):
    shape = full.shape
    cut = full.reshape(shape[:axis] + (N_DEV, shape[axis] // N_DEV) + shape[axis + 1:])
    return jnp.moveaxis(cut, axis, 0)


def _step(x, target, w, m, v):
    t, d = x.shape[1], x.shape[2]
    x2 = x.reshape(t, d)
    tgt2 = target.reshape(t, d)

    first_layer = ("a_w_in", "a_w_out", "f_w_up", "f_w_down")
    small = _pack([w[k].reshape(-1) for k in GATHER_F32], 8)
    gathered = _all_gather([w[k][:1].astype(BF16) for k in first_layer] + [small], "all_gather_first_layer")
    gw = dict(zip(GATHER_F32, _unpack(gathered[-1], [w[k].shape for k in GATHER_F32], 8, (N_DEV,))))
    first_mats = {k: a.reshape((N_DEV,) + a.shape[2:]) for k, a in zip(first_layer, gathered[:-1])}
    later = [(w[k][1:] if k in first_layer else w[k]).astype(BF16) for k in GATHER_BF16]
    later_handles, later_started = _exchange_start(later, "all_gather_later_start", whole=True)
    later_mats = {}

    def mat(name, layer):
        if name in first_layer:
            return first_mats[name] if layer == 0 else later_mats[name][:, layer - 1]
        return later_mats[name][:, layer]

    def full_cols(name, layer):
        a = gw[name][:, layer]
        if a.ndim == 2:
            return a.reshape(1, -1)
        return jnp.moveaxis(a, 0, 1).reshape(a.shape[1], -1)

    ones = jnp.ones((1, d), F32)
    zeros = jnp.zeros((1, d), F32)

    saved = []
    res, res_g, res_b = x2, ones, zeros
    xin = x2.astype(BF16)
    for i in range(DEPTH):
        kind, j = i % 3, i // 3
        sv = {"xin": xin, "kind": kind, "j": j}
        if i == 1:
            later_mats.update(zip(GATHER_BF16, _exchange_wait(later_handles, xin, "all_gather_later_wait", whole=True)))
        if kind == 0:
            w_in = mat("a_w_in", j)
            w_out = mat("a_w_out", j).reshape(1, d, d)
            dw, dwb = full_cols("a_dw", j), full_cols("a_dw_b", j)
            if i == 0:
                dwb = dwb + later_started[0:1, 0:1]
            lg, lb = full_cols("a_ln_g", j), full_cols("a_ln_b", j)
            h1 = _mm_nn(xin, w_in, BF16, f"conv_in_{i}")
            s_act, q = _conv_fwd(h1, dw, dwb, lg, lb, f"conv_mix_{i}")
            sv.update(h1=h1, q=q, w_in=w_in, dw=dw, lg=lg, lb=lb)
        elif kind == 1:
            w_in = mat("b_w_in", j)
            w_out = mat("b_w_out", j).reshape(1, d, d)
            lg, lb = w["b_ln_g"][j].reshape(1, d), w["b_ln_b"][j].reshape(1, d)
            ws, bst = w["b_ws"][j], w["b_bs"][j].T
            h1 = _mm_nn(xin, w_in, BF16, f"sgu_in_{i}")
            s_act = _sgu_fwd(h1, lg, lb, ws, bst, f"sgu_mix_{i}")
            sv.update(h1=h1, w_in=w_in, lg=lg, lb=lb, ws=ws, bst=bst)
        else:
            w_in = mat("c_w_in", j).reshape(1, d, d)
            w_out = mat("c_w_out", j).reshape(1, d, d)
            wgrp = jnp.moveaxis(mat("c_w_grp", j), 0, 1).reshape(4, d // 4, d // 4)
            scale = full_cols("c_scale", j)
            yp = _mm_nn(xin, w_in, F32, f"pool_in_{i}")[0]
            s_act = _pool_fwd(yp, wgrp, scale, f"pool_mix_{i}")
            sv.update(yp=yp, w_in=w_in, wgrp=wgrp, scale=scale)
        g1, b1 = w["ln1_g"][i].reshape(1, d), w["ln1_b"][i].reshape(1, d)
        xhat1, y1, rstd1 = _mm_res_ln(s_act.reshape(1, t, d), w_out, res, res_g, res_b, g1, b1, f"mix_out_ln_{i}")
        w_up = mat("f_w_up", i)
        fdw = gw["f_dw"][:, i]
        n_ff = w_up.shape[-1]
        w_down = mat("f_w_down", i).reshape(4, n_ff, d)
        g2, b2 = w["ln2_g"][i].reshape(1, d), w["ln2_b"][i].reshape(1, d)
        u, h_ffn, a_act, xhat2, y2, rstd2 = _ffn_fwd(y1, w_up, fdw, w_down, xhat1, g1, b1, g2, b2, f"ffn_fwd_{i}")
        sv.update(s_act=s_act, w_out=w_out, xhat1=xhat1, y1=y1, rstd1=rstd1, g1=g1, u=u, h_ffn=h_ffn, a_act=a_act,
                  w_up=w_up,
                  fdw=fdw, w_down=w_down, xhat2=xhat2, rstd2=rstd2, g2=g2, b2=b2)
        saved.append(sv)
        res, res_g, res_b, xin = xhat2, g2, b2, y2

    last = saved[-1]
    dr2, dr2b, dg2, db2, sq = _loss_head(last["xhat2"], last["rstd2"], last["g2"], last["b2"], tgt2, "loss_head")
    loss = lax.psum((0.5 / d) * jnp.sum(sq), ("x", "y", "c"))

    grads = {k: [None] * w[k].shape[0] for k in WEIGHTS}
    grad_x = None
    small_names = [k for k in WEIGHTS if k not in GATHER_BF16]
    exchanges = []

    def start_exchange(keys, tag, extra=()):
        pieces = [grads[k][l].astype(BF16).reshape(N_DEV, -1, w[k].shape[-1]) for k, l in keys]
        handles, token = _exchange_start(pieces + list(extra), f"exchange_start_{tag}")
        exchanges.append((keys, handles, tag))
        return token[0:1, 0:1]

    for i in reversed(range(DEPTH)):
        sv = saved[i]
        kind, j = sv["kind"], sv["j"]
        grads["ln2_g"][i], grads["ln2_b"][i] = dg2, db2
        da = _mm_nt_out(dr2b, sv["w_down"], f"ffn_da_{i}")
        grads["f_w_down"][i] = _to_shards(_mm_tn(sv["a_act"], dr2b.reshape(1, t, d), f"ffn_dwdown_{i}")
                                          .reshape(-1, d), 0)
        du, dfdw = _ffn_gate_bwd(sv["h_ffn"], sv["u"], da, sv["fdw"], f"ffn_gate_bwd_{i}")
        grads["f_dw"][i] = dfdw
        grads["f_w_up"][i] = _mm_tn(sv["y1"].reshape(1, t, d), du, f"ffn_dwup_{i}")
        started = start_exchange([("f_w_up", i), ("f_w_down", i)], f"ffn_{i}")
        dr1, dr1b, dg1, db1 = _mm_nt_lnb(du, sv["w_up"], dr2, sv["xhat1"], sv["rstd1"], sv["g1"], f"ffn_dx_ln_{i}")
        grads["ln1_g"][i], grads["ln1_b"][i] = dg1, db1
        ds = _mm_nt_out(dr1b, sv["w_out"], f"mix_ds_{i}")[0]
        dw_out = _to_shards(_mm_tn(sv["s_act"].reshape(1, t, d), dr1b.reshape(1, t, d), f"mix_dwout_{i}")[0], 0)
        xin3 = sv["xin"].reshape(1, t, d)
        if kind == 0:
            dh1, ddw, ddwb, dlg, dlb = _conv_bwd(ds, sv["q"], sv["h1"], sv["dw"], sv["lg"] + started, sv["lb"],
                                                  f"conv_mix_bwd_{i}")
            grads["a_w_out"][j] = dw_out
            grads["a_dw"][j] = _to_shards(jnp.sum(ddw, axis=1), 1)
            grads["a_dw_b"][j] = _to_shards(ddwb[0], 0)
            grads["a_ln_g"][j] = _to_shards(dlg[0], 0)
            grads["a_ln_b"][j] = _to_shards(dlb[0], 0)
            grads["a_w_in"][j] = _mm_tn(xin3, dh1, f"conv_dwin_{i}")
            dh_in, w_in = dh1, sv["w_in"]
            mixer_keys = [("a_w_in", j), ("a_w_out", j)]
        elif kind == 1:
            dh1, dlg, dlb, dws, dbias = _sgu_bwd(sv["h1"], ds, sv["lg"] + started, sv["lb"], sv["ws"], sv["bst"],
                                                  f"sgu_mix_bwd_{i}")
            grads["b_w_out"][j] = dw_out
            grads["b_ln_g"][j], grads["b_ln_b"][j] = dlg[0], dlb[0]
            grads["b_ws"][j] = dws
            grads["b_bs"][j] = jnp.sum(dbias.reshape(CHUNK, HEADS, CHUNK), axis=-1).T
            grads["b_w_in"][j] = _mm_tn(xin3, dh1, f"sgu_dwin_{i}")
            dh_in, w_in = dh1, sv["w_in"]
            mixer_keys = [("b_w_in", j), ("b_w_out", j)]
        else:
            dyp, dscale, dwgrp = _pool_bwd(sv["yp"], ds, sv["wgrp"], sv["scale"] + started, f"pool_mix_bwd_{i}")
            grads["c_w_out"][j] = dw_out
            grads["c_scale"][j] = _to_shards(dscale[0], 0)
            grads["c_w_grp"][j] = _to_shards(dwgrp, 1)
            dh_in, w_in = dyp.reshape(1, t, d), sv["w_in"]
            grads["c_w_in"][j] = _to_shards(_mm_tn(xin3, dh_in, f"pool_dwin_{i}")[0], 0)
            mixer_keys = [("c_w_in", j), ("c_w_grp", j), ("c_w_out", j)]
        if i > 0:
            started = start_exchange(mixer_keys, f"mixer_{i}")
            prev = saved[i - 1]
            prev["fdw"] = prev["fdw"] + started
            dr2, dr2b, dg2, db2 = _mm_nt_lnb(dh_in, w_in, dr1, prev["xhat2"], prev["rstd2"], prev["g2"],
                                              f"mix_dx_ln_{i}")
        else:
            flat = []
            for k in small_names:
                if k in REPLICATED:
                    full = jnp.stack([gk.reshape(w[k].shape[1:]) for gk in grads[k]], axis=0)
                    flat.append(jnp.broadcast_to(full.reshape(1, -1), (N_DEV, full.size)))
                else:
                    flat.append(jnp.stack(grads[k], axis=1).reshape(N_DEV, -1))
            small_pieces = _pack(flat, 8, (N_DEV,))
            pad_rows = -(-small_pieces.shape[1] // 128) * 128 - small_pieces.shape[1]
            small_pieces = jnp.pad(small_pieces, ((0, 0), (0, pad_rows), (0, 0)))
            start_exchange(mixer_keys, f"mixer_{i}", extra=[small_pieces])
            grad_x = _mm_nt_res(dh_in, w_in, dr1, "mix_dx_0").reshape(x.shape)

    received, small_received = {}, None
    for keys, handles, tag in exchanges:
        lands = _exchange_wait(handles, grad_x, f"exchange_wait_{tag}")
        received.update(zip(keys, lands))
        if len(lands) > len(keys):
            small_received = lands[-1]

    kinds = ("grad", "delta", "new_m", "new_v")
    result = {}
    for k in GATHER_BF16:
        cols = w[k].shape[-1]
        bufs = _adamw([received[(k, l)] for l in range(w[k].shape[0])], w[k].reshape(-1, cols),
                      m[k].reshape(-1, cols), v[k].reshape(-1, cols), f"adamw_{k}")
        result.update({(kind, k): buf.reshape(w[k].shape) for kind, buf in zip(kinds, bufs)})

    def packed(tree):
        return jnp.pad(_pack([tree[k].reshape(-1) for k in small_names], 8), ((0, pad_rows), (0, 0)))

    bufs = _adamw([small_received], packed(w), packed(m), packed(v), "adamw_small")
    shapes = [w[k].shape for k in small_names]
    for kind, buf in zip(kinds, bufs):
        result.update({(kind, k): a for k, a in zip(small_names, _unpack(buf, shapes, 8))})
    outs = [result[(kind, k)] for kind in kinds for k in WEIGHTS]
    return (loss, grad_x, *outs)


def kernel(x, a_w_in, a_dw, a_dw_b, a_ln_g, a_ln_b, a_w_out, b_w_in, b_ln_g, b_ln_b, b_ws, b_bs, b_w_out, c_w_in, c_w_grp, c_scale, c_w_out, f_w_up, f_dw, f_w_down, ln1_g, ln1_b, ln2_g, ln2_b, loss_target, m_a_w_in, m_a_dw, m_a_dw_b, m_a_ln_g, m_a_ln_b, m_a_w_out, m_b_w_in, m_b_ln_g, m_b_ln_b, m_b_ws, m_b_bs, m_b_w_out, m_c_w_in, m_c_w_grp, m_c_scale, m_c_w_out, m_f_w_up, m_f_dw, m_f_w_down, m_ln1_g, m_ln1_b, m_ln2_g, m_ln2_b, v_a_w_in, v_a_dw, v_a_dw_b, v_a_ln_g, v_a_ln_b, v_a_w_out, v_b_w_in, v_b_ln_g, v_b_ln_b, v_b_ws, v_b_bs, v_b_w_out, v_c_w_in, v_c_w_grp, v_c_scale, v_c_w_out, v_f_w_up, v_f_dw, v_f_w_down, v_ln1_g, v_ln1_b, v_ln2_g, v_ln2_b):
    w = dict(zip(WEIGHTS, (a_w_in, a_dw, a_dw_b, a_ln_g, a_ln_b, a_w_out, b_w_in, b_ln_g, b_ln_b, b_ws, b_bs, b_w_out,
                           c_w_in, c_w_grp, c_scale, c_w_out, f_w_up, f_dw, f_w_down, ln1_g, ln1_b, ln2_g, ln2_b)))
    m = dict(zip(WEIGHTS, (m_a_w_in, m_a_dw, m_a_dw_b, m_a_ln_g, m_a_ln_b, m_a_w_out, m_b_w_in, m_b_ln_g, m_b_ln_b,
                           m_b_ws, m_b_bs, m_b_w_out, m_c_w_in, m_c_w_grp, m_c_scale, m_c_w_out, m_f_w_up, m_f_dw,
                           m_f_w_down, m_ln1_g, m_ln1_b, m_ln2_g, m_ln2_b)))
    v = dict(zip(WEIGHTS, (v_a_w_in, v_a_dw, v_a_dw_b, v_a_ln_g, v_a_ln_b, v_a_w_out, v_b_w_in, v_b_ln_g, v_b_ln_b,
                           v_b_ws, v_b_bs, v_b_w_out, v_c_w_in, v_c_w_grp, v_c_scale, v_c_w_out, v_f_w_up, v_f_dw,
                           v_f_w_down, v_ln1_g, v_ln1_b, v_ln2_g, v_ln2_b)))
    return _step(x, loss_target, w, m, v)
```

```python
import functools

import jax
import jax.numpy as jnp
from jax import lax
from jax.experimental import pallas as pl
from jax.experimental.pallas import tpu as pltpu

N_DEV = 8
DEPTH = 4
ALPHA = float((2 * DEPTH) ** 0.25)
LN_EPS = 1e-5
CONV_W = 31
CONV_HALO = 32
FFN_HALO = 16
POOL_WINDOWS = (2, 4, 8, 16)
CHUNK = 128
HEADS = 8
LANES = 1024
ADAM_LR, ADAM_B1, ADAM_B2, ADAM_EPS, ADAM_WD, ADAM_STEP = 0.001, 0.9, 0.999, 1e-08, 0.01, 10
VMEM_LIMIT = 56 * 1024 * 1024
F32, BF16 = jnp.float32, jnp.bfloat16
MESH = pl.DeviceIdType.MESH

WEIGHTS = ['a_w_in', 'a_dw', 'a_dw_b', 'a_ln_g', 'a_ln_b', 'a_w_out', 'b_w_in', 'b_ln_g', 'b_ln_b', 'b_ws', 'b_bs',
           'b_w_out', 'c_w_in', 'c_w_grp', 'c_scale', 'c_w_out', 'f_w_up', 'f_dw', 'f_w_down', 'ln1_g', 'ln1_b',
           'ln2_g', 'ln2_b']
REPLICATED = ('b_ln_g', 'b_ln_b', 'b_ws', 'b_bs', 'ln1_g', 'ln1_b', 'ln2_g', 'ln2_b')
GATHER_BF16 = ('a_w_in', 'a_w_out', 'b_w_in', 'b_w_out', 'c_w_in', 'c_w_grp', 'c_w_out', 'f_w_up', 'f_w_down')
GATHER_F32 = ('a_dw', 'a_dw_b', 'a_ln_g', 'a_ln_b', 'c_scale', 'f_dw')


def _params(n_axes):
    return pltpu.CompilerParams(dimension_semantics=("arbitrary",) * n_axes, vmem_limit_bytes=VMEM_LIMIT)


def _resident(shape):
    zeros = (0,) * len(shape)
    return pl.BlockSpec(shape, lambda i: zeros, pipeline_mode=pl.Buffered(1))


def _sigmoid(x):
    return 1.0 / (1.0 + jnp.exp(-x))


def _gelu(x):
    return 0.5 * x * (1.0 + lax.erf(x * 0.7071067811865476))


def _gelu_grad(x):
    return 0.5 * (1.0 + lax.erf(x * 0.7071067811865476)) + x * jnp.exp(-0.5 * x * x) * 0.3989422804014327


def _ln_stats(r):
    mu = jnp.mean(r, axis=-1, keepdims=True)
    xc = r - mu
    var = jnp.mean(xc * xc, axis=-1, keepdims=True)
    rstd = lax.rsqrt(var + LN_EPS)
    return xc * rstd, rstd


def _ln_bwd(dy, xhat, rstd, g):
    dxhat = dy * g
    m1 = jnp.mean(dxhat, axis=-1, keepdims=True)
    m2 = jnp.mean(dxhat * xhat, axis=-1, keepdims=True)
    dr = rstd * (dxhat - m1 - xhat * m2)
    return dr, jnp.sum(dy * xhat, axis=0, keepdims=True), jnp.sum(dy, axis=0, keepdims=True)


def _accumulate(ref, value, first):
    @pl.when(first)
    def _():
        ref[...] = value

    @pl.when(jnp.logical_not(first))
    def _():
        ref[...] += value


def _dot(a, b):
    return jnp.dot(a, b, preferred_element_type=F32)


def _dot_nt(a, b):
    return lax.dot_general(a, b, (((1,), (1,)), ((), ())), preferred_element_type=F32)


def _dot_tn(a, b):
    return lax.dot_general(a, b, (((0,), (0,)), ((), ())), preferred_element_type=F32)


def _cat_lanes(h, lo, hi):
    return jnp.concatenate([h[s] for s in range(lo, hi)], axis=-1)


def _mm_nn(x, w, out_dtype, name, tm=512):
    t, k = x.shape
    s_n, _, n = w.shape
    tm = min(tm, t)

    def body(x_ref, w_ref, o_ref):
        x_tile = x_ref[...]
        for s in range(s_n):
            o_ref[s] = _dot(x_tile, w_ref[s]).astype(o_ref.dtype)

    return pl.pallas_call(
        body, out_shape=jax.ShapeDtypeStruct((s_n, t, n), out_dtype), grid=(t // tm,),
        in_specs=[pl.BlockSpec((tm, k), lambda i: (i, 0)), _resident((s_n, k, n))],
        out_specs=pl.BlockSpec((s_n, tm, n), lambda i: (0, i, 0)),
        name=name, compiler_params=_params(1))(x, w)


def _mm_res_ln(a, w, res, gp, bp, g, b, name, tm=512):
    s_n, t, ka = a.shape
    d = w.shape[-1]
    tm = min(tm, t)

    def body(a_ref, w_ref, res_ref, gp_ref, bp_ref, g_ref, b_ref, xhat_ref, y_ref, rstd_ref):
        acc = _dot(a_ref[0], w_ref[0])
        for s in range(1, s_n):
            acc += _dot(a_ref[s], w_ref[s])
        r = ALPHA * (res_ref[...] * gp_ref[...] + bp_ref[...]) + acc
        xhat, rstd = _ln_stats(r)
        xhat_ref[...] = xhat
        y_ref[...] = (xhat * g_ref[...] + b_ref[...]).astype(BF16)
        rstd_ref[...] = rstd

    row = pl.BlockSpec((1, d), lambda i: (0, 0))
    tile = pl.BlockSpec((tm, d), lambda i: (i, 0))
    return pl.pallas_call(
        body,
        out_shape=(jax.ShapeDtypeStruct((t, d), F32), jax.ShapeDtypeStruct((t, d), BF16),
                   jax.ShapeDtypeStruct((t, 1), F32)),
        grid=(t // tm,),
        in_specs=[pl.BlockSpec((s_n, tm, ka), lambda i: (0, i, 0)), _resident((s_n, ka, d)),
                  tile, row, row, row, row],
        out_specs=(tile, tile, pl.BlockSpec((tm, 1), lambda i: (i, 0))),
        name=name, compiler_params=_params(1))(a, w, res, gp, bp, g, b)


def _mm_nt_out(x, w, name, tm=512):
    t, n = x.shape
    s_n, k, _ = w.shape
    tm = min(tm, t)

    def body(x_ref, w_ref, o_ref):
        x_tile = x_ref[...]
        for s in range(s_n):
            o_ref[s] = _dot_nt(x_tile, w_ref[s]).astype(o_ref.dtype)

    return pl.pallas_call(
        body, out_shape=jax.ShapeDtypeStruct((s_n, t, k), BF16), grid=(t // tm,),
        in_specs=[pl.BlockSpec((tm, n), lambda i: (i, 0)), _resident((s_n, k, n))],
        out_specs=pl.BlockSpec((s_n, tm, k), lambda i: (0, i, 0)),
        name=name, compiler_params=_params(1))(x, w)


def _mm_nt_lnb(dh, w, drn, xhat, rstd, g, name, tm=512):
    s_n, t, n = dh.shape
    k = w.shape[1]
    tm = min(tm, t)

    def body(dh_ref, w_ref, drn_ref, xhat_ref, rstd_ref, g_ref, dr_ref, drb_ref, dg_ref, db_ref):
        first = pl.program_id(0) == 0
        acc = _dot_nt(dh_ref[0], w_ref[0])
        for s in range(1, s_n):
            acc += _dot_nt(dh_ref[s], w_ref[s])
        dy = acc + ALPHA * drn_ref[...]
        dr, dg, db = _ln_bwd(dy, xhat_ref[...], rstd_ref[...], g_ref[...])
        dr_ref[...] = dr
        drb_ref[...] = dr.astype(BF16)
        _accumulate(dg_ref, dg, first)
        _accumulate(db_ref, db, first)

    tile = pl.BlockSpec((tm, k), lambda i: (i, 0))
    row = pl.BlockSpec((1, k), lambda i: (0, 0))
    return pl.pallas_call(
        body,
        out_shape=(jax.ShapeDtypeStruct((t, k), F32), jax.ShapeDtypeStruct((t, k), BF16),
                   jax.ShapeDtypeStruct((1, k), F32), jax.ShapeDtypeStruct((1, k), F32)),
        grid=(t // tm,),
        in_specs=[pl.BlockSpec((s_n, tm, n), lambda i: (0, i, 0)), _resident((s_n, k, n)),
                  tile, tile, pl.BlockSpec((tm, 1), lambda i: (i, 0)), row],
        out_specs=(tile, tile, row, row),
        name=name, compiler_params=_params(1))(dh, w, drn, xhat, rstd, g)


def _mm_nt_res(dh, w, drn, name, tm=512):
    s_n, t, n = dh.shape
    k = w.shape[1]
    tm = min(tm, t)

    def body(dh_ref, w_ref, drn_ref, o_ref):
        acc = _dot_nt(dh_ref[0], w_ref[0])
        for s in range(1, s_n):
            acc += _dot_nt(dh_ref[s], w_ref[s])
        o_ref[...] = acc + ALPHA * drn_ref[...]

    tile = pl.BlockSpec((tm, k), lambda i: (i, 0))
    return pl.pallas_call(
        body, out_shape=jax.ShapeDtypeStruct((t, k), F32), grid=(t // tm,),
        in_specs=[pl.BlockSpec((s_n, tm, n), lambda i: (0, i, 0)), _resident((s_n, k, n)), tile],
        out_specs=tile, name=name, compiler_params=_params(1))(dh, w, drn)


def _mm_tn(lhs, rhs, name, tm=2048):
    sl, t, kl = lhs.shape
    sr, _, n = rhs.shape
    s_n = max(sl, sr)
    tm = min(tm, t)

    def body(l_ref, r_ref, o_ref):
        _accumulate(o_ref, _dot_tn(l_ref[...], r_ref[...]), pl.program_id(1) == 0)

    return pl.pallas_call(
        body, out_shape=jax.ShapeDtypeStruct((s_n, kl, n), F32), grid=(s_n, t // tm),
        in_specs=[pl.BlockSpec((None, tm, kl), (lambda s, i: (s, i, 0)) if sl > 1 else (lambda s, i: (0, i, 0))),
                  pl.BlockSpec((None, tm, n), (lambda s, i: (s, i, 0)) if sr > 1 else (lambda s, i: (0, i, 0)))],
        out_specs=pl.BlockSpec((None, kl, n), lambda s, i: (s, 0, 0)),
        name=name, compiler_params=_params(2))(lhs, rhs)


def _glu(h):
    return _cat_lanes(h, 0, 4).astype(F32) * _sigmoid(_cat_lanes(h, 4, 8).astype(F32))


def _shifted_windows(src_ref, r0, c0, row_block, offsets):
    span = row_block + CONV_HALO
    big = src_ref[pl.ds(r0, span), pl.ds(c0, 128)]
    for sub in range(8):
        taps = [k for k, o in enumerate(offsets) if o % 8 == sub]
        if not taps:
            continue
        rolled = big if sub == 0 else pltpu.roll(big, span - sub, 0)
        for k in taps:
            lo = offsets[k] - sub
            yield k, rolled[lo:lo + row_block]


def _tap_loop(src_ref, dst_ref, weight_ref, rows, offsets, weight_rows, row_block=64):
    d = dst_ref.shape[-1]

    def block(cb, carry):
        c0 = pl.multiple_of(cb * 128, 128)
        for r0 in range(0, rows, row_block):
            acc = jnp.zeros((row_block, 128), F32)
            for k, window in _shifted_windows(src_ref, r0, c0, row_block, offsets):
                acc += weight_ref[pl.ds(weight_rows[k], 1), pl.ds(c0, 128)] * window
            dst_ref[pl.ds(r0, row_block), pl.ds(c0, 128)] = acc
        return carry

    lax.fori_loop(0, d // 128, block, 0)


def _conv_fwd(h1, dw, dwb, g, b, name, tm=256):
    _, t, _ = h1.shape
    d = dw.shape[-1]
    tm = min(tm, t)
    hb = tm // CONV_HALO

    def body(h_ref, halo_ref, dw_ref, dwb_ref, g_ref, b_ref, s_ref, q_ref, ext_ref):
        i = pl.program_id(0)
        ext_ref[pl.ds(0, CONV_HALO), :] = _glu(halo_ref[...]) * (i > 0).astype(F32)
        ext_ref[pl.ds(CONV_HALO, tm), :] = _glu(h_ref[...])
        _tap_loop(ext_ref, q_ref, dw_ref, tm, [2 + k for k in range(CONV_W)], list(range(CONV_W)))
        q = q_ref[...] + dwb_ref[...]
        q_ref[...] = q
        qhat, _ = _ln_stats(q)
        z = qhat * g_ref[...] + b_ref[...]
        s_ref[...] = (z * _sigmoid(z)).astype(BF16)

    row = pl.BlockSpec((1, d), lambda i: (0, 0))
    tile = pl.BlockSpec((tm, d), lambda i: (i, 0))
    return pl.pallas_call(
        body, out_shape=(jax.ShapeDtypeStruct((t, d), BF16), jax.ShapeDtypeStruct((t, d), F32)), grid=(t // tm,),
        in_specs=[pl.BlockSpec((8, tm, 256), lambda i: (0, i, 0)),
                  pl.BlockSpec((8, CONV_HALO, 256), lambda i: (0, jnp.maximum(i * hb - 1, 0), 0)),
                  pl.BlockSpec((CONV_W, d), lambda i: (0, 0)), row, row, row],
        out_specs=(tile, tile), scratch_shapes=[pltpu.VMEM((tm + CONV_HALO, d), F32)],
        name=name, compiler_params=_params(1))(h1, h1, dw, dwb, g, b)


def _conv_bwd(ds, q, h1, dw, g, b, name, tm=256):
    _, t, _ = h1.shape
    d = dw.shape[-1]
    tm = min(tm, t)
    hb = tm // CONV_HALO
    n_t = t // tm
    last_halo = t // CONV_HALO - 1

    def body(ds_ref, dsn_ref, q_ref, qn_ref, h_ref, hp_ref, dw_ref, g_ref, b_ref,
             dh_ref, ddw_ref, ddwb_ref, dg_ref, db_ref, dq_ref, p_ref, dp_ref):
        i = pl.program_id(0)
        first = i == 0
        valid = (i < n_t - 1).astype(F32)

        def dq_rows(ds_rows, q_rows, scale):
            qhat, rstd = _ln_stats(q_rows)
            z = qhat * g_ref[...] + b_ref[...]
            sg = _sigmoid(z)
            dz = ds_rows.astype(F32) * (sg * (1.0 + z * (1.0 - sg))) * scale
            dq, dg, db = _ln_bwd(dz, qhat, rstd, g_ref[...])
            return dq, dg, db

        dq, dg, db = dq_rows(ds_ref[...], q_ref[...], 1.0)
        dq_ref[pl.ds(0, tm), :] = dq
        dq_ref[pl.ds(tm, CONV_HALO), :] = dq_rows(dsn_ref[...], qn_ref[...], valid)[0]
        _accumulate(dg_ref, dg, first)
        _accumulate(db_ref, db, first)
        _accumulate(ddwb_ref, jnp.sum(dq, axis=0, keepdims=True), first)

        p_ref[pl.ds(0, CONV_HALO), :] = _glu(hp_ref[...]) * (i > 0).astype(F32)
        p_ref[pl.ds(CONV_HALO, tm), :] = _glu(h_ref[...])

        _tap_loop(dq_ref, dp_ref, dw_ref, tm, list(range(CONV_W)), [CONV_W - 1 - o for o in range(CONV_W)])

        @pl.when(first)
        def _():
            ddw_ref[...] = jnp.zeros_like(ddw_ref)

        row_block = 64

        def block(cb, carry):
            c0 = pl.multiple_of(cb * 128, 128)
            for r0 in range(0, tm, row_block):
                dqb = dq_ref[pl.ds(r0, row_block), pl.ds(c0, 128)]
                for k, window in _shifted_windows(p_ref, r0, c0, row_block, [2 + k for k in range(CONV_W)]):
                    prod = dqb * window
                    ddw_ref[k, :, pl.ds(c0, 128)] += jnp.sum(prod.reshape(row_block // 8, 8, 128), axis=0)
            return carry

        lax.fori_loop(0, d // 128, block, 0)

        h = h_ref[...]
        a = _cat_lanes(h, 0, 4).astype(F32)
        sg = _sigmoid(_cat_lanes(h, 4, 8).astype(F32))
        dp = dp_ref[...]
        da = (dp * sg).astype(BF16)
        dgate = (dp * a * sg * (1.0 - sg)).astype(BF16)
        for s in range(4):
            dh_ref[s] = da[:, s * 256:(s + 1) * 256]
            dh_ref[4 + s] = dgate[:, s * 256:(s + 1) * 256]

    row = pl.BlockSpec((1, d), lambda i: (0, 0))
    tile = pl.BlockSpec((tm, d), lambda i: (i, 0))
    nxt = pl.BlockSpec((CONV_HALO, d), lambda i: (jnp.minimum((i + 1) * hb, last_halo), 0))
    return pl.pallas_call(
        body,
        out_shape=(jax.ShapeDtypeStruct((8, t, 256), BF16), jax.ShapeDtypeStruct((CONV_W, 8, d), F32),
                   jax.ShapeDtypeStruct((1, d), F32), jax.ShapeDtypeStruct((1, d), F32),
                   jax.ShapeDtypeStruct((1, d), F32)),
        grid=(n_t,),
        in_specs=[tile, nxt, tile, nxt,
                  pl.BlockSpec((8, tm, 256), lambda i: (0, i, 0)),
                  pl.BlockSpec((8, CONV_HALO, 256), lambda i: (0, jnp.maximum(i * hb - 1, 0), 0)),
                  pl.BlockSpec((CONV_W, d), lambda i: (0, 0)), row, row],
        out_specs=(pl.BlockSpec((8, tm, 256), lambda i: (0, i, 0)),
                   pl.BlockSpec((CONV_W, 8, d), lambda i: (0, 0, 0)), row, row, row),
        scratch_shapes=[pltpu.VMEM((tm + CONV_HALO, d), F32), pltpu.VMEM((tm + CONV_HALO, d), F32),
                        pltpu.VMEM((tm, d), F32)],
        name=name, compiler_params=_params(1))(ds, ds, q, q, h1, h1, dw, g, b)


def _conv3(ext, dw):
    e1 = pltpu.roll(ext, 1, 0)
    e2 = pltpu.roll(ext, 2, 0)
    return dw[2:3] * ext + dw[1:2] * e1 + dw[0:1] * e2, e1, e2


def _ffn_fwd(x, w_up, fdw, w_down, res, gp, bp, g, b, name, tm=256):
    t, d = x.shape
    _, _, n = w_up.shape
    tm = min(tm, t)
    carry_rows = 8

    def body(x_ref, wu_ref, dw_ref, wd_ref, res_ref, gp_ref, bp_ref, g_ref, b_ref,
             u_ref, h_ref, a_ref, xhat_ref, y_ref, rstd_ref, carry_ref):
        @pl.when(pl.program_id(0) == 0)
        def _():
            carry_ref[...] = jnp.zeros_like(carry_ref)

        x_tile = x_ref[...]
        acc = None
        for j in range(4):
            h = []
            for s in (j, 4 + j):
                ub = _dot(x_tile, wu_ref[s]).astype(BF16)
                u_ref[s] = ub
                uf = ub.astype(F32)
                ext = jnp.concatenate([carry_ref[s], uf], axis=0)
                carry_ref[s] = uf[tm - carry_rows:]
                hb = _conv3(ext, dw_ref[s])[0][carry_rows:].astype(BF16)
                h_ref[s] = hb
                h.append(hb.astype(F32))
            a = (h[0] * _sigmoid(h[0]) * h[1]).astype(BF16)
            a_ref[j] = a
            part = _dot(a, wd_ref[j])
            acc = part if acc is None else acc + part
        r = ALPHA * (res_ref[...] * gp_ref[...] + bp_ref[...]) + acc
        xhat, rstd = _ln_stats(r)
        xhat_ref[...] = xhat
        y_ref[...] = (xhat * g_ref[...] + b_ref[...]).astype(BF16)
        rstd_ref[...] = rstd

    row = pl.BlockSpec((1, d), lambda i: (0, 0))
    tile = pl.BlockSpec((tm, d), lambda i: (i, 0))
    return pl.pallas_call(
        body,
        out_shape=(jax.ShapeDtypeStruct((8, t, n), BF16), jax.ShapeDtypeStruct((8, t, n), BF16),
                   jax.ShapeDtypeStruct((4, t, n), BF16), jax.ShapeDtypeStruct((t, d), F32),
                   jax.ShapeDtypeStruct((t, d), BF16), jax.ShapeDtypeStruct((t, 1), F32)),
        grid=(t // tm,),
        in_specs=[tile, _resident((8, d, n)), pl.BlockSpec((8, 3, n), lambda i: (0, 0, 0)), _resident((4, n, d)),
                  tile, row, row, row, row],
        out_specs=(pl.BlockSpec((8, tm, n), lambda i: (0, i, 0)), pl.BlockSpec((8, tm, n), lambda i: (0, i, 0)),
                   pl.BlockSpec((4, tm, n), lambda i: (0, i, 0)),
                   tile, tile, pl.BlockSpec((tm, 1), lambda i: (i, 0))),
        scratch_shapes=[pltpu.VMEM((8, carry_rows, n), F32)],
        name=name, compiler_params=_params(1))(x, w_up, fdw, w_down, res, gp, bp, g, b)


def _ffn_gate_bwd(h, u, da, fdw, name, tm=256):
    _, t, n = u.shape
    tm = min(tm, t)
    hb = tm // FFN_HALO
    n_t = t // tm
    last_halo = t // FFN_HALO - 1
    h4, u4 = h.reshape(2, 4, t, n), u.reshape(2, 4, t, n)
    fdw4 = fdw.reshape(2, 4, 3, n)
    rows = tm + FFN_HALO

    def body(h_ref, hn_ref, u_ref, da_ref, dan_ref, dw_ref, du_ref, ddw_ref):
        i = pl.program_id(1)
        keep_next = (i < n_t - 1).astype(F32)
        hg = jnp.concatenate([h_ref[0], hn_ref[0]], axis=0).astype(F32)
        hv = jnp.concatenate([h_ref[1], hn_ref[1]], axis=0).astype(F32)
        da_ext = jnp.concatenate([da_ref[...].astype(F32), dan_ref[...].astype(F32) * keep_next], axis=0)
        sg = _sigmoid(hg)
        silu = hg * sg
        dh = (da_ext * hv * (sg + silu * (1.0 - sg)), da_ext * silu)
        for p in range(2):
            dwp = dw_ref[p]
            d1 = pltpu.roll(dh[p], rows - 1, 0)
            d2 = pltpu.roll(dh[p], rows - 2, 0)
            du_ref[p] = (dwp[2:3] * dh[p] + dwp[1:2] * d1 + dwp[0:1] * d2)[:tm].astype(BF16)
            up = u_ref[p].astype(F32)
            part = jnp.concatenate([jnp.sum(d[:tm] * up, axis=0, keepdims=True) for d in (d2, d1, dh[p])], axis=0)
            _accumulate(ddw_ref.at[p], part, i == 0)

    tile = pl.BlockSpec((2, None, tm, n), lambda j, i: (0, j, i, 0))
    nxt = pl.BlockSpec((2, None, FFN_HALO, n), lambda j, i: (0, j, jnp.minimum((i + 1) * hb, last_halo), 0))
    du, ddw = pl.pallas_call(
        body, out_shape=(jax.ShapeDtypeStruct((2, 4, t, n), BF16), jax.ShapeDtypeStruct((2, 4, 3, n), F32)),
        grid=(4, n_t),
        in_specs=[tile, nxt, tile,
                  pl.BlockSpec((None, tm, n), lambda j, i: (j, i, 0)),
                  pl.BlockSpec((None, FFN_HALO, n), lambda j, i: (j, jnp.minimum((i + 1) * hb, last_halo), 0)),
                  pl.BlockSpec((2, None, 3, n), lambda j, i: (0, j, 0, 0))],
        out_specs=(tile, pl.BlockSpec((2, None, 3, n), lambda j, i: (0, j, 0, 0))),
        name=name, compiler_params=_params(2))(h4, h4, u4, da, da, fdw4)
    return du.reshape(8, t, n), ddw.reshape(8, 3, n)


def _tril_mask():
    r = lax.broadcasted_iota(jnp.int32, (CHUNK, CHUNK), 0)
    c = lax.broadcasted_iota(jnp.int32, (CHUNK, CHUNK), 1)
    return (r >= c).astype(F32)


def _sgu_fwd(h1, g, b, ws, bst, name, tm=256):
    _, t, _ = h1.shape
    d = g.shape[-1]
    tm = min(tm, t)

    def body(h_ref, g_ref, b_ref, ws_ref, bst_ref, m_ref):
        h = h_ref[...]
        u = _gelu(_cat_lanes(h, 0, 4).astype(F32))
        v = _gelu(_cat_lanes(h, 4, 8).astype(F32))
        vn = (_ln_stats(v)[0] * g_ref[...] + b_ref[...]).astype(BF16)
        mask = _tril_mask()
        for hh in range(HEADS):
            cols = slice(hh * CHUNK, (hh + 1) * CHUNK)
            wm = (ws_ref[hh] * mask).astype(BF16)
            bias = bst_ref[:, hh:hh + 1]
            for c in range(tm // CHUNK):
                rows = slice(c * CHUNK, (c + 1) * CHUNK)
                sblk = _dot(wm, vn[rows, cols]) + bias
                m_ref[rows, cols] = (u[rows, cols] * sblk).astype(BF16)

    row = pl.BlockSpec((1, d), lambda i: (0, 0))
    return pl.pallas_call(
        body, out_shape=jax.ShapeDtypeStruct((t, d), BF16), grid=(t // tm,),
        in_specs=[pl.BlockSpec((8, tm, 256), lambda i: (0, i, 0)), row, row,
                  pl.BlockSpec((HEADS, CHUNK, CHUNK), lambda i: (0, 0, 0)),
                  pl.BlockSpec((CHUNK, HEADS), lambda i: (0, 0))],
        out_specs=pl.BlockSpec((tm, d), lambda i: (i, 0)),
        name=name, compiler_params=_params(1))(h1, g, b, ws, bst)


def _sgu_bwd(h1, dm, g, b, ws, bst, name, tm=256):
    _, t, _ = h1.shape
    d = g.shape[-1]
    tm = min(tm, t)

    def body(h_ref, dm_ref, g_ref, b_ref, ws_ref, bst_ref, dh_ref, dg_ref, db_ref, dws_ref, dbias_ref,
             du_ref, dvn_ref):
        first = pl.program_id(0) == 0
        h = h_ref[...]
        zu = _cat_lanes(h, 0, 4).astype(F32)
        zv = _cat_lanes(h, 4, 8).astype(F32)
        u = _gelu(zu)
        vhat, rstd = _ln_stats(_gelu(zv))
        vn = (vhat * g_ref[...] + b_ref[...]).astype(BF16)
        dm = dm_ref[...].astype(F32)
        mask = _tril_mask()

        @pl.when(first)
        def _():
            dws_ref[...] = jnp.zeros_like(dws_ref)
            dbias_ref[...] = jnp.zeros_like(dbias_ref)

        for hh in range(HEADS):
            cols = slice(hh * CHUNK, (hh + 1) * CHUNK)
            wm = (ws_ref[hh] * mask).astype(BF16)
            bias = bst_ref[:, hh:hh + 1]
            for c in range(tm // CHUNK):
                rows = slice(c * CHUNK, (c + 1) * CHUNK)
                vb = vn[rows, cols]
                sblk = _dot(wm, vb) + bias
                dmb = dm[rows, cols]
                du_ref[rows, cols] = dmb * sblk
                dsb = dmb * u[rows, cols]
                dbias_ref[:, cols] += dsb
                dsb16 = dsb.astype(BF16)
                dws_ref[hh] += _dot_nt(dsb16, vb) * mask
                dvn_ref[rows, cols] = _dot_tn(wm, dsb16)

        dvn = dvn_ref[...]
        dv, dg, db = _ln_bwd(dvn, vhat, rstd, g_ref[...])
        _accumulate(dg_ref, dg, first)
        _accumulate(db_ref, db, first)
        dzu = (du_ref[...] * _gelu_grad(zu)).astype(BF16)
        dzv = (dv * _gelu_grad(zv)).astype(BF16)
        for s in range(4):
            dh_ref[s] = dzu[:, s * 256:(s + 1) * 256]
            dh_ref[4 + s] = dzv[:, s * 256:(s + 1) * 256]

    row = pl.BlockSpec((1, d), lambda i: (0, 0))
    tile = pl.BlockSpec((tm, d), lambda i: (i, 0))
    h_tile = pl.BlockSpec((8, tm, 256), lambda i: (0, i, 0))
    return pl.pallas_call(
        body,
        out_shape=(jax.ShapeDtypeStruct((8, t, 256), BF16), jax.ShapeDtypeStruct((1, d), F32),
                   jax.ShapeDtypeStruct((1, d), F32), jax.ShapeDtypeStruct((HEADS, CHUNK, CHUNK), F32),
                   jax.ShapeDtypeStruct((CHUNK, d), F32)),
        grid=(t // tm,),
        in_specs=[h_tile, tile, row, row, pl.BlockSpec((HEADS, CHUNK, CHUNK), lambda i: (0, 0, 0)),
                  pl.BlockSpec((CHUNK, HEADS), lambda i: (0, 0))],
        out_specs=(h_tile, row, row, pl.BlockSpec((HEADS, CHUNK, CHUNK), lambda i: (0, 0, 0)),
                   pl.BlockSpec((CHUNK, d), lambda i: (0, 0))),
        scratch_shapes=[pltpu.VMEM((tm, d), F32), pltpu.VMEM((tm, d), F32)],
        name=name, compiler_params=_params(1))(h1, dm, g, b, ws, bst)


def _pool_minus_self(ext, first_token, grp):
    s = ext
    for step in range(grp + 1):
        s = s + pltpu.roll(s, 1 << step, 0)
    rows = ext.shape[0] - FFN_HALO
    tok = first_token + lax.broadcasted_iota(jnp.int32, (rows, 1), 0)
    count = jnp.minimum(tok + 1, POOL_WINDOWS[grp]).astype(F32)
    return s[FFN_HALO:] / count - ext[FFN_HALO:]


def _pool_fwd(y, wgrp, scale, name, tm=256):
    t, d = y.shape
    tm = min(tm, t)
    hb = tm // FFN_HALO
    gd = d // len(POOL_WINDOWS)

    def body(y_ref, yp_ref, w_ref, sc_ref, z_ref):
        i = pl.program_id(0)
        ext = jnp.concatenate([yp_ref[...] * (i > 0).astype(F32), y_ref[...]], axis=0)
        for grp in range(len(POOL_WINDOWS)):
            cols = slice(grp * gd, (grp + 1) * gd)
            p = _pool_minus_self(ext[:, cols], i * tm, grp)
            z_ref[:, cols] = (_dot(p.astype(BF16), w_ref[grp]) * sc_ref[:, cols]).astype(BF16)

    return pl.pallas_call(
        body, out_shape=jax.ShapeDtypeStruct((t, d), BF16), grid=(t // tm,),
        in_specs=[pl.BlockSpec((tm, d), lambda i: (i, 0)),
                  pl.BlockSpec((FFN_HALO, d), lambda i: (jnp.maximum(i * hb - 1, 0), 0)),
                  pl.BlockSpec((len(POOL_WINDOWS), gd, gd), lambda i: (0, 0, 0)),
                  pl.BlockSpec((1, d), lambda i: (0, 0))],
        out_specs=pl.BlockSpec((tm, d), lambda i: (i, 0)),
        name=name, compiler_params=_params(1))(y, y, wgrp, scale)


def _pool_bwd(y, dz, wgrp, scale, name, tm=256):
    t, d = y.shape
    tm = min(tm, t)
    hb = tm // FFN_HALO
    n_t = t // tm
    last_halo = t // FFN_HALO - 1
    gd = d // len(POOL_WINDOWS)
    rows = tm + FFN_HALO

    def body(y_ref, yp_ref, dz_ref, dzn_ref, w_ref, sc_ref, dy_ref, dsc_ref, dw_ref):
        i = pl.program_id(0)
        first = i == 0
        ext = jnp.concatenate([yp_ref[...] * (i > 0).astype(F32), y_ref[...]], axis=0)
        dz_ext = jnp.concatenate([dz_ref[...].astype(F32), dzn_ref[...].astype(F32) * (i < n_t - 1).astype(F32)],
                                 axis=0)
        tok = i * tm + lax.broadcasted_iota(jnp.int32, (rows, 1), 0)
        dsc = []
        for grp in range(len(POOL_WINDOWS)):
            cols = slice(grp * gd, (grp + 1) * gd)
            p16 = _pool_minus_self(ext[:, cols], i * tm, grp).astype(BF16)
            zg = _dot(p16, w_ref[grp])
            dsc.append(jnp.sum(dz_ext[:tm, cols] * zg, axis=0, keepdims=True))
            dzg = (dz_ext[:, cols] * sc_ref[:, cols]).astype(BF16)
            _accumulate(dw_ref.at[grp], _dot_tn(p16, dzg[:tm]), first)
            dp = _dot_nt(dzg, w_ref[grp])
            s = dp / jnp.minimum(tok + 1, POOL_WINDOWS[grp]).astype(F32)
            for step in range(grp + 1):
                s = s + pltpu.roll(s, rows - (1 << step), 0)
            dy_ref[:, cols] = (s[:tm] - dp[:tm]).astype(BF16)
        _accumulate(dsc_ref, jnp.concatenate(dsc, axis=-1), first)

    tile = pl.BlockSpec((tm, d), lambda i: (i, 0))
    return pl.pallas_call(
        body,
        out_shape=(jax.ShapeDtypeStruct((t, d), BF16), jax.ShapeDtypeStruct((1, d), F32),
                   jax.ShapeDtypeStruct((len(POOL_WINDOWS), gd, gd), F32)),
        grid=(n_t,),
        in_specs=[tile, pl.BlockSpec((FFN_HALO, d), lambda i: (jnp.maximum(i * hb - 1, 0), 0)),
                  tile, pl.BlockSpec((FFN_HALO, d), lambda i: (jnp.minimum((i + 1) * hb, last_halo), 0)),
                  pl.BlockSpec((len(POOL_WINDOWS), gd, gd), lambda i: (0, 0, 0)),
                  pl.BlockSpec((1, d), lambda i: (0, 0))],
        out_specs=(tile, pl.BlockSpec((1, d), lambda i: (0, 0)),
                   pl.BlockSpec((len(POOL_WINDOWS), gd, gd), lambda i: (0, 0, 0))),
        name=name, compiler_params=_params(1))(y, y, dz, dz, wgrp, scale)


def _loss_head(xhat, rstd, g, b, target, name, tm=512):
    t, d = xhat.shape
    tm = min(tm, t)

    def body(xhat_ref, rstd_ref, g_ref, b_ref, tgt_ref, dr_ref, drb_ref, dg_ref, db_ref, sq_ref):
        first = pl.program_id(0) == 0
        xhat_t = xhat_ref[...]
        diff = xhat_t * g_ref[...] + b_ref[...] - tgt_ref[...]
        dr, dg, db = _ln_bwd(diff * (1.0 / d), xhat_t, rstd_ref[...], g_ref[...])
        dr_ref[...] = dr
        drb_ref[...] = dr.astype(BF16)
        _accumulate(dg_ref, dg, first)
        _accumulate(db_ref, db, first)
        _accumulate(sq_ref, jnp.sum(diff * diff, axis=0, keepdims=True), first)

    row = pl.BlockSpec((1, d), lambda i: (0, 0))
    tile = pl.BlockSpec((tm, d), lambda i: (i, 0))
    return pl.pallas_call(
        body,
        out_shape=(jax.ShapeDtypeStruct((t, d), F32), jax.ShapeDtypeStruct((t, d), BF16),
                   jax.ShapeDtypeStruct((1, d), F32), jax.ShapeDtypeStruct((1, d), F32),
                   jax.ShapeDtypeStruct((1, d), F32)),
        grid=(t // tm,),
        in_specs=[tile, pl.BlockSpec((tm, 1), lambda i: (i, 0)), row, row, tile],
        out_specs=(tile, tile, row, row, row),
        name=name, compiler_params=_params(1))(xhat, rstd, g, b, target)


def _my_place():
    return lax.axis_index("x"), lax.axis_index("y"), lax.axis_index("c")


def _flip(coord, bit):
    return 1 - coord if bit else coord


def _all_gather(arrays, name):
    n = len(arrays)

    def body(*refs):
        ins, outs = refs[:n], refs[n:2 * n]
        send_sems, recv_sems, local_sems = refs[2 * n:]
        x, y, c = _my_place()
        me, sibling = (x, y, c), (x, y, 1 - c)
        chips = [(1 - x, y), (x, 1 - y), (1 - x, 1 - y)]

        def copy(a, k, block, to, src=None):
            idx = 4 * block[0] + 2 * block[1] + block[2]
            return pltpu.make_async_remote_copy(
                src_ref=outs[a].at[idx] if src is None else src, dst_ref=outs[a].at[idx],
                send_sem=send_sems.at[a, k], recv_sem=recv_sems.at[a, k], device_id=to, device_id_type=MESH)

        mine, first, passed = [], [], []
        for a in range(n):
            cp = pltpu.make_async_copy(ins[a], outs[a].at[4 * x + 2 * y + c], local_sems.at[a])
            cp.start()
            mine.append(cp)
            first.append(copy(a, 0, me, sibling, src=ins[a]))
            first += [copy(a, 1 + j, me, (*chip, c), src=ins[a]) for j, chip in enumerate(chips)]
        for cp in first:
            cp.start()
        for j, chip in enumerate(chips):
            for a in range(n):
                copy(a, 1 + j, (*chip, c), me).wait_recv()
                cp = copy(a, 4 + j, (*chip, c), sibling)
                cp.start()
                passed.append(cp)
        for a in range(n):
            copy(a, 0, sibling, me).wait_recv()
            for j, chip in enumerate(chips):
                copy(a, 4 + j, (*chip, 1 - c), me).wait_recv()
        for cp in first + passed:
            cp.wait_send()
        for cp in mine:
            cp.wait()

    hbm = pl.BlockSpec(memory_space=pltpu.HBM)
    return pl.pallas_call(
        body, out_shape=tuple(jax.ShapeDtypeStruct((N_DEV,) + a.shape, a.dtype) for a in arrays),
        in_specs=[hbm] * n, out_specs=tuple([hbm] * n),
        scratch_shapes=[pltpu.SemaphoreType.DMA((n, 7)), pltpu.SemaphoreType.DMA((n, 7)),
                        pltpu.SemaphoreType.DMA((n,))],
        name=name)(*arrays)


def _peers_of(x, y, c):
    peers = [(_flip(x, k & 4), _flip(y, k & 2), _flip(c, k & 1)) for k in range(1, N_DEV)]
    return peers, [4 * p[0] + 2 * p[1] + p[2] for p in peers]


def _exchange_copies(pieces, lands, send_sems, recv_sems, local_sems, whole):
    x, y, c = _my_place()
    me = 4 * x + 2 * y + c
    peers, slots = _peers_of(x, y, c)

    def piece(p, slot):
        return p if whole else p.at[slot]

    local = [pltpu.make_async_copy(piece(p, me), z.at[me], local_sems.at[a])
             for a, (p, z) in enumerate(zip(pieces, lands))]
    remote = []
    for k, peer in enumerate(peers):
        for a, (p, z) in enumerate(zip(pieces, lands)):
            sems = dict(send_sem=send_sems.at[7 * a + k], recv_sem=recv_sems.at[7 * a + k], device_id=peer,
                        device_id_type=MESH)
            remote.append((pltpu.make_async_remote_copy(src_ref=piece(p, slots[k]), dst_ref=z.at[me], **sems),
                           pltpu.make_async_remote_copy(src_ref=piece(p, slots[k]), dst_ref=z.at[slots[k]], **sems)))
    return local, remote


def _exchange_start(pieces, name, whole=False):
    n = len(pieces)

    def body(*refs):
        ins, lands = refs[:n], refs[n:2 * n]
        send_sems, recv_sems, local_sems = refs[2 * n:2 * n + 3]
        token_ref = refs[-1]
        local, remote = _exchange_copies(ins, lands, send_sems, recv_sems, local_sems, whole)
        for cp in local:
            cp.start()
        for cp, _ in remote:
            cp.start()
        token_ref[...] = jnp.zeros_like(token_ref)

    hbm = pl.BlockSpec(memory_space=pltpu.HBM)
    sem = pl.BlockSpec(memory_space=pltpu.SEMAPHORE)
    thru = [pltpu.HBM(p.shape, p.dtype) for p in pieces]
    zones = [pltpu.HBM(((N_DEV,) + p.shape) if whole else p.shape, p.dtype) for p in pieces]
    outs = pl.pallas_call(
        body,
        out_shape=(pltpu.SemaphoreType.DMA((7 * n,)), pltpu.SemaphoreType.DMA((7 * n,)), pltpu.SemaphoreType.DMA((n,)),
                   *thru, *zones, jax.ShapeDtypeStruct((8, 128), F32)),
        in_specs=[hbm] * (2 * n), out_specs=(sem, sem, sem, *([hbm] * (2 * n)), pl.BlockSpec(memory_space=pltpu.VMEM)),
        input_output_aliases={i: 3 + i for i in range(2 * n)},
        compiler_params=pltpu.CompilerParams(has_side_effects=pltpu.SideEffectType.DATAFLOW_SIDE_EFFECTING),
        name=name,
    )(*[pltpu.with_memory_space_constraint(p, pltpu.HBM) for p in pieces],
      *[pltpu.with_memory_space_constraint(lax.empty(z.shape, z.dtype), pltpu.HBM) for z in zones])
    return outs[:-1], outs[-1]


def _exchange_wait(handles, after, name, whole=False):
    send_sems, recv_sems, local_sems = handles[:3]
    n = (len(handles) - 3) // 2
    pieces, lands = handles[3:3 + n], handles[3 + n:]

    def body(*refs):
        ins, zones = refs[:n], refs[n:2 * n]
        s_sems, r_sems, l_sems = refs[2 * n:2 * n + 3]
        local, remote = _exchange_copies(ins, zones, s_sems, r_sems, l_sems, whole)
        for cp in local:
            cp.wait()
        for cp, landed in remote:
            cp.wait_send()
            landed.wait_recv()

    hbm = pl.BlockSpec(memory_space=pltpu.HBM)
    sem = pl.BlockSpec(memory_space=pltpu.SEMAPHORE)
    thru = [pltpu.HBM(p.shape, p.dtype) for p in list(pieces) + list(lands)]
    outs = pl.pallas_call(
        body, out_shape=tuple(thru),
        in_specs=[hbm] * (2 * n) + [sem, sem, sem, pl.BlockSpec(memory_space=pl.ANY)], out_specs=tuple([hbm] * (2 * n)),
        input_output_aliases={i: i for i in range(2 * n)},
        compiler_params=pltpu.CompilerParams(has_side_effects=pltpu.SideEffectType.DATAFLOW_SIDE_EFFECTING),
        name=name,
    )(*pieces, *lands, send_sems, recv_sems, local_sems, after)
    return outs[n:]


def _adamw(pieces, w, m, v, name, max_rows=256):
    n_l = len(pieces)
    _, r, cols = pieces[0].shape
    tr = max(rows for rows in range(16, min(max_rows, r) + 1, 16) if r % rows == 0)
    n_r = r // tr
    c1 = 1.0 / (1.0 - ADAM_B1 ** ADAM_STEP)
    c2 = 1.0 / (1.0 - ADAM_B2 ** ADAM_STEP)

    def body(*refs):
        p_refs = refs[:n_l]
        w_ref, m_ref, v_ref, g_ref, d_ref, nm_ref, nv_ref = refs[n_l:]
        for layer, p_ref in enumerate(p_refs):
            @pl.when(pl.program_id(0) == layer)
            def _(p_ref=p_ref):
                g = p_ref[0].astype(F32)
                for k in range(1, N_DEV):
                    g = g + p_ref[k].astype(F32)
                nm = ADAM_B1 * m_ref[...] + (1.0 - ADAM_B1) * g
                nv = ADAM_B2 * v_ref[...] + (1.0 - ADAM_B2) * (g * g)
                g_ref[...] = g
                nm_ref[...] = nm
                nv_ref[...] = nv
                d_ref[...] = -ADAM_LR * ((nm * c1) / (jnp.sqrt(nv * c2) + ADAM_EPS) + ADAM_WD * w_ref[...])

    def piece_spec(layer):
        return pl.BlockSpec((N_DEV, tr, cols), lambda l, i: (0, jnp.where(l == layer, i, 0), 0))

    tile = pl.BlockSpec((tr, cols), lambda l, i: (l * n_r + i, 0))
    out = jax.ShapeDtypeStruct((n_l * r, cols), F32)
    return pl.pallas_call(
        body, out_shape=(out, out, out, out), grid=(n_l, n_r),
        in_specs=[piece_spec(layer) for layer in range(n_l)] + [tile, tile, tile],
        out_specs=(tile, tile, tile, tile), name=name, compiler_params=_params(2))(*pieces, w, m, v)


def _rows_of(numel, row_tile):
    rows = -(-numel // LANES)
    return -(-rows // row_tile) * row_tile


def _pack(flat_list, row_tile, lead=()):
    parts = []
    for a in flat_list:
        numel = a.shape[-1]
        rows = _rows_of(numel, row_tile)
        pad = [(0, 0)] * len(lead) + [(0, rows * LANES - numel)]
        parts.append(jnp.pad(a, pad).reshape(*lead, rows, LANES))
    return jnp.concatenate(parts, axis=len(lead))


def _unpack(buf, shapes, row_tile, lead=()):
    out, r0 = [], 0
    for shape in shapes:
        numel = 1
        for s in shape:
            numel *= s
        rows = _rows_of(numel, row_tile)
        part = lax.slice_in_dim(buf, r0, r0 + rows, axis=len(lead)).reshape(*lead, rows * LANES)
        out.append(lax.slice_in_dim(part, 0, numel, axis=len(lead)).reshape(*lead, *shape))
        r0 += rows
    return out


def _to_shards(full, axis):
    shape = full.shape
    cut = full.reshape(shape[:axis] + (N_DEV, shape[axis] // N_DEV) + shape[axis + 1:])
    return jnp.moveaxis(cut, axis, 0)


def _step(x, target, w, m, v):
    t, d = x.shape[1], x.shape[2]
    x2 = x.reshape(t, d)
    tgt2 = target.reshape(t, d)

    small = _pack([w[k].reshape(-1) for k in GATHER_F32], 8)
    w_in_first, small_all = _all_gather([w["a_w_in"][:1].astype(BF16), small], "all_gather_first")
    gw = dict(zip(GATHER_F32, _unpack(small_all, [w[k].shape for k in GATHER_F32], 8, (N_DEV,))))
    mats = {("a_w_in", 0): w_in_first.reshape((N_DEV,) + w_in_first.shape[2:])}
    gather_groups = {
        "layer0": [("a_w_out", 0, 1), ("f_w_up", 0, 1), ("f_w_down", 0, 1)],
        "layer1": [("b_w_in", 0, 1), ("b_w_out", 0, 1), ("f_w_up", 1, 2), ("f_w_down", 1, 2)],
        "layer23": [("a_w_in", 1, 2), ("a_w_out", 1, 2), ("c_w_in", 0, 1), ("c_w_grp", 0, 1), ("c_w_out", 0, 1),
                    ("f_w_up", 2, 4), ("f_w_down", 2, 4)],
    }
    gather_handles = {}
    for tag, group in gather_groups.items():
        gather_handles[tag], gather_started = _exchange_start(
            [w[k][lo:hi].astype(BF16) for k, lo, hi in group], f"all_gather_{tag}_start", whole=True)

    def finish_gather(tag, after):
        lands = _exchange_wait(gather_handles[tag], after, f"all_gather_{tag}_wait", whole=True)
        for (k, lo, hi), land in zip(gather_groups[tag], lands):
            mats.update({(k, layer): land[:, layer - lo] for layer in range(lo, hi)})

    def mat(name, layer):
        return mats[(name, layer)]

    def full_cols(name, layer):
        a = gw[name][:, layer]
        if a.ndim == 2:
            return a.reshape(1, -1)
        return jnp.moveaxis(a, 0, 1).reshape(a.shape[1], -1)

    ones = jnp.ones((1, d), F32)
    zeros = jnp.zeros((1, d), F32)

    saved = []
    res, res_g, res_b = x2, ones, zeros
    xin = x2.astype(BF16)
    for i in range(DEPTH):
        kind, j = i % 3, i // 3
        sv = {"xin": xin, "kind": kind, "j": j}
        if i == 1:
            finish_gather("layer1", xin)
        if i == 2:
            finish_gather("layer23", xin)
        if kind == 0:
            w_in = mat("a_w_in", j)
            dw, dwb = full_cols("a_dw", j), full_cols("a_dw_b", j)
            if i == 0:
                dwb = dwb + gather_started[0:1, 0:1]
            lg, lb = full_cols("a_ln_g", j), full_cols("a_ln_b", j)
            h1 = _mm_nn(xin, w_in, BF16, f"conv_in_{i}")
            s_act, q = _conv_fwd(h1, dw, dwb, lg, lb, f"conv_mix_{i}")
            sv.update(h1=h1, q=q, w_in=w_in, dw=dw, lg=lg, lb=lb)
        elif kind == 1:
            w_in = mat("b_w_in", j)
            lg, lb = w["b_ln_g"][j].reshape(1, d), w["b_ln_b"][j].reshape(1, d)
            ws, bst = w["b_ws"][j], w["b_bs"][j].T
            h1 = _mm_nn(xin, w_in, BF16, f"sgu_in_{i}")
            s_act = _sgu_fwd(h1, lg, lb, ws, bst, f"sgu_mix_{i}")
            sv.update(h1=h1, w_in=w_in, lg=lg, lb=lb, ws=ws, bst=bst)
        else:
            w_in = mat("c_w_in", j).reshape(1, d, d)
            wgrp = jnp.moveaxis(mat("c_w_grp", j), 0, 1).reshape(4, d // 4, d // 4)
            scale = full_cols("c_scale", j)
            yp = _mm_nn(xin, w_in, F32, f"pool_in_{i}")[0]
            s_act = _pool_fwd(yp, wgrp, scale, f"pool_mix_{i}")
            sv.update(yp=yp, w_in=w_in, wgrp=wgrp, scale=scale)
        if i == 0:
            finish_gather("layer0", s_act)
        w_out = mat(("a_w_out", "b_w_out", "c_w_out")[kind], j).reshape(1, d, d)
        g1, b1 = w["ln1_g"][i].reshape(1, d), w["ln1_b"][i].reshape(1, d)
        xhat1, y1, rstd1 = _mm_res_ln(s_act.reshape(1, t, d), w_out, res, res_g, res_b, g1, b1, f"mix_out_ln_{i}")
        w_up = mat("f_w_up", i)
        fdw = gw["f_dw"][:, i]
        n_ff = w_up.shape[-1]
        w_down = mat("f_w_down", i).reshape(4, n_ff, d)
        g2, b2 = w["ln2_g"][i].reshape(1, d), w["ln2_b"][i].reshape(1, d)
        u, h_ffn, a_act, xhat2, y2, rstd2 = _ffn_fwd(y1, w_up, fdw, w_down, xhat1, g1, b1, g2, b2, f"ffn_fwd_{i}")
        sv.update(s_act=s_act, w_out=w_out, xhat1=xhat1, y1=y1, rstd1=rstd1, g1=g1, u=u, h_ffn=h_ffn, a_act=a_act,
                  w_up=w_up,
                  fdw=fdw, w_down=w_down, xhat2=xhat2, rstd2=rstd2, g2=g2, b2=b2)
        saved.append(sv)
        res, res_g, res_b, xin = xhat2, g2, b2, y2

    last = saved[-1]
    dr2, dr2b, dg2, db2, sq = _loss_head(last["xhat2"], last["rstd2"], last["g2"], last["b2"], tgt2, "loss_head")
    loss = lax.psum((0.5 / d) * jnp.sum(sq), ("x", "y", "c"))

    grads = {k: [None] * w[k].shape[0] for k in WEIGHTS}
    grad_x = None
    small_names = [k for k in WEIGHTS if k not in GATHER_BF16]
    exchanges = []

    def start_exchange(keys, tag, extra=()):
        pieces = [grads[k][l].astype(BF16).reshape(N_DEV, -1, w[k].shape[-1]) for k, l in keys]
        handles, token = _exchange_start(pieces + list(extra), f"exchange_start_{tag}")
        exchanges.append((keys, handles, tag))
        return token[0:1, 0:1]

    for i in reversed(range(DEPTH)):
        sv = saved[i]
        kind, j = sv["kind"], sv["j"]
        grads["ln2_g"][i], grads["ln2_b"][i] = dg2, db2
        da = _mm_nt_out(dr2b, sv["w_down"], f"ffn_da_{i}")
        grads["f_w_down"][i] = _to_shards(_mm_tn(sv["a_act"], dr2b.reshape(1, t, d), f"ffn_dwdown_{i}")
                                          .reshape(-1, d), 0)
        du, dfdw = _ffn_gate_bwd(sv["h_ffn"], sv["u"], da, sv["fdw"], f"ffn_gate_bwd_{i}")
        grads["f_dw"][i] = dfdw
        grads["f_w_up"][i] = _mm_tn(sv["y1"].reshape(1, t, d), du, f"ffn_dwup_{i}")
        started = start_exchange([("f_w_up", i), ("f_w_down", i)], f"ffn_{i}")
        dr1, dr1b, dg1, db1 = _mm_nt_lnb(du, sv["w_up"], dr2, sv["xhat1"], sv["rstd1"], sv["g1"], f"ffn_dx_ln_{i}")
        grads["ln1_g"][i], grads["ln1_b"][i] = dg1, db1
        ds = _mm_nt_out(dr1b, sv["w_out"], f"mix_ds_{i}")[0]
        dw_out = _to_shards(_mm_tn(sv["s_act"].reshape(1, t, d), dr1b.reshape(1, t, d), f"mix_dwout_{i}")[0], 0)
        xin3 = sv["xin"].reshape(1, t, d)
        if kind == 0:
            dh1, ddw, ddwb, dlg, dlb = _conv_bwd(ds, sv["q"], sv["h1"], sv["dw"], sv["lg"] + started, sv["lb"],
                                                  f"conv_mix_bwd_{i}")
            grads["a_w_out"][j] = dw_out
            grads["a_dw"][j] = _to_shards(jnp.sum(ddw, axis=1), 1)
            grads["a_dw_b"][j] = _to_shards(ddwb[0], 0)
            grads["a_ln_g"][j] = _to_shards(dlg[0], 0)
            grads["a_ln_b"][j] = _to_shards(dlb[0], 0)
            grads["a_w_in"][j] = _mm_tn(xin3, dh1, f"conv_dwin_{i}")
            dh_in, w_in = dh1, sv["w_in"]
            mixer_keys = [("a_w_in", j), ("a_w_out", j)]
        elif kind == 1:
            dh1, dlg, dlb, dws, dbias = _sgu_bwd(sv["h1"], ds, sv["lg"] + started, sv["lb"], sv["ws"], sv["bst"],
                                                  f"sgu_mix_bwd_{i}")
            grads["b_w_out"][j] = dw_out
            grads["b_ln_g"][j], grads["b_ln_b"][j] = dlg[0], dlb[0]
            grads["b_ws"][j] = dws
            grads["b_bs"][j] = jnp.sum(dbias.reshape(CHUNK, HEADS, CHUNK), axis=-1).T
            grads["b_w_in"][j] = _mm_tn(xin3, dh1, f"sgu_dwin_{i}")
            dh_in, w_in = dh1, sv["w_in"]
            mixer_keys = [("b_w_in", j), ("b_w_out", j)]
        else:
            dyp, dscale, dwgrp = _pool_bwd(sv["yp"], ds, sv["wgrp"], sv["scale"] + started, f"pool_mix_bwd_{i}")
            grads["c_w_out"][j] = dw_out
            grads["c_scale"][j] = _to_shards(dscale[0], 0)
            grads["c_w_grp"][j] = _to_shards(dwgrp, 1)
            dh_in, w_in = dyp.reshape(1, t, d), sv["w_in"]
            grads["c_w_in"][j] = _to_shards(_mm_tn(xin3, dh_in, f"pool_dwin_{i}")[0], 0)
            mixer_keys = [("c_w_in", j), ("c_w_grp", j), ("c_w_out", j)]
        if i > 0:
            started = start_exchange(mixer_keys, f"mixer_{i}")
            prev = saved[i - 1]
            prev["fdw"] = prev["fdw"] + started
            dr2, dr2b, dg2, db2 = _mm_nt_lnb(dh_in, w_in, dr1, prev["xhat2"], prev["rstd2"], prev["g2"],
                                              f"mix_dx_ln_{i}")
        else:
            flat = []
            for k in small_names:
                if k in REPLICATED:
                    full = jnp.stack([gk.reshape(w[k].shape[1:]) for gk in grads[k]], axis=0)
                    flat.append(jnp.broadcast_to(full.reshape(1, -1), (N_DEV, full.size)))
                else:
                    flat.append(jnp.stack(grads[k], axis=1).reshape(N_DEV, -1))
            small_pieces = _pack(flat, 8, (N_DEV,))
            pad_rows = -(-small_pieces.shape[1] // 128) * 128 - small_pieces.shape[1]
            small_pieces = jnp.pad(small_pieces, ((0, 0), (0, pad_rows), (0, 0)))
            start_exchange(mixer_keys, f"mixer_{i}", extra=[small_pieces])
            grad_x = _mm_nt_res(dh_in, w_in, dr1, "mix_dx_0").reshape(x.shape)

    kinds = ("grad", "delta", "new_m", "new_v")
    received, result = {}, {}

    def finish_exchange(group, after):
        keys, handles, tag = group
        lands = _exchange_wait(handles, after, f"exchange_wait_{tag}")
        received.update(zip(keys, lands))
        return lands

    def update(k):
        cols = w[k].shape[-1]
        bufs = _adamw([received[(k, l)] for l in range(w[k].shape[0])], w[k].reshape(-1, cols),
                      m[k].reshape(-1, cols), v[k].reshape(-1, cols), f"adamw_{k}")
        result.update({(kind, k): buf.reshape(w[k].shape) for kind, buf in zip(kinds, bufs)})

    for group in exchanges[:-1]:
        finish_exchange(group, grad_x)
    late = [k for k, _ in exchanges[-1][0]]
    for k in GATHER_BF16:
        if k not in late:
            update(k)
    small_received = finish_exchange(exchanges[-1], result[("new_v", "f_w_down")])[-1]
    for k in late:
        update(k)

    def packed(tree):
        return jnp.pad(_pack([tree[k].reshape(-1) for k in small_names], 8), ((0, pad_rows), (0, 0)))

    bufs = _adamw([small_received], packed(w), packed(m), packed(v), "adamw_small")
    shapes = [w[k].shape for k in small_names]
    for kind, buf in zip(kinds, bufs):
        result.update({(kind, k): a for k, a in zip(small_names, _unpack(buf, shapes, 8))})
    outs = [result[(kind, k)] for kind in kinds for k in WEIGHTS]
    return (loss, grad_x, *outs)


def kernel(x, a_w_in, a_dw, a_dw_b, a_ln_g, a_ln_b, a_w_out, b_w_in, b_ln_g, b_ln_b, b_ws, b_bs, b_w_out, c_w_in, c_w_grp, c_scale, c_w_out, f_w_up, f_dw, f_w_down, ln1_g, ln1_b, ln2_g, ln2_b, loss_target, m_a_w_in, m_a_dw, m_a_dw_b, m_a_ln_g, m_a_ln_b, m_a_w_out, m_b_w_in, m_b_ln_g, m_b_ln_b, m_b_ws, m_b_bs, m_b_w_out, m_c_w_in, m_c_w_grp, m_c_scale, m_c_w_out, m_f_w_up, m_f_dw, m_f_w_down, m_ln1_g, m_ln1_b, m_ln2_g, m_ln2_b, v_a_w_in, v_a_dw, v_a_dw_b, v_a_ln_g, v_a_ln_b, v_a_w_out, v_b_w_in, v_b_ln_g, v_b_ln_b, v_b_ws, v_b_bs, v_b_w_out, v_c_w_in, v_c_w_grp, v_c_scale, v_c_w_out, v_f_w_up, v_f_dw, v_f_w_down, v_ln1_g, v_ln1_b, v_ln2_g, v_ln2_b):
    w = dict(zip(WEIGHTS, (a_w_in, a_dw, a_dw_b, a_ln_g, a_ln_b, a_w_out, b_w_in, b_ln_g, b_ln_b, b_ws, b_bs, b_w_out,
                           c_w_in, c_w_grp, c_scale, c_w_out, f_w_up, f_dw, f_w_down, ln1_g, ln1_b, ln2_g, ln2_b)))
    m = dict(zip(WEIGHTS, (m_a_w_in, m_a_dw, m_a_dw_b, m_a_ln_g, m_a_ln_b, m_a_w_out, m_b_w_in, m_b_ln_g, m_b_ln_b,
                           m_b_ws, m_b_bs, m_b_w_out, m_c_w_in, m_c_w_grp, m_c_scale, m_c_w_out, m_f_w_up, m_f_dw,
                           m_f_w_down, m_ln1_g, m_ln1_b, m_ln2_g, m_ln2_b)))
    v = dict(zip(WEIGHTS, (v_a_w_in, v_a_dw, v_a_dw_b, v_a_ln_g, v_a_ln_b, v_a_w_out, v_b_w_in, v_b_ln_g, v_b_ln_b,
                           v_b_ws, v_b_bs, v_b_w_out, v_c_w_in, v_c_w_grp, v_c_scale, v_c_w_out, v_f_w_up, v_f_dw,
                           v_f_w_down, v_ln1_g, v_ln1_b, v_ln2_g, v_ln2_b)))
    return _step(x, loss_target, w, m, v)
```

```python
import functools

import jax
import jax.numpy as jnp
from jax import lax
from jax.experimental import pallas as pl
from jax.experimental.pallas import tpu as pltpu

N_DEV = 8
DEPTH = 4
ALPHA = float((2 * DEPTH) ** 0.25)
LN_EPS = 1e-5
CONV_W = 31
CONV_HALO = 32
FFN_HALO = 16
POOL_WINDOWS = (2, 4, 8, 16)
CHUNK = 128
HEADS = 8
LANES = 1024
ADAM_LR, ADAM_B1, ADAM_B2, ADAM_EPS, ADAM_WD, ADAM_STEP = 0.001, 0.9, 0.999, 1e-08, 0.01, 10
VMEM_LIMIT = 56 * 1024 * 1024
F32, BF16 = jnp.float32, jnp.bfloat16
MESH = pl.DeviceIdType.MESH

WEIGHTS = ['a_w_in', 'a_dw', 'a_dw_b', 'a_ln_g', 'a_ln_b', 'a_w_out', 'b_w_in', 'b_ln_g', 'b_ln_b', 'b_ws', 'b_bs',
           'b_w_out', 'c_w_in', 'c_w_grp', 'c_scale', 'c_w_out', 'f_w_up', 'f_dw', 'f_w_down', 'ln1_g', 'ln1_b',
           'ln2_g', 'ln2_b']
REPLICATED = ('b_ln_g', 'b_ln_b', 'b_ws', 'b_bs', 'ln1_g', 'ln1_b', 'ln2_g', 'ln2_b')
GATHER_BF16 = ('a_w_in', 'a_w_out', 'b_w_in', 'b_w_out', 'c_w_in', 'c_w_grp', 'c_w_out', 'f_w_up', 'f_w_down')
GATHER_F32 = ('a_dw', 'a_dw_b', 'a_ln_g', 'a_ln_b', 'c_scale', 'f_dw')


def _params(n_axes):
    return pltpu.CompilerParams(dimension_semantics=("arbitrary",) * n_axes, vmem_limit_bytes=VMEM_LIMIT)


def _resident(shape):
    zeros = (0,) * len(shape)
    return pl.BlockSpec(shape, lambda i: zeros, pipeline_mode=pl.Buffered(1))


def _sigmoid(x):
    return 1.0 / (1.0 + jnp.exp(-x))


def _gelu(x):
    return 0.5 * x * (1.0 + lax.erf(x * 0.7071067811865476))


def _gelu_grad(x):
    return 0.5 * (1.0 + lax.erf(x * 0.7071067811865476)) + x * jnp.exp(-0.5 * x * x) * 0.3989422804014327


def _ln_stats(r):
    mu = jnp.mean(r, axis=-1, keepdims=True)
    xc = r - mu
    var = jnp.mean(xc * xc, axis=-1, keepdims=True)
    rstd = lax.rsqrt(var + LN_EPS)
    return xc * rstd, rstd


def _ln_bwd(dy, xhat, rstd, g):
    dxhat = dy * g
    m1 = jnp.mean(dxhat, axis=-1, keepdims=True)
    m2 = jnp.mean(dxhat * xhat, axis=-1, keepdims=True)
    dr = rstd * (dxhat - m1 - xhat * m2)
    return dr, jnp.sum(dy * xhat, axis=0, keepdims=True), jnp.sum(dy, axis=0, keepdims=True)


def _accumulate(ref, value, first):
    @pl.when(first)
    def _():
        ref[...] = value

    @pl.when(jnp.logical_not(first))
    def _():
        ref[...] += value


def _dot(a, b):
    return jnp.dot(a, b, preferred_element_type=F32)


def _dot_nt(a, b):
    return lax.dot_general(a, b, (((1,), (1,)), ((), ())), preferred_element_type=F32)


def _dot_tn(a, b):
    return lax.dot_general(a, b, (((0,), (0,)), ((), ())), preferred_element_type=F32)


def _cat_lanes(h, lo, hi):
    return jnp.concatenate([h[s] for s in range(lo, hi)], axis=-1)


def _mm_nn(x, w, out_dtype, name, tm=512):
    t, k = x.shape
    s_n, _, n = w.shape
    tm = min(tm, t)

    def body(x_ref, w_ref, o_ref):
        x_tile = x_ref[...]
        for s in range(s_n):
            o_ref[s] = _dot(x_tile, w_ref[s]).astype(o_ref.dtype)

    return pl.pallas_call(
        body, out_shape=jax.ShapeDtypeStruct((s_n, t, n), out_dtype), grid=(t // tm,),
        in_specs=[pl.BlockSpec((tm, k), lambda i: (i, 0)), _resident((s_n, k, n))],
        out_specs=pl.BlockSpec((s_n, tm, n), lambda i: (0, i, 0)),
        name=name, compiler_params=_params(1))(x, w)


def _mm_res_ln(a, w, res, gp, bp, g, b, name, tm=512):
    s_n, t, ka = a.shape
    d = w.shape[-1]
    tm = min(tm, t)

    def body(a_ref, w_ref, res_ref, gp_ref, bp_ref, g_ref, b_ref, xhat_ref, y_ref, rstd_ref):
        acc = _dot(a_ref[0], w_ref[0])
        for s in range(1, s_n):
            acc += _dot(a_ref[s], w_ref[s])
        r = ALPHA * (res_ref[...] * gp_ref[...] + bp_ref[...]) + acc
        xhat, rstd = _ln_stats(r)
        xhat_ref[...] = xhat
        y_ref[...] = (xhat * g_ref[...] + b_ref[...]).astype(BF16)
        rstd_ref[...] = rstd

    row = pl.BlockSpec((1, d), lambda i: (0, 0))
    tile = pl.BlockSpec((tm, d), lambda i: (i, 0))
    return pl.pallas_call(
        body,
        out_shape=(jax.ShapeDtypeStruct((t, d), F32), jax.ShapeDtypeStruct((t, d), BF16),
                   jax.ShapeDtypeStruct((t, 1), F32)),
        grid=(t // tm,),
        in_specs=[pl.BlockSpec((s_n, tm, ka), lambda i: (0, i, 0)), _resident((s_n, ka, d)),
                  tile, row, row, row, row],
        out_specs=(tile, tile, pl.BlockSpec((tm, 1), lambda i: (i, 0))),
        name=name, compiler_params=_params(1))(a, w, res, gp, bp, g, b)


def _mm_nt_out(x, w, name, tm=512):
    t, n = x.shape
    s_n, k, _ = w.shape
    tm = min(tm, t)

    def body(x_ref, w_ref, o_ref):
        x_tile = x_ref[...]
        for s in range(s_n):
            o_ref[s] = _dot_nt(x_tile, w_ref[s]).astype(o_ref.dtype)

    return pl.pallas_call(
        body, out_shape=jax.ShapeDtypeStruct((s_n, t, k), BF16), grid=(t // tm,),
        in_specs=[pl.BlockSpec((tm, n), lambda i: (i, 0)), _resident((s_n, k, n))],
        out_specs=pl.BlockSpec((s_n, tm, k), lambda i: (0, i, 0)),
        name=name, compiler_params=_params(1))(x, w)


def _mm_nt_lnb(dh, w, drn, xhat, rstd, g, name, tm=512):
    s_n, t, n = dh.shape
    k = w.shape[1]
    tm = min(tm, t)

    def body(dh_ref, w_ref, drn_ref, xhat_ref, rstd_ref, g_ref, dr_ref, drb_ref, dg_ref, db_ref):
        first = pl.program_id(0) == 0
        acc = _dot_nt(dh_ref[0], w_ref[0])
        for s in range(1, s_n):
            acc += _dot_nt(dh_ref[s], w_ref[s])
        dy = acc + ALPHA * drn_ref[...]
        dr, dg, db = _ln_bwd(dy, xhat_ref[...], rstd_ref[...], g_ref[...])
        dr_ref[...] = dr
        drb_ref[...] = dr.astype(BF16)
        _accumulate(dg_ref, dg, first)
        _accumulate(db_ref, db, first)

    tile = pl.BlockSpec((tm, k), lambda i: (i, 0))
    row = pl.BlockSpec((1, k), lambda i: (0, 0))
    return pl.pallas_call(
        body,
        out_shape=(jax.ShapeDtypeStruct((t, k), F32), jax.ShapeDtypeStruct((t, k), BF16),
                   jax.ShapeDtypeStruct((1, k), F32), jax.ShapeDtypeStruct((1, k), F32)),
        grid=(t // tm,),
        in_specs=[pl.BlockSpec((s_n, tm, n), lambda i: (0, i, 0)), _resident((s_n, k, n)),
                  tile, tile, pl.BlockSpec((tm, 1), lambda i: (i, 0)), row],
        out_specs=(tile, tile, row, row),
        name=name, compiler_params=_params(1))(dh, w, drn, xhat, rstd, g)


def _mm_nt_res(dh, w, drn, name, tm=512):
    s_n, t, n = dh.shape
    k = w.shape[1]
    tm = min(tm, t)

    def body(dh_ref, w_ref, drn_ref, o_ref):
        acc = _dot_nt(dh_ref[0], w_ref[0])
        for s in range(1, s_n):
            acc += _dot_nt(dh_ref[s], w_ref[s])
        o_ref[...] = acc + ALPHA * drn_ref[...]

    tile = pl.BlockSpec((tm, k), lambda i: (i, 0))
    return pl.pallas_call(
        body, out_shape=jax.ShapeDtypeStruct((t, k), F32), grid=(t // tm,),
        in_specs=[pl.BlockSpec((s_n, tm, n), lambda i: (0, i, 0)), _resident((s_n, k, n)), tile],
        out_specs=tile, name=name, compiler_params=_params(1))(dh, w, drn)


def _mm_tn(lhs, rhs, name, tm=2048):
    sl, t, kl = lhs.shape
    sr, _, n = rhs.shape
    s_n = max(sl, sr)
    tm = min(tm, t)

    def body(l_ref, r_ref, o_ref):
        _accumulate(o_ref, _dot_tn(l_ref[...], r_ref[...]), pl.program_id(1) == 0)

    return pl.pallas_call(
        body, out_shape=jax.ShapeDtypeStruct((s_n, kl, n), F32), grid=(s_n, t // tm),
        in_specs=[pl.BlockSpec((None, tm, kl), (lambda s, i: (s, i, 0)) if sl > 1 else (lambda s, i: (0, i, 0))),
                  pl.BlockSpec((None, tm, n), (lambda s, i: (s, i, 0)) if sr > 1 else (lambda s, i: (0, i, 0)))],
        out_specs=pl.BlockSpec((None, kl, n), lambda s, i: (s, 0, 0)),
        name=name, compiler_params=_params(2))(lhs, rhs)


def _glu(h):
    return _cat_lanes(h, 0, 4).astype(F32) * _sigmoid(_cat_lanes(h, 4, 8).astype(F32))


def _shifted_windows(src_ref, r0, c0, row_block, offsets):
    span = row_block + CONV_HALO
    big = src_ref[pl.ds(r0, span), pl.ds(c0, 128)]
    for sub in range(8):
        taps = [k for k, o in enumerate(offsets) if o % 8 == sub]
        if not taps:
            continue
        rolled = big if sub == 0 else pltpu.roll(big, span - sub, 0)
        for k in taps:
            lo = offsets[k] - sub
            yield k, rolled[lo:lo + row_block]


def _tap_loop(src_ref, dst_ref, weight_ref, rows, offsets, weight_rows, row_block=64):
    d = dst_ref.shape[-1]

    def block(cb, carry):
        c0 = pl.multiple_of(cb * 128, 128)
        for r0 in range(0, rows, row_block):
            acc = jnp.zeros((row_block, 128), F32)
            for k, window in _shifted_windows(src_ref, r0, c0, row_block, offsets):
                acc += weight_ref[pl.ds(weight_rows[k], 1), pl.ds(c0, 128)] * window
            dst_ref[pl.ds(r0, row_block), pl.ds(c0, 128)] = acc
        return carry

    lax.fori_loop(0, d // 128, block, 0)


def _conv_fwd(h1, dw, dwb, g, b, name, tm=256):
    _, t, _ = h1.shape
    d = dw.shape[-1]
    tm = min(tm, t)
    hb = tm // CONV_HALO

    def body(h_ref, halo_ref, dw_ref, dwb_ref, g_ref, b_ref, s_ref, q_ref, ext_ref):
        i = pl.program_id(0)
        ext_ref[pl.ds(0, CONV_HALO), :] = _glu(halo_ref[...]) * (i > 0).astype(F32)
        ext_ref[pl.ds(CONV_HALO, tm), :] = _glu(h_ref[...])
        _tap_loop(ext_ref, q_ref, dw_ref, tm, [2 + k for k in range(CONV_W)], list(range(CONV_W)))
        q = q_ref[...] + dwb_ref[...]
        q_ref[...] = q
        qhat, _ = _ln_stats(q)
        z = qhat * g_ref[...] + b_ref[...]
        s_ref[...] = (z * _sigmoid(z)).astype(BF16)

    row = pl.BlockSpec((1, d), lambda i: (0, 0))
    tile = pl.BlockSpec((tm, d), lambda i: (i, 0))
    return pl.pallas_call(
        body, out_shape=(jax.ShapeDtypeStruct((t, d), BF16), jax.ShapeDtypeStruct((t, d), F32)), grid=(t // tm,),
        in_specs=[pl.BlockSpec((8, tm, 256), lambda i: (0, i, 0)),
                  pl.BlockSpec((8, CONV_HALO, 256), lambda i: (0, jnp.maximum(i * hb - 1, 0), 0)),
                  pl.BlockSpec((CONV_W, d), lambda i: (0, 0)), row, row, row],
        out_specs=(tile, tile), scratch_shapes=[pltpu.VMEM((tm + CONV_HALO, d), F32)],
        name=name, compiler_params=_params(1))(h1, h1, dw, dwb, g, b)


def _conv_bwd(ds, q, h1, dw, g, b, name, tm=256):
    _, t, _ = h1.shape
    d = dw.shape[-1]
    tm = min(tm, t)
    hb = tm // CONV_HALO
    n_t = t // tm
    last_halo = t // CONV_HALO - 1

    def body(ds_ref, dsn_ref, q_ref, qn_ref, h_ref, hp_ref, dw_ref, g_ref, b_ref,
             dh_ref, ddw_ref, ddwb_ref, dg_ref, db_ref, dq_ref, p_ref, dp_ref):
        i = pl.program_id(0)
        first = i == 0
        valid = (i < n_t - 1).astype(F32)

        def dq_rows(ds_rows, q_rows, scale):
            qhat, rstd = _ln_stats(q_rows)
            z = qhat * g_ref[...] + b_ref[...]
            sg = _sigmoid(z)
            dz = ds_rows.astype(F32) * (sg * (1.0 + z * (1.0 - sg))) * scale
            dq, dg, db = _ln_bwd(dz, qhat, rstd, g_ref[...])
            return dq, dg, db

        dq, dg, db = dq_rows(ds_ref[...], q_ref[...], 1.0)
        dq_ref[pl.ds(0, tm), :] = dq
        dq_ref[pl.ds(tm, CONV_HALO), :] = dq_rows(dsn_ref[...], qn_ref[...], valid)[0]
        _accumulate(dg_ref, dg, first)
        _accumulate(db_ref, db, first)
        _accumulate(ddwb_ref, jnp.sum(dq, axis=0, keepdims=True), first)

        p_ref[pl.ds(0, CONV_HALO), :] = _glu(hp_ref[...]) * (i > 0).astype(F32)
        p_ref[pl.ds(CONV_HALO, tm), :] = _glu(h_ref[...])

        _tap_loop(dq_ref, dp_ref, dw_ref, tm, list(range(CONV_W)), [CONV_W - 1 - o for o in range(CONV_W)])

        @pl.when(first)
        def _():
            ddw_ref[...] = jnp.zeros_like(ddw_ref)

        row_block = 64

        def block(cb, carry):
            c0 = pl.multiple_of(cb * 128, 128)
            for r0 in range(0, tm, row_block):
                dqb = dq_ref[pl.ds(r0, row_block), pl.ds(c0, 128)]
                for k, window in _shifted_windows(p_ref, r0, c0, row_block, [2 + k for k in range(CONV_W)]):
                    prod = dqb * window
                    ddw_ref[k, :, pl.ds(c0, 128)] += jnp.sum(prod.reshape(row_block // 8, 8, 128), axis=0)
            return carry

        lax.fori_loop(0, d // 128, block, 0)

        h = h_ref[...]
        a = _cat_lanes(h, 0, 4).astype(F32)
        sg = _sigmoid(_cat_lanes(h, 4, 8).astype(F32))
        dp = dp_ref[...]
        da = (dp * sg).astype(BF16)
        dgate = (dp * a * sg * (1.0 - sg)).astype(BF16)
        for s in range(4):
            dh_ref[s] = da[:, s * 256:(s + 1) * 256]
            dh_ref[4 + s] = dgate[:, s * 256:(s + 1) * 256]

    row = pl.BlockSpec((1, d), lambda i: (0, 0))
    tile = pl.BlockSpec((tm, d), lambda i: (i, 0))
    nxt = pl.BlockSpec((CONV_HALO, d), lambda i: (jnp.minimum((i + 1) * hb, last_halo), 0))
    return pl.pallas_call(
        body,
        out_shape=(jax.ShapeDtypeStruct((8, t, 256), BF16), jax.ShapeDtypeStruct((CONV_W, 8, d), F32),
                   jax.ShapeDtypeStruct((1, d), F32), jax.ShapeDtypeStruct((1, d), F32),
                   jax.ShapeDtypeStruct((1, d), F32)),
        grid=(n_t,),
        in_specs=[tile, nxt, tile, nxt,
                  pl.BlockSpec((8, tm, 256), lambda i: (0, i, 0)),
                  pl.BlockSpec((8, CONV_HALO, 256), lambda i: (0, jnp.maximum(i * hb - 1, 0), 0)),
                  pl.BlockSpec((CONV_W, d), lambda i: (0, 0)), row, row],
        out_specs=(pl.BlockSpec((8, tm, 256), lambda i: (0, i, 0)),
                   pl.BlockSpec((CONV_W, 8, d), lambda i: (0, 0, 0)), row, row, row),
        scratch_shapes=[pltpu.VMEM((tm + CONV_HALO, d), F32), pltpu.VMEM((tm + CONV_HALO, d), F32),
                        pltpu.VMEM((tm, d), F32)],
        name=name, compiler_params=_params(1))(ds, ds, q, q, h1, h1, dw, g, b)


def _conv3(ext, dw):
    e1 = pltpu.roll(ext, 1, 0)
    e2 = pltpu.roll(ext, 2, 0)
    return dw[2:3] * ext + dw[1:2] * e1 + dw[0:1] * e2, e1, e2


def _ffn_fwd(x, w_up, fdw, w_down, res, gp, bp, g, b, name, tm=256):
    t, d = x.shape
    _, _, n = w_up.shape
    tm = min(tm, t)
    carry_rows = 8

    def body(x_ref, wu_ref, dw_ref, wd_ref, res_ref, gp_ref, bp_ref, g_ref, b_ref,
             u_ref, h_ref, a_ref, xhat_ref, y_ref, rstd_ref, carry_ref):
        @pl.when(pl.program_id(0) == 0)
        def _():
            carry_ref[...] = jnp.zeros_like(carry_ref)

        x_tile = x_ref[...]
        acc = None
        for j in range(4):
            h = []
            for s in (j, 4 + j):
                ub = _dot(x_tile, wu_ref[s]).astype(BF16)
                u_ref[s] = ub
                uf = ub.astype(F32)
                ext = jnp.concatenate([carry_ref[s], uf], axis=0)
                carry_ref[s] = uf[tm - carry_rows:]
                hb = _conv3(ext, dw_ref[s])[0][carry_rows:].astype(BF16)
                h_ref[s] = hb
                h.append(hb.astype(F32))
            a = (h[0] * _sigmoid(h[0]) * h[1]).astype(BF16)
            a_ref[j] = a
            part = _dot(a, wd_ref[j])
            acc = part if acc is None else acc + part
        r = ALPHA * (res_ref[...] * gp_ref[...] + bp_ref[...]) + acc
        xhat, rstd = _ln_stats(r)
        xhat_ref[...] = xhat
        y_ref[...] = (xhat * g_ref[...] + b_ref[...]).astype(BF16)
        rstd_ref[...] = rstd

    row = pl.BlockSpec((1, d), lambda i: (0, 0))
    tile = pl.BlockSpec((tm, d), lambda i: (i, 0))
    return pl.pallas_call(
        body,
        out_shape=(jax.ShapeDtypeStruct((8, t, n), BF16), jax.ShapeDtypeStruct((8, t, n), BF16),
                   jax.ShapeDtypeStruct((4, t, n), BF16), jax.ShapeDtypeStruct((t, d), F32),
                   jax.ShapeDtypeStruct((t, d), BF16), jax.ShapeDtypeStruct((t, 1), F32)),
        grid=(t // tm,),
        in_specs=[tile, _resident((8, d, n)), pl.BlockSpec((8, 3, n), lambda i: (0, 0, 0)), _resident((4, n, d)),
                  tile, row, row, row, row],
        out_specs=(pl.BlockSpec((8, tm, n), lambda i: (0, i, 0)), pl.BlockSpec((8, tm, n), lambda i: (0, i, 0)),
                   pl.BlockSpec((4, tm, n), lambda i: (0, i, 0)),
                   tile, tile, pl.BlockSpec((tm, 1), lambda i: (i, 0))),
        scratch_shapes=[pltpu.VMEM((8, carry_rows, n), F32)],
        name=name, compiler_params=_params(1))(x, w_up, fdw, w_down, res, gp, bp, g, b)


def _ffn_gate_bwd(h, u, da, fdw, name, tm=256):
    _, t, n = u.shape
    tm = min(tm, t)
    hb = tm // FFN_HALO
    n_t = t // tm
    last_halo = t // FFN_HALO - 1
    h4, u4 = h.reshape(2, 4, t, n), u.reshape(2, 4, t, n)
    fdw4 = fdw.reshape(2, 4, 3, n)
    rows = tm + FFN_HALO

    def body(h_ref, hn_ref, u_ref, da_ref, dan_ref, dw_ref, du_ref, ddw_ref):
        i = pl.program_id(1)
        keep_next = (i < n_t - 1).astype(F32)
        hg = jnp.concatenate([h_ref[0], hn_ref[0]], axis=0).astype(F32)
        hv = jnp.concatenate([h_ref[1], hn_ref[1]], axis=0).astype(F32)
        da_ext = jnp.concatenate([da_ref[...].astype(F32), dan_ref[...].astype(F32) * keep_next], axis=0)
        sg = _sigmoid(hg)
        silu = hg * sg
        dh = (da_ext * hv * (sg + silu * (1.0 - sg)), da_ext * silu)
        for p in range(2):
            dwp = dw_ref[p]
            d1 = pltpu.roll(dh[p], rows - 1, 0)
            d2 = pltpu.roll(dh[p], rows - 2, 0)
            du_ref[p] = (dwp[2:3] * dh[p] + dwp[1:2] * d1 + dwp[0:1] * d2)[:tm].astype(BF16)
            up = u_ref[p].astype(F32)
            part = jnp.concatenate([jnp.sum(d[:tm] * up, axis=0, keepdims=True) for d in (d2, d1, dh[p])], axis=0)
            _accumulate(ddw_ref.at[p], part, i == 0)

    tile = pl.BlockSpec((2, None, tm, n), lambda j, i: (0, j, i, 0))
    nxt = pl.BlockSpec((2, None, FFN_HALO, n), lambda j, i: (0, j, jnp.minimum((i + 1) * hb, last_halo), 0))
    du, ddw = pl.pallas_call(
        body, out_shape=(jax.ShapeDtypeStruct((2, 4, t, n), BF16), jax.ShapeDtypeStruct((2, 4, 3, n), F32)),
        grid=(4, n_t),
        in_specs=[tile, nxt, tile,
                  pl.BlockSpec((None, tm, n), lambda j, i: (j, i, 0)),
                  pl.BlockSpec((None, FFN_HALO, n), lambda j, i: (j, jnp.minimum((i + 1) * hb, last_halo), 0)),
                  pl.BlockSpec((2, None, 3, n), lambda j, i: (0, j, 0, 0))],
        out_specs=(tile, pl.BlockSpec((2, None, 3, n), lambda j, i: (0, j, 0, 0))),
        name=name, compiler_params=_params(2))(h4, h4, u4, da, da, fdw4)
    return du.reshape(8, t, n), ddw.reshape(8, 3, n)


def _tril_mask():
    r = lax.broadcasted_iota(jnp.int32, (CHUNK, CHUNK), 0)
    c = lax.broadcasted_iota(jnp.int32, (CHUNK, CHUNK), 1)
    return (r >= c).astype(F32)


def _sgu_fwd(h1, g, b, ws, bst, name, tm=256):
    _, t, _ = h1.shape
    d = g.shape[-1]
    tm = min(tm, t)

    def body(h_ref, g_ref, b_ref, ws_ref, bst_ref, m_ref):
        h = h_ref[...]
        u = _gelu(_cat_lanes(h, 0, 4).astype(F32))
        v = _gelu(_cat_lanes(h, 4, 8).astype(F32))
        vn = (_ln_stats(v)[0] * g_ref[...] + b_ref[...]).astype(BF16)
        mask = _tril_mask()
        for hh in range(HEADS):
            cols = slice(hh * CHUNK, (hh + 1) * CHUNK)
            wm = (ws_ref[hh] * mask).astype(BF16)
            bias = bst_ref[:, hh:hh + 1]
            for c in range(tm // CHUNK):
                rows = slice(c * CHUNK, (c + 1) * CHUNK)
                sblk = _dot(wm, vn[rows, cols]) + bias
                m_ref[rows, cols] = (u[rows, cols] * sblk).astype(BF16)

    row = pl.BlockSpec((1, d), lambda i: (0, 0))
    return pl.pallas_call(
        body, out_shape=jax.ShapeDtypeStruct((t, d), BF16), grid=(t // tm,),
        in_specs=[pl.BlockSpec((8, tm, 256), lambda i: (0, i, 0)), row, row,
                  pl.BlockSpec((HEADS, CHUNK, CHUNK), lambda i: (0, 0, 0)),
                  pl.BlockSpec((CHUNK, HEADS), lambda i: (0, 0))],
        out_specs=pl.BlockSpec((tm, d), lambda i: (i, 0)),
        name=name, compiler_params=_params(1))(h1, g, b, ws, bst)


def _sgu_bwd(h1, dm, g, b, ws, bst, name, tm=256):
    _, t, _ = h1.shape
    d = g.shape[-1]
    tm = min(tm, t)

    def body(h_ref, dm_ref, g_ref, b_ref, ws_ref, bst_ref, dh_ref, dg_ref, db_ref, dws_ref, dbias_ref,
             du_ref, dvn_ref):
        first = pl.program_id(0) == 0
        h = h_ref[...]
        zu = _cat_lanes(h, 0, 4).astype(F32)
        zv = _cat_lanes(h, 4, 8).astype(F32)
        u = _gelu(zu)
        vhat, rstd = _ln_stats(_gelu(zv))
        vn = (vhat * g_ref[...] + b_ref[...]).astype(BF16)
        dm = dm_ref[...].astype(F32)
        mask = _tril_mask()

        @pl.when(first)
        def _():
            dws_ref[...] = jnp.zeros_like(dws_ref)
            dbias_ref[...] = jnp.zeros_like(dbias_ref)

        for hh in range(HEADS):
            cols = slice(hh * CHUNK, (hh + 1) * CHUNK)
            wm = (ws_ref[hh] * mask).astype(BF16)
            bias = bst_ref[:, hh:hh + 1]
            for c in range(tm // CHUNK):
                rows = slice(c * CHUNK, (c + 1) * CHUNK)
                vb = vn[rows, cols]
                sblk = _dot(wm, vb) + bias
                dmb = dm[rows, cols]
                du_ref[rows, cols] = dmb * sblk
                dsb = dmb * u[rows, cols]
                dbias_ref[:, cols] += dsb
                dsb16 = dsb.astype(BF16)
                dws_ref[hh] += _dot_nt(dsb16, vb) * mask
                dvn_ref[rows, cols] = _dot_tn(wm, dsb16)

        dvn = dvn_ref[...]
        dv, dg, db = _ln_bwd(dvn, vhat, rstd, g_ref[...])
        _accumulate(dg_ref, dg, first)
        _accumulate(db_ref, db, first)
        dzu = (du_ref[...] * _gelu_grad(zu)).astype(BF16)
        dzv = (dv * _gelu_grad(zv)).astype(BF16)
        for s in range(4):
            dh_ref[s] = dzu[:, s * 256:(s + 1) * 256]
            dh_ref[4 + s] = dzv[:, s * 256:(s + 1) * 256]

    row = pl.BlockSpec((1, d), lambda i: (0, 0))
    tile = pl.BlockSpec((tm, d), lambda i: (i, 0))
    h_tile = pl.BlockSpec((8, tm, 256), lambda i: (0, i, 0))
    return pl.pallas_call(
        body,
        out_shape=(jax.ShapeDtypeStruct((8, t, 256), BF16), jax.ShapeDtypeStruct((1, d), F32),
                   jax.ShapeDtypeStruct((1, d), F32), jax.ShapeDtypeStruct((HEADS, CHUNK, CHUNK), F32),
                   jax.ShapeDtypeStruct((CHUNK, d), F32)),
        grid=(t // tm,),
        in_specs=[h_tile, tile, row, row, pl.BlockSpec((HEADS, CHUNK, CHUNK), lambda i: (0, 0, 0)),
                  pl.BlockSpec((CHUNK, HEADS), lambda i: (0, 0))],
        out_specs=(h_tile, row, row, pl.BlockSpec((HEADS, CHUNK, CHUNK), lambda i: (0, 0, 0)),
                   pl.BlockSpec((CHUNK, d), lambda i: (0, 0))),
        scratch_shapes=[pltpu.VMEM((tm, d), F32), pltpu.VMEM((tm, d), F32)],
        name=name, compiler_params=_params(1))(h1, dm, g, b, ws, bst)


def _pool_minus_self(ext, first_token, grp):
    s = ext
    for step in range(grp + 1):
        s = s + pltpu.roll(s, 1 << step, 0)
    rows = ext.shape[0] - FFN_HALO
    tok = first_token + lax.broadcasted_iota(jnp.int32, (rows, 1), 0)
    count = jnp.minimum(tok + 1, POOL_WINDOWS[grp]).astype(F32)
    return s[FFN_HALO:] / count - ext[FFN_HALO:]


def _pool_fwd(y, wgrp, scale, name, tm=256):
    t, d = y.shape
    tm = min(tm, t)
    hb = tm // FFN_HALO
    gd = d // len(POOL_WINDOWS)

    def body(y_ref, yp_ref, w_ref, sc_ref, z_ref):
        i = pl.program_id(0)
        ext = jnp.concatenate([yp_ref[...] * (i > 0).astype(F32), y_ref[...]], axis=0)
        for grp in range(len(POOL_WINDOWS)):
            cols = slice(grp * gd, (grp + 1) * gd)
            p = _pool_minus_self(ext[:, cols], i * tm, grp)
            z_ref[:, cols] = (_dot(p.astype(BF16), w_ref[grp]) * sc_ref[:, cols]).astype(BF16)

    return pl.pallas_call(
        body, out_shape=jax.ShapeDtypeStruct((t, d), BF16), grid=(t // tm,),
        in_specs=[pl.BlockSpec((tm, d), lambda i: (i, 0)),
                  pl.BlockSpec((FFN_HALO, d), lambda i: (jnp.maximum(i * hb - 1, 0), 0)),
                  pl.BlockSpec((len(POOL_WINDOWS), gd, gd), lambda i: (0, 0, 0)),
                  pl.BlockSpec((1, d), lambda i: (0, 0))],
        out_specs=pl.BlockSpec((tm, d), lambda i: (i, 0)),
        name=name, compiler_params=_params(1))(y, y, wgrp, scale)


def _pool_bwd(y, dz, wgrp, scale, name, tm=256):
    t, d = y.shape
    tm = min(tm, t)
    hb = tm // FFN_HALO
    n_t = t // tm
    last_halo = t // FFN_HALO - 1
    gd = d // len(POOL_WINDOWS)
    rows = tm + FFN_HALO

    def body(y_ref, yp_ref, dz_ref, dzn_ref, w_ref, sc_ref, dy_ref, dsc_ref, dw_ref):
        i = pl.program_id(0)
        first = i == 0
        ext = jnp.concatenate([yp_ref[...] * (i > 0).astype(F32), y_ref[...]], axis=0)
        dz_ext = jnp.concatenate([dz_ref[...].astype(F32), dzn_ref[...].astype(F32) * (i < n_t - 1).astype(F32)],
                                 axis=0)
        tok = i * tm + lax.broadcasted_iota(jnp.int32, (rows, 1), 0)
        dsc = []
        for grp in range(len(POOL_WINDOWS)):
            cols = slice(grp * gd, (grp + 1) * gd)
            p16 = _pool_minus_self(ext[:, cols], i * tm, grp).astype(BF16)
            zg = _dot(p16, w_ref[grp])
            dsc.append(jnp.sum(dz_ext[:tm, cols] * zg, axis=0, keepdims=True))
            dzg = (dz_ext[:, cols] * sc_ref[:, cols]).astype(BF16)
            _accumulate(dw_ref.at[grp], _dot_tn(p16, dzg[:tm]), first)
            dp = _dot_nt(dzg, w_ref[grp])
            s = dp / jnp.minimum(tok + 1, POOL_WINDOWS[grp]).astype(F32)
            for step in range(grp + 1):
                s = s + pltpu.roll(s, rows - (1 << step), 0)
            dy_ref[:, cols] = (s[:tm] - dp[:tm]).astype(BF16)
        _accumulate(dsc_ref, jnp.concatenate(dsc, axis=-1), first)

    tile = pl.BlockSpec((tm, d), lambda i: (i, 0))
    return pl.pallas_call(
        body,
        out_shape=(jax.ShapeDtypeStruct((t, d), BF16), jax.ShapeDtypeStruct((1, d), F32),
                   jax.ShapeDtypeStruct((len(POOL_WINDOWS), gd, gd), F32)),
        grid=(n_t,),
        in_specs=[tile, pl.BlockSpec((FFN_HALO, d), lambda i: (jnp.maximum(i * hb - 1, 0), 0)),
                  tile, pl.BlockSpec((FFN_HALO, d), lambda i: (jnp.minimum((i + 1) * hb, last_halo), 0)),
                  pl.BlockSpec((len(POOL_WINDOWS), gd, gd), lambda i: (0, 0, 0)),
                  pl.BlockSpec((1, d), lambda i: (0, 0))],
        out_specs=(tile, pl.BlockSpec((1, d), lambda i: (0, 0)),
                   pl.BlockSpec((len(POOL_WINDOWS), gd, gd), lambda i: (0, 0, 0))),
        name=name, compiler_params=_params(1))(y, y, dz, dz, wgrp, scale)


def _loss_head(xhat, rstd, g, b, target, name, tm=512):
    t, d = xhat.shape
    tm = min(tm, t)

    def body(xhat_ref, rstd_ref, g_ref, b_ref, tgt_ref, dr_ref, drb_ref, dg_ref, db_ref, sq_ref):
        first = pl.program_id(0) == 0
        xhat_t = xhat_ref[...]
        diff = xhat_t * g_ref[...] + b_ref[...] - tgt_ref[...]
        dr, dg, db = _ln_bwd(diff * (1.0 / d), xhat_t, rstd_ref[...], g_ref[...])
        dr_ref[...] = dr
        drb_ref[...] = dr.astype(BF16)
        _accumulate(dg_ref, dg, first)
        _accumulate(db_ref, db, first)
        _accumulate(sq_ref, jnp.sum(diff * diff, axis=0, keepdims=True), first)

    row = pl.BlockSpec((1, d), lambda i: (0, 0))
    tile = pl.BlockSpec((tm, d), lambda i: (i, 0))
    return pl.pallas_call(
        body,
        out_shape=(jax.ShapeDtypeStruct((t, d), F32), jax.ShapeDtypeStruct((t, d), BF16),
                   jax.ShapeDtypeStruct((1, d), F32), jax.ShapeDtypeStruct((1, d), F32),
                   jax.ShapeDtypeStruct((1, d), F32)),
        grid=(t // tm,),
        in_specs=[tile, pl.BlockSpec((tm, 1), lambda i: (i, 0)), row, row, tile],
        out_specs=(tile, tile, row, row, row),
        name=name, compiler_params=_params(1))(xhat, rstd, g, b, target)


def _my_place():
    return lax.axis_index("x"), lax.axis_index("y"), lax.axis_index("c")


def _flip(coord, bit):
    return 1 - coord if bit else coord


def _all_gather(arrays, name):
    n = len(arrays)

    def body(*refs):
        ins, outs = refs[:n], refs[n:2 * n]
        send_sems, recv_sems, local_sems = refs[2 * n:]
        x, y, c = _my_place()
        me, sibling = (x, y, c), (x, y, 1 - c)
        chips = [(1 - x, y), (x, 1 - y), (1 - x, 1 - y)]

        def copy(a, k, block, to, src=None):
            idx = 4 * block[0] + 2 * block[1] + block[2]
            return pltpu.make_async_remote_copy(
                src_ref=outs[a].at[idx] if src is None else src, dst_ref=outs[a].at[idx],
                send_sem=send_sems.at[a, k], recv_sem=recv_sems.at[a, k], device_id=to, device_id_type=MESH)

        mine, first, passed = [], [], []
        for a in range(n):
            cp = pltpu.make_async_copy(ins[a], outs[a].at[4 * x + 2 * y + c], local_sems.at[a])
            cp.start()
            mine.append(cp)
            first.append(copy(a, 0, me, sibling, src=ins[a]))
            first += [copy(a, 1 + j, me, (*chip, c), src=ins[a]) for j, chip in enumerate(chips)]
        for cp in first:
            cp.start()
        for j, chip in enumerate(chips):
            for a in range(n):
                copy(a, 1 + j, (*chip, c), me).wait_recv()
                cp = copy(a, 4 + j, (*chip, c), sibling)
                cp.start()
                passed.append(cp)
        for a in range(n):
            copy(a, 0, sibling, me).wait_recv()
            for j, chip in enumerate(chips):
                copy(a, 4 + j, (*chip, 1 - c), me).wait_recv()
        for cp in first + passed:
            cp.wait_send()
        for cp in mine:
            cp.wait()

    hbm = pl.BlockSpec(memory_space=pltpu.HBM)
    return pl.pallas_call(
        body, out_shape=tuple(jax.ShapeDtypeStruct((N_DEV,) + a.shape, a.dtype) for a in arrays),
        in_specs=[hbm] * n, out_specs=tuple([hbm] * n),
        scratch_shapes=[pltpu.SemaphoreType.DMA((n, 7)), pltpu.SemaphoreType.DMA((n, 7)),
                        pltpu.SemaphoreType.DMA((n,))],
        name=name)(*arrays)


def _peers_of(x, y, c):
    peers = [(_flip(x, k & 4), _flip(y, k & 2), _flip(c, k & 1)) for k in range(1, N_DEV)]
    return peers, [4 * p[0] + 2 * p[1] + p[2] for p in peers]


def _exchange_copies(pieces, lands, send_sems, recv_sems, local_sems, whole):
    x, y, c = _my_place()
    me = 4 * x + 2 * y + c
    peers, slots = _peers_of(x, y, c)

    def piece(p, slot):
        return p if whole else p.at[slot]

    local = [pltpu.make_async_copy(piece(p, me), z.at[me], local_sems.at[a])
             for a, (p, z) in enumerate(zip(pieces, lands))]
    remote = []
    for k, peer in enumerate(peers):
        for a, (p, z) in enumerate(zip(pieces, lands)):
            sems = dict(send_sem=send_sems.at[7 * a + k], recv_sem=recv_sems.at[7 * a + k], device_id=peer,
                        device_id_type=MESH)
            remote.append((pltpu.make_async_remote_copy(src_ref=piece(p, slots[k]), dst_ref=z.at[me], **sems),
                           pltpu.make_async_remote_copy(src_ref=piece(p, slots[k]), dst_ref=z.at[slots[k]], **sems)))
    return local, remote


def _exchange_start(pieces, name, whole=False):
    n = len(pieces)

    def body(*refs):
        ins, lands = refs[:n], refs[n:2 * n]
        send_sems, recv_sems, local_sems = refs[2 * n:2 * n + 3]
        token_ref = refs[-1]
        local, remote = _exchange_copies(ins, lands, send_sems, recv_sems, local_sems, whole)
        for cp in local:
            cp.start()
        for cp, _ in remote:
            cp.start()
        token_ref[...] = jnp.zeros_like(token_ref)

    hbm = pl.BlockSpec(memory_space=pltpu.HBM)
    sem = pl.BlockSpec(memory_space=pltpu.SEMAPHORE)
    thru = [pltpu.HBM(p.shape, p.dtype) for p in pieces]
    zones = [pltpu.HBM(((N_DEV,) + p.shape) if whole else p.shape, p.dtype) for p in pieces]
    outs = pl.pallas_call(
        body,
        out_shape=(pltpu.SemaphoreType.DMA((7 * n,)), pltpu.SemaphoreType.DMA((7 * n,)), pltpu.SemaphoreType.DMA((n,)),
                   *thru, *zones, jax.ShapeDtypeStruct((8, 128), F32)),
        in_specs=[hbm] * (2 * n), out_specs=(sem, sem, sem, *([hbm] * (2 * n)), pl.BlockSpec(memory_space=pltpu.VMEM)),
        input_output_aliases={i: 3 + i for i in range(2 * n)},
        compiler_params=pltpu.CompilerParams(has_side_effects=pltpu.SideEffectType.DATAFLOW_SIDE_EFFECTING),
        name=name,
    )(*[pltpu.with_memory_space_constraint(p, pltpu.HBM) for p in pieces],
      *[pltpu.with_memory_space_constraint(lax.empty(z.shape, z.dtype), pltpu.HBM) for z in zones])
    return outs[:-1], outs[-1]


def _exchange_wait(handles, after, name, whole=False):
    send_sems, recv_sems, local_sems = handles[:3]
    n = (len(handles) - 3) // 2
    pieces, lands = handles[3:3 + n], handles[3 + n:]

    def body(*refs):
        ins, zones = refs[:n], refs[n:2 * n]
        s_sems, r_sems, l_sems = refs[2 * n:2 * n + 3]
        local, remote = _exchange_copies(ins, zones, s_sems, r_sems, l_sems, whole)
        for cp in local:
            cp.wait()
        for cp, landed in remote:
            cp.wait_send()
            landed.wait_recv()

    hbm = pl.BlockSpec(memory_space=pltpu.HBM)
    sem = pl.BlockSpec(memory_space=pltpu.SEMAPHORE)
    thru = [pltpu.HBM(p.shape, p.dtype) for p in list(pieces) + list(lands)]
    outs = pl.pallas_call(
        body, out_shape=tuple(thru),
        in_specs=[hbm] * (2 * n) + [sem, sem, sem, pl.BlockSpec(memory_space=pl.ANY)], out_specs=tuple([hbm] * (2 * n)),
        input_output_aliases={i: i for i in range(2 * n)},
        compiler_params=pltpu.CompilerParams(has_side_effects=pltpu.SideEffectType.DATAFLOW_SIDE_EFFECTING),
        name=name,
    )(*pieces, *lands, send_sems, recv_sems, local_sems, after)
    return outs[n:]


def _adamw(pieces, w, m, v, name, max_rows=256):
    n_l = len(pieces)
    _, r, cols = pieces[0].shape
    tr = max(rows for rows in range(16, min(max_rows, r) + 1, 16) if r % rows == 0)
    n_r = r // tr
    c1 = 1.0 / (1.0 - ADAM_B1 ** ADAM_STEP)
    c2 = 1.0 / (1.0 - ADAM_B2 ** ADAM_STEP)

    def body(*refs):
        p_refs = refs[:n_l]
        w_ref, m_ref, v_ref, g_ref, d_ref, nm_ref, nv_ref = refs[n_l:]
        for layer, p_ref in enumerate(p_refs):
            @pl.when(pl.program_id(0) == layer)
            def _(p_ref=p_ref):
                g = p_ref[0].astype(F32)
                for k in range(1, N_DEV):
                    g = g + p_ref[k].astype(F32)
                nm = ADAM_B1 * m_ref[...] + (1.0 - ADAM_B1) * g
                nv = ADAM_B2 * v_ref[...] + (1.0 - ADAM_B2) * (g * g)
                g_ref[...] = g
                nm_ref[...] = nm
                nv_ref[...] = nv
                d_ref[...] = -ADAM_LR * ((nm * c1) / (jnp.sqrt(nv * c2) + ADAM_EPS) + ADAM_WD * w_ref[...])

    def piece_spec(layer):
        return pl.BlockSpec((N_DEV, tr, cols), lambda l, i: (0, jnp.where(l == layer, i, 0), 0))

    tile = pl.BlockSpec((tr, cols), lambda l, i: (l * n_r + i, 0))
    out = jax.ShapeDtypeStruct((n_l * r, cols), F32)
    return pl.pallas_call(
        body, out_shape=(out, out, out, out), grid=(n_l, n_r),
        in_specs=[piece_spec(layer) for layer in range(n_l)] + [tile, tile, tile],
        out_specs=(tile, tile, tile, tile), name=name, compiler_params=_params(2))(*pieces, w, m, v)


def _rows_of(numel, row_tile):
    rows = -(-numel // LANES)
    return -(-rows // row_tile) * row_tile


def _pack(flat_list, row_tile, lead=()):
    parts = []
    for a in flat_list:
        numel = a.shape[-1]
        rows = _rows_of(numel, row_tile)
        pad = [(0, 0)] * len(lead) + [(0, rows * LANES - numel)]
        parts.append(jnp.pad(a, pad).reshape(*lead, rows, LANES))
    return jnp.concatenate(parts, axis=len(lead))


def _unpack(buf, shapes, row_tile, lead=()):
    out, r0 = [], 0
    for shape in shapes:
        numel = 1
        for s in shape:
            numel *= s
        rows = _rows_of(numel, row_tile)
        part = lax.slice_in_dim(buf, r0, r0 + rows, axis=len(lead)).reshape(*lead, rows * LANES)
        out.append(lax.slice_in_dim(part, 0, numel, axis=len(lead)).reshape(*lead, *shape))
        r0 += rows
    return out


def _to_shards(full, axis):
    shape = full.shape
    cut = full.reshape(shape[:axis] + (N_DEV, shape[axis] // N_DEV) + shape[axis + 1:])
    return jnp.moveaxis(cut, axis, 0)


def _step(x, target, w, m, v):
    t, d = x.shape[1], x.shape[2]
    x2 = x.reshape(t, d)
    tgt2 = target.reshape(t, d)

    small = _pack([w[k].reshape(-1) for k in GATHER_F32], 8)
    w_in_first, small_all = _all_gather([w["a_w_in"][:1].astype(BF16), small], "all_gather_first")
    gw = dict(zip(GATHER_F32, _unpack(small_all, [w[k].shape for k in GATHER_F32], 8, (N_DEV,))))
    mats = {("a_w_in", 0): w_in_first.reshape((N_DEV,) + w_in_first.shape[2:])}
    gather_groups = {
        "layer0": [("a_w_out", 0, 1), ("f_w_up", 0, 1), ("f_w_down", 0, 1)],
        "layer1": [("b_w_in", 0, 1), ("b_w_out", 0, 1), ("f_w_up", 1, 2), ("f_w_down", 1, 2)],
        "layer23": [("a_w_in", 1, 2), ("a_w_out", 1, 2), ("c_w_in", 0, 1), ("c_w_grp", 0, 1), ("c_w_out", 0, 1),
                    ("f_w_up", 2, 4), ("f_w_down", 2, 4)],
    }
    gather_handles = {}

    def start_gather(tag):
        gather_handles[tag], token = _exchange_start(
            [w[k][lo:hi].astype(BF16) for k, lo, hi in gather_groups[tag]], f"all_gather_{tag}_start", whole=True)
        return token[0:1, 0:1]

    gather_started = start_gather("layer0")

    def finish_gather(tag, after):
        lands = _exchange_wait(gather_handles[tag], after, f"all_gather_{tag}_wait", whole=True)
        for (k, lo, hi), land in zip(gather_groups[tag], lands):
            mats.update({(k, layer): land[:, layer - lo] for layer in range(lo, hi)})

    def mat(name, layer):
        return mats[(name, layer)]

    def full_cols(name, layer):
        a = gw[name][:, layer]
        if a.ndim == 2:
            return a.reshape(1, -1)
        return jnp.moveaxis(a, 0, 1).reshape(a.shape[1], -1)

    ones = jnp.ones((1, d), F32)
    zeros = jnp.zeros((1, d), F32)

    saved = []
    res, res_g, res_b = x2, ones, zeros
    xin = x2.astype(BF16)
    for i in range(DEPTH):
        kind, j = i % 3, i // 3
        sv = {"xin": xin, "kind": kind, "j": j}
        if i == 1:
            finish_gather("layer1", xin)
            gather_started = start_gather("layer23")
        if i == 2:
            finish_gather("layer23", xin)
        if kind == 0:
            w_in = mat("a_w_in", j)
            dw, dwb = full_cols("a_dw", j), full_cols("a_dw_b", j)
            if i == 0:
                dwb = dwb + gather_started
            lg, lb = full_cols("a_ln_g", j), full_cols("a_ln_b", j)
            h1 = _mm_nn(xin, w_in, BF16, f"conv_in_{i}")
            s_act, q = _conv_fwd(h1, dw, dwb, lg, lb, f"conv_mix_{i}")
            sv.update(h1=h1, q=q, w_in=w_in, dw=dw, lg=lg, lb=lb)
        elif kind == 1:
            w_in = mat("b_w_in", j)
            lg, lb = w["b_ln_g"][j].reshape(1, d) + gather_started, w["b_ln_b"][j].reshape(1, d)
            ws, bst = w["b_ws"][j], w["b_bs"][j].T
            h1 = _mm_nn(xin, w_in, BF16, f"sgu_in_{i}")
            s_act = _sgu_fwd(h1, lg, lb, ws, bst, f"sgu_mix_{i}")
            sv.update(h1=h1, w_in=w_in, lg=lg, lb=lb, ws=ws, bst=bst)
        else:
            w_in = mat("c_w_in", j).reshape(1, d, d)
            wgrp = jnp.moveaxis(mat("c_w_grp", j), 0, 1).reshape(4, d // 4, d // 4)
            scale = full_cols("c_scale", j)
            yp = _mm_nn(xin, w_in, F32, f"pool_in_{i}")[0]
            s_act = _pool_fwd(yp, wgrp, scale, f"pool_mix_{i}")
            sv.update(yp=yp, w_in=w_in, wgrp=wgrp, scale=scale)
        if i == 0:
            finish_gather("layer0", s_act)
            res_g = res_g + start_gather("layer1")
        w_out = mat(("a_w_out", "b_w_out", "c_w_out")[kind], j).reshape(1, d, d)
        g1, b1 = w["ln1_g"][i].reshape(1, d), w["ln1_b"][i].reshape(1, d)
        xhat1, y1, rstd1 = _mm_res_ln(s_act.reshape(1, t, d), w_out, res, res_g, res_b, g1, b1, f"mix_out_ln_{i}")
        w_up = mat("f_w_up", i)
        fdw = gw["f_dw"][:, i]
        n_ff = w_up.shape[-1]
        w_down = mat("f_w_down", i).reshape(4, n_ff, d)
        g2, b2 = w["ln2_g"][i].reshape(1, d), w["ln2_b"][i].reshape(1, d)
        u, h_ffn, a_act, xhat2, y2, rstd2 = _ffn_fwd(y1, w_up, fdw, w_down, xhat1, g1, b1, g2, b2, f"ffn_fwd_{i}")
        sv.update(s_act=s_act, w_out=w_out, xhat1=xhat1, y1=y1, rstd1=rstd1, g1=g1, u=u, h_ffn=h_ffn, a_act=a_act,
                  w_up=w_up,
                  fdw=fdw, w_down=w_down, xhat2=xhat2, rstd2=rstd2, g2=g2, b2=b2)
        saved.append(sv)
        res, res_g, res_b, xin = xhat2, g2, b2, y2

    last = saved[-1]
    dr2, dr2b, dg2, db2, sq = _loss_head(last["xhat2"], last["rstd2"], last["g2"], last["b2"], tgt2, "loss_head")
    loss = lax.psum((0.5 / d) * jnp.sum(sq), ("x", "y", "c"))

    grads = {k: [None] * w[k].shape[0] for k in WEIGHTS}
    grad_x = None
    small_names = [k for k in WEIGHTS if k not in GATHER_BF16]
    exchanges = []

    def start_exchange(keys, tag, extra=()):
        pieces = [grads[k][l].astype(BF16).reshape(N_DEV, -1, w[k].shape[-1]) for k, l in keys]
        handles, token = _exchange_start(pieces + list(extra), f"exchange_start_{tag}")
        exchanges.append((keys, handles, tag))
        return token[0:1, 0:1]

    for i in reversed(range(DEPTH)):
        sv = saved[i]
        kind, j = sv["kind"], sv["j"]
        grads["ln2_g"][i], grads["ln2_b"][i] = dg2, db2
        da = _mm_nt_out(dr2b, sv["w_down"], f"ffn_da_{i}")
        grads["f_w_down"][i] = _to_shards(_mm_tn(sv["a_act"], dr2b.reshape(1, t, d), f"ffn_dwdown_{i}")
                                          .reshape(-1, d), 0)
        du, dfdw = _ffn_gate_bwd(sv["h_ffn"], sv["u"], da, sv["fdw"], f"ffn_gate_bwd_{i}")
        grads["f_dw"][i] = dfdw
        grads["f_w_up"][i] = _mm_tn(sv["y1"].reshape(1, t, d), du, f"ffn_dwup_{i}")
        started = start_exchange([("f_w_up", i), ("f_w_down", i)], f"ffn_{i}")
        dr1, dr1b, dg1, db1 = _mm_nt_lnb(du, sv["w_up"], dr2, sv["xhat1"], sv["rstd1"], sv["g1"], f"ffn_dx_ln_{i}")
        grads["ln1_g"][i], grads["ln1_b"][i] = dg1, db1
        ds = _mm_nt_out(dr1b, sv["w_out"], f"mix_ds_{i}")[0]
        dw_out = _to_shards(_mm_tn(sv["s_act"].reshape(1, t, d), dr1b.reshape(1, t, d), f"mix_dwout_{i}")[0], 0)
        xin3 = sv["xin"].reshape(1, t, d)
        if kind == 0:
            dh1, ddw, ddwb, dlg, dlb = _conv_bwd(ds, sv["q"], sv["h1"], sv["dw"], sv["lg"] + started, sv["lb"],
                                                  f"conv_mix_bwd_{i}")
            grads["a_w_out"][j] = dw_out
            grads["a_dw"][j] = _to_shards(jnp.sum(ddw, axis=1), 1)
            grads["a_dw_b"][j] = _to_shards(ddwb[0], 0)
            grads["a_ln_g"][j] = _to_shards(dlg[0], 0)
            grads["a_ln_b"][j] = _to_shards(dlb[0], 0)
            grads["a_w_in"][j] = _mm_tn(xin3, dh1, f"conv_dwin_{i}")
            dh_in, w_in = dh1, sv["w_in"]
            mixer_keys = [("a_w_in", j), ("a_w_out", j)]
        elif kind == 1:
            dh1, dlg, dlb, dws, dbias = _sgu_bwd(sv["h1"], ds, sv["lg"] + started, sv["lb"], sv["ws"], sv["bst"],
                                                  f"sgu_mix_bwd_{i}")
            grads["b_w_out"][j] = dw_out
            grads["b_ln_g"][j], grads["b_ln_b"][j] = dlg[0], dlb[0]
            grads["b_ws"][j] = dws
            grads["b_bs"][j] = jnp.sum(dbias.reshape(CHUNK, HEADS, CHUNK), axis=-1).T
            grads["b_w_in"][j] = _mm_tn(xin3, dh1, f"sgu_dwin_{i}")
            dh_in, w_in = dh1, sv["w_in"]
            mixer_keys = [("b_w_in", j), ("b_w_out", j)]
        else:
            dyp, dscale, dwgrp = _pool_bwd(sv["yp"], ds, sv["wgrp"], sv["scale"] + started, f"pool_mix_bwd_{i}")
            grads["c_w_out"][j] = dw_out
            grads["c_scale"][j] = _to_shards(dscale[0], 0)
            grads["c_w_grp"][j] = _to_shards(dwgrp, 1)
            dh_in, w_in = dyp.reshape(1, t, d), sv["w_in"]
            grads["c_w_in"][j] = _to_shards(_mm_tn(xin3, dh_in, f"pool_dwin_{i}")[0], 0)
            mixer_keys = [("c_w_in", j), ("c_w_grp", j), ("c_w_out", j)]
        if i > 0:
            started = start_exchange(mixer_keys, f"mixer_{i}")
            prev = saved[i - 1]
            prev["fdw"] = prev["fdw"] + started
            dr2, dr2b, dg2, db2 = _mm_nt_lnb(dh_in, w_in, dr1, prev["xhat2"], prev["rstd2"], prev["g2"],
                                              f"mix_dx_ln_{i}")
        else:
            flat = []
            for k in small_names:
                if k in REPLICATED:
                    full = jnp.stack([gk.reshape(w[k].shape[1:]) for gk in grads[k]], axis=0)
                    flat.append(jnp.broadcast_to(full.reshape(1, -1), (N_DEV, full.size)))
                else:
                    flat.append(jnp.stack(grads[k], axis=1).reshape(N_DEV, -1))
            small_pieces = _pack(flat, 8, (N_DEV,))
            pad_rows = -(-small_pieces.shape[1] // 128) * 128 - small_pieces.shape[1]
            small_pieces = jnp.pad(small_pieces, ((0, 0), (0, pad_rows), (0, 0)))
            start_exchange(mixer_keys, f"mixer_{i}", extra=[small_pieces])
            grad_x = _mm_nt_res(dh_in, w_in, dr1, "mix_dx_0").reshape(x.shape)

    kinds = ("grad", "delta", "new_m", "new_v")
    received, result = {}, {}

    def finish_exchange(group, after):
        keys, handles, tag = group
        lands = _exchange_wait(handles, after, f"exchange_wait_{tag}")
        received.update(zip(keys, lands))
        return lands

    def update(k):
        cols = w[k].shape[-1]
        bufs = _adamw([received[(k, l)] for l in range(w[k].shape[0])], w[k].reshape(-1, cols),
                      m[k].reshape(-1, cols), v[k].reshape(-1, cols), f"adamw_{k}")
        result.update({(kind, k): buf.reshape(w[k].shape) for kind, buf in zip(kinds, bufs)})

    for group in exchanges[:-1]:
        finish_exchange(group, grad_x)
    late = [k for k, _ in exchanges[-1][0]]
    for k in GATHER_BF16:
        if k not in late:
            update(k)
    small_received = finish_exchange(exchanges[-1], result[("new_v", "f_w_down")])[-1]
    for k in late:
        update(k)

    def packed(tree):
        return jnp.pad(_pack([tree[k].reshape(-1) for k in small_names], 8), ((0, pad_rows), (0, 0)))

    bufs = _adamw([small_received], packed(w), packed(m), packed(v), "adamw_small")
    shapes = [w[k].shape for k in small_names]
    for kind, buf in zip(kinds, bufs):
        result.update({(kind, k): a for k, a in zip(small_names, _unpack(buf, shapes, 8))})
    outs = [result[(kind, k)] for kind in kinds for k in WEIGHTS]
    return (loss, grad_x, *outs)


def kernel(x, a_w_in, a_dw, a_dw_b, a_ln_g, a_ln_b, a_w_out, b_w_in, b_ln_g, b_ln_b, b_ws, b_bs, b_w_out, c_w_in, c_w_grp, c_scale, c_w_out, f_w_up, f_dw, f_w_down, ln1_g, ln1_b, ln2_g, ln2_b, loss_target, m_a_w_in, m_a_dw, m_a_dw_b, m_a_ln_g, m_a_ln_b, m_a_w_out, m_b_w_in, m_b_ln_g, m_b_ln_b, m_b_ws, m_b_bs, m_b_w_out, m_c_w_in, m_c_w_grp, m_c_scale, m_c_w_out, m_f_w_up, m_f_dw, m_f_w_down, m_ln1_g, m_ln1_b, m_ln2_g, m_ln2_b, v_a_w_in, v_a_dw, v_a_dw_b, v_a_ln_g, v_a_ln_b, v_a_w_out, v_b_w_in, v_b_ln_g, v_b_ln_b, v_b_ws, v_b_bs, v_b_w_out, v_c_w_in, v_c_w_grp, v_c_scale, v_c_w_out, v_f_w_up, v_f_dw, v_f_w_down, v_ln1_g, v_ln1_b, v_ln2_g, v_ln2_b):
    w = dict(zip(WEIGHTS, (a_w_in, a_dw, a_dw_b, a_ln_g, a_ln_b, a_w_out, b_w_in, b_ln_g, b_ln_b, b_ws, b_bs, b_w_out,
                           c_w_in, c_w_grp, c_scale, c_w_out, f_w_up, f_dw, f_w_down, ln1_g, ln1_b, ln2_g, ln2_b)))
    m = dict(zip(WEIGHTS, (m_a_w_in, m_a_dw, m_a_dw_b, m_a_ln_g, m_a_ln_b, m_a_w_out, m_b_w_in, m_b_ln_g, m_b_ln_b,
                           m_b_ws, m_b_bs, m_b_w_out, m_c_w_in, m_c_w_grp, m_c_scale, m_c_w_out, m_f_w_up, m_f_dw,
                           m_f_w_down, m_ln1_g, m_ln1_b, m_ln2_g, m_ln2_b)))
    v = dict(zip(WEIGHTS, (v_a_w_in, v_a_dw, v_a_dw_b, v_a_ln_g, v_a_ln_b, v_a_w_out, v_b_w_in, v_b_ln_g, v_b_ln_b,
                           v_b_ws, v_b_bs, v_b_w_out, v_c_w_in, v_c_w_grp, v_c_scale, v_c_w_out, v_f_w_up, v_f_dw,
                           v_f_w_down, v_ln1_g, v_ln1_b, v_ln2_g, v_ln2_b)))
    return _step(x, loss_target, w, m, v)
```

```python
import functools

import jax
import jax.numpy as jnp
from jax import lax
from jax.experimental import pallas as pl
from jax.experimental.pallas import tpu as pltpu

N_DEV = 8
DEPTH = 4
ALPHA = float((2 * DEPTH) ** 0.25)
LN_EPS = 1e-5
CONV_W = 31
CONV_HALO = 32
FFN_HALO = 16
POOL_WINDOWS = (2, 4, 8, 16)
CHUNK = 128
HEADS = 8
LANES = 1024
ADAM_LR, ADAM_B1, ADAM_B2, ADAM_EPS, ADAM_WD, ADAM_STEP = 0.001, 0.9, 0.999, 1e-08, 0.01, 10
VMEM_LIMIT = 56 * 1024 * 1024
F32, BF16 = jnp.float32, jnp.bfloat16
MESH = pl.DeviceIdType.MESH

WEIGHTS = ['a_w_in', 'a_dw', 'a_dw_b', 'a_ln_g', 'a_ln_b', 'a_w_out', 'b_w_in', 'b_ln_g', 'b_ln_b', 'b_ws', 'b_bs',
           'b_w_out', 'c_w_in', 'c_w_grp', 'c_scale', 'c_w_out', 'f_w_up', 'f_dw', 'f_w_down', 'ln1_g', 'ln1_b',
           'ln2_g', 'ln2_b']
REPLICATED = ('b_ln_g', 'b_ln_b', 'b_ws', 'b_bs', 'ln1_g', 'ln1_b', 'ln2_g', 'ln2_b')
GATHER_BF16 = ('a_w_in', 'a_w_out', 'b_w_in', 'b_w_out', 'c_w_in', 'c_w_grp', 'c_w_out', 'f_w_up', 'f_w_down')
GATHER_F32 = ('a_dw', 'a_dw_b', 'a_ln_g', 'a_ln_b', 'c_scale', 'f_dw')


def _params(n_axes):
    return pltpu.CompilerParams(dimension_semantics=("arbitrary",) * n_axes, vmem_limit_bytes=VMEM_LIMIT)


def _resident(shape):
    zeros = (0,) * len(shape)
    return pl.BlockSpec(shape, lambda i: zeros, pipeline_mode=pl.Buffered(1))


def _sigmoid(x):
    return 1.0 / (1.0 + jnp.exp(-x))


def _gelu(x):
    return 0.5 * x * (1.0 + lax.erf(x * 0.7071067811865476))


def _gelu_grad(x):
    return 0.5 * (1.0 + lax.erf(x * 0.7071067811865476)) + x * jnp.exp(-0.5 * x * x) * 0.3989422804014327


def _ln_stats(r):
    mu = jnp.mean(r, axis=-1, keepdims=True)
    xc = r - mu
    var = jnp.mean(xc * xc, axis=-1, keepdims=True)
    rstd = lax.rsqrt(var + LN_EPS)
    return xc * rstd, rstd


def _ln_bwd(dy, xhat, rstd, g):
    dxhat = dy * g
    m1 = jnp.mean(dxhat, axis=-1, keepdims=True)
    m2 = jnp.mean(dxhat * xhat, axis=-1, keepdims=True)
    dr = rstd * (dxhat - m1 - xhat * m2)
    return dr, jnp.sum(dy * xhat, axis=0, keepdims=True), jnp.sum(dy, axis=0, keepdims=True)


def _accumulate(ref, value, first):
    @pl.when(first)
    def _():
        ref[...] = value

    @pl.when(jnp.logical_not(first))
    def _():
        ref[...] += value


def _dot(a, b):
    return jnp.dot(a, b, preferred_element_type=F32)


def _dot_nt(a, b):
    return lax.dot_general(a, b, (((1,), (1,)), ((), ())), preferred_element_type=F32)


def _dot_tn(a, b):
    return lax.dot_general(a, b, (((0,), (0,)), ((), ())), preferred_element_type=F32)


def _cat_lanes(h, lo, hi):
    return jnp.concatenate([h[s] for s in range(lo, hi)], axis=-1)


def _mm_nn(x, w, out_dtype, name, tm=512):
    t, k = x.shape
    s_n, _, n = w.shape
    tm = min(tm, t)

    def body(x_ref, w_ref, o_ref):
        x_tile = x_ref[...]
        for s in range(s_n):
            o_ref[s] = _dot(x_tile, w_ref[s]).astype(o_ref.dtype)

    return pl.pallas_call(
        body, out_shape=jax.ShapeDtypeStruct((s_n, t, n), out_dtype), grid=(t // tm,),
        in_specs=[pl.BlockSpec((tm, k), lambda i: (i, 0)), _resident((s_n, k, n))],
        out_specs=pl.BlockSpec((s_n, tm, n), lambda i: (0, i, 0)),
        name=name, compiler_params=_params(1))(x, w)


def _mm_res_ln(a, w, res, gp, bp, g, b, name, tm=512):
    s_n, t, ka = a.shape
    d = w.shape[-1]
    tm = min(tm, t)

    def body(a_ref, w_ref, res_ref, gp_ref, bp_ref, g_ref, b_ref, xhat_ref, y_ref, rstd_ref):
        acc = _dot(a_ref[0], w_ref[0])
        for s in range(1, s_n):
            acc += _dot(a_ref[s], w_ref[s])
        r = ALPHA * (res_ref[...] * gp_ref[...] + bp_ref[...]) + acc
        xhat, rstd = _ln_stats(r)
        xhat_ref[...] = xhat
        y_ref[...] = (xhat * g_ref[...] + b_ref[...]).astype(BF16)
        rstd_ref[...] = rstd

    row = pl.BlockSpec((1, d), lambda i: (0, 0))
    tile = pl.BlockSpec((tm, d), lambda i: (i, 0))
    return pl.pallas_call(
        body,
        out_shape=(jax.ShapeDtypeStruct((t, d), F32), jax.ShapeDtypeStruct((t, d), BF16),
                   jax.ShapeDtypeStruct((t, 1), F32)),
        grid=(t // tm,),
        in_specs=[pl.BlockSpec((s_n, tm, ka), lambda i: (0, i, 0)), _resident((s_n, ka, d)),
                  tile, row, row, row, row],
        out_specs=(tile, tile, pl.BlockSpec((tm, 1), lambda i: (i, 0))),
        name=name, compiler_params=_params(1))(a, w, res, gp, bp, g, b)


def _mm_nt_out(x, w, name, tm=512):
    t, n = x.shape
    s_n, k, _ = w.shape
    tm = min(tm, t)

    def body(x_ref, w_ref, o_ref):
        x_tile = x_ref[...]
        for s in range(s_n):
            o_ref[s] = _dot_nt(x_tile, w_ref[s]).astype(o_ref.dtype)

    return pl.pallas_call(
        body, out_shape=jax.ShapeDtypeStruct((s_n, t, k), BF16), grid=(t // tm,),
        in_specs=[pl.BlockSpec((tm, n), lambda i: (i, 0)), _resident((s_n, k, n))],
        out_specs=pl.BlockSpec((s_n, tm, k), lambda i: (0, i, 0)),
        name=name, compiler_params=_params(1))(x, w)


def _mm_nt_lnb(dh, w, drn, xhat, rstd, g, name, tm=512):
    s_n, t, n = dh.shape
    k = w.shape[1]
    tm = min(tm, t)

    def body(dh_ref, w_ref, drn_ref, xhat_ref, rstd_ref, g_ref, dr_ref, drb_ref, dg_ref, db_ref):
        first = pl.program_id(0) == 0
        acc = _dot_nt(dh_ref[0], w_ref[0])
        for s in range(1, s_n):
            acc += _dot_nt(dh_ref[s], w_ref[s])
        dy = acc + ALPHA * drn_ref[...]
        dr, dg, db = _ln_bwd(dy, xhat_ref[...], rstd_ref[...], g_ref[...])
        dr_ref[...] = dr
        drb_ref[...] = dr.astype(BF16)
        _accumulate(dg_ref, dg, first)
        _accumulate(db_ref, db, first)

    tile = pl.BlockSpec((tm, k), lambda i: (i, 0))
    row = pl.BlockSpec((1, k), lambda i: (0, 0))
    return pl.pallas_call(
        body,
        out_shape=(jax.ShapeDtypeStruct((t, k), F32), jax.ShapeDtypeStruct((t, k), BF16),
                   jax.ShapeDtypeStruct((1, k), F32), jax.ShapeDtypeStruct((1, k), F32)),
        grid=(t // tm,),
        in_specs=[pl.BlockSpec((s_n, tm, n), lambda i: (0, i, 0)), _resident((s_n, k, n)),
                  tile, tile, pl.BlockSpec((tm, 1), lambda i: (i, 0)), row],
        out_specs=(tile, tile, row, row),
        name=name, compiler_params=_params(1))(dh, w, drn, xhat, rstd, g)


def _mm_nt_res(dh, w, drn, name, tm=512):
    s_n, t, n = dh.shape
    k = w.shape[1]
    tm = min(tm, t)

    def body(dh_ref, w_ref, drn_ref, o_ref):
        acc = _dot_nt(dh_ref[0], w_ref[0])
        for s in range(1, s_n):
            acc += _dot_nt(dh_ref[s], w_ref[s])
        o_ref[...] = acc + ALPHA * drn_ref[...]

    tile = pl.BlockSpec((tm, k), lambda i: (i, 0))
    return pl.pallas_call(
        body, out_shape=jax.ShapeDtypeStruct((t, k), F32), grid=(t // tm,),
        in_specs=[pl.BlockSpec((s_n, tm, n), lambda i: (0, i, 0)), _resident((s_n, k, n)), tile],
        out_specs=tile, name=name, compiler_params=_params(1))(dh, w, drn)


def _mm_tn(lhs, rhs, name, tm=2048):
    sl, t, kl = lhs.shape
    sr, _, n = rhs.shape
    s_n = max(sl, sr)
    tm = min(tm, t)

    def body(l_ref, r_ref, o_ref):
        _accumulate(o_ref, _dot_tn(l_ref[...], r_ref[...]), pl.program_id(1) == 0)

    return pl.pallas_call(
        body, out_shape=jax.ShapeDtypeStruct((s_n, kl, n), F32), grid=(s_n, t // tm),
        in_specs=[pl.BlockSpec((None, tm, kl), (lambda s, i: (s, i, 0)) if sl > 1 else (lambda s, i: (0, i, 0))),
                  pl.BlockSpec((None, tm, n), (lambda s, i: (s, i, 0)) if sr > 1 else (lambda s, i: (0, i, 0)))],
        out_specs=pl.BlockSpec((None, kl, n), lambda s, i: (s, 0, 0)),
        name=name, compiler_params=_params(2))(lhs, rhs)


def _glu(h):
    return _cat_lanes(h, 0, 4).astype(F32) * _sigmoid(_cat_lanes(h, 4, 8).astype(F32))


def _shifted_windows(src_ref, r0, c0, row_block, offsets):
    span = row_block + CONV_HALO
    big = src_ref[pl.ds(r0, span), pl.ds(c0, 128)]
    for sub in range(8):
        taps = [k for k, o in enumerate(offsets) if o % 8 == sub]
        if not taps:
            continue
        rolled = big if sub == 0 else pltpu.roll(big, span - sub, 0)
        for k in taps:
            lo = offsets[k] - sub
            yield k, rolled[lo:lo + row_block]


def _tap_loop(src_ref, dst_ref, weight_ref, rows, offsets, weight_rows, row_block=64):
    d = dst_ref.shape[-1]

    def block(cb, carry):
        c0 = pl.multiple_of(cb * 128, 128)
        for r0 in range(0, rows, row_block):
            acc = jnp.zeros((row_block, 128), F32)
            for k, window in _shifted_windows(src_ref, r0, c0, row_block, offsets):
                acc += weight_ref[pl.ds(weight_rows[k], 1), pl.ds(c0, 128)] * window
            dst_ref[pl.ds(r0, row_block), pl.ds(c0, 128)] = acc
        return carry

    lax.fori_loop(0, d // 128, block, 0)


def _conv_fwd(h1, dw, dwb, g, b, name, tm=256):
    _, t, _ = h1.shape
    d = dw.shape[-1]
    tm = min(tm, t)
    hb = tm // CONV_HALO

    def body(h_ref, halo_ref, dw_ref, dwb_ref, g_ref, b_ref, s_ref, q_ref, ext_ref):
        i = pl.program_id(0)
        ext_ref[pl.ds(0, CONV_HALO), :] = _glu(halo_ref[...]) * (i > 0).astype(F32)
        ext_ref[pl.ds(CONV_HALO, tm), :] = _glu(h_ref[...])
        _tap_loop(ext_ref, q_ref, dw_ref, tm, [2 + k for k in range(CONV_W)], list(range(CONV_W)))
        q = q_ref[...] + dwb_ref[...]
        q_ref[...] = q
        qhat, _ = _ln_stats(q)
        z = qhat * g_ref[...] + b_ref[...]
        s_ref[...] = (z * _sigmoid(z)).astype(BF16)

    row = pl.BlockSpec((1, d), lambda i: (0, 0))
    tile = pl.BlockSpec((tm, d), lambda i: (i, 0))
    return pl.pallas_call(
        body, out_shape=(jax.ShapeDtypeStruct((t, d), BF16), jax.ShapeDtypeStruct((t, d), F32)), grid=(t // tm,),
        in_specs=[pl.BlockSpec((8, tm, 256), lambda i: (0, i, 0)),
                  pl.BlockSpec((8, CONV_HALO, 256), lambda i: (0, jnp.maximum(i * hb - 1, 0), 0)),
                  pl.BlockSpec((CONV_W, d), lambda i: (0, 0)), row, row, row],
        out_specs=(tile, tile), scratch_shapes=[pltpu.VMEM((tm + CONV_HALO, d), F32)],
        name=name, compiler_params=_params(1))(h1, h1, dw, dwb, g, b)


def _conv_bwd(ds, q, h1, dw, g, b, name, tm=256):
    _, t, _ = h1.shape
    d = dw.shape[-1]
    tm = min(tm, t)
    hb = tm // CONV_HALO
    n_t = t // tm
    last_halo = t // CONV_HALO - 1

    def body(ds_ref, dsn_ref, q_ref, qn_ref, h_ref, hp_ref, dw_ref, g_ref, b_ref,
             dh_ref, ddw_ref, ddwb_ref, dg_ref, db_ref, dq_ref, p_ref, dp_ref):
        i = pl.program_id(0)
        first = i == 0
        valid = (i < n_t - 1).astype(F32)

        def dq_rows(ds_rows, q_rows, scale):
            qhat, rstd = _ln_stats(q_rows)
            z = qhat * g_ref[...] + b_ref[...]
            sg = _sigmoid(z)
            dz = ds_rows.astype(F32) * (sg * (1.0 + z * (1.0 - sg))) * scale
            dq, dg, db = _ln_bwd(dz, qhat, rstd, g_ref[...])
            return dq, dg, db

        dq, dg, db = dq_rows(ds_ref[...], q_ref[...], 1.0)
        dq_ref[pl.ds(0, tm), :] = dq
        dq_ref[pl.ds(tm, CONV_HALO), :] = dq_rows(dsn_ref[...], qn_ref[...], valid)[0]
        _accumulate(dg_ref, dg, first)
        _accumulate(db_ref, db, first)
        _accumulate(ddwb_ref, jnp.sum(dq, axis=0, keepdims=True), first)

        p_ref[pl.ds(0, CONV_HALO), :] = _glu(hp_ref[...]) * (i > 0).astype(F32)
        p_ref[pl.ds(CONV_HALO, tm), :] = _glu(h_ref[...])

        _tap_loop(dq_ref, dp_ref, dw_ref, tm, list(range(CONV_W)), [CONV_W - 1 - o for o in range(CONV_W)])

        @pl.when(first)
        def _():
            ddw_ref[...] = jnp.zeros_like(ddw_ref)

        row_block = 64

        def block(cb, carry):
            c0 = pl.multiple_of(cb * 128, 128)
            for r0 in range(0, tm, row_block):
                dqb = dq_ref[pl.ds(r0, row_block), pl.ds(c0, 128)]
                for k, window in _shifted_windows(p_ref, r0, c0, row_block, [2 + k for k in range(CONV_W)]):
                    prod = dqb * window
                    ddw_ref[k, :, pl.ds(c0, 128)] += jnp.sum(prod.reshape(row_block // 8, 8, 128), axis=0)
            return carry

        lax.fori_loop(0, d // 128, block, 0)

        h = h_ref[...]
        a = _cat_lanes(h, 0, 4).astype(F32)
        sg = _sigmoid(_cat_lanes(h, 4, 8).astype(F32))
        dp = dp_ref[...]
        da = (dp * sg).astype(BF16)
        dgate = (dp * a * sg * (1.0 - sg)).astype(BF16)
        for s in range(4):
            dh_ref[s] = da[:, s * 256:(s + 1) * 256]
            dh_ref[4 + s] = dgate[:, s * 256:(s + 1) * 256]

    row = pl.BlockSpec((1, d), lambda i: (0, 0))
    tile = pl.BlockSpec((tm, d), lambda i: (i, 0))
    nxt = pl.BlockSpec((CONV_HALO, d), lambda i: (jnp.minimum((i + 1) * hb, last_halo), 0))
    return pl.pallas_call(
        body,
        out_shape=(jax.ShapeDtypeStruct((8, t, 256), BF16), jax.ShapeDtypeStruct((CONV_W, 8, d), F32),
                   jax.ShapeDtypeStruct((1, d), F32), jax.ShapeDtypeStruct((1, d), F32),
                   jax.ShapeDtypeStruct((1, d), F32)),
        grid=(n_t,),
        in_specs=[tile, nxt, tile, nxt,
                  pl.BlockSpec((8, tm, 256), lambda i: (0, i, 0)),
                  pl.BlockSpec((8, CONV_HALO, 256), lambda i: (0, jnp.maximum(i * hb - 1, 0), 0)),
                  pl.BlockSpec((CONV_W, d), lambda i: (0, 0)), row, row],
        out_specs=(pl.BlockSpec((8, tm, 256), lambda i: (0, i, 0)),
                   pl.BlockSpec((CONV_W, 8, d), lambda i: (0, 0, 0)), row, row, row),
        scratch_shapes=[pltpu.VMEM((tm + CONV_HALO, d), F32), pltpu.VMEM((tm + CONV_HALO, d), F32),
                        pltpu.VMEM((tm, d), F32)],
        name=name, compiler_params=_params(1))(ds, ds, q, q, h1, h1, dw, g, b)


def _conv3(ext, dw):
    e1 = pltpu.roll(ext, 1, 0)
    e2 = pltpu.roll(ext, 2, 0)
    return dw[2:3] * ext + dw[1:2] * e1 + dw[0:1] * e2, e1, e2


def _ffn_fwd(x, w_up, fdw, w_down, res, gp, bp, g, b, name, tm=256):
    t, d = x.shape
    _, _, n = w_up.shape
    tm = min(tm, t)
    carry_rows = 8

    def body(x_ref, wu_ref, dw_ref, wd_ref, res_ref, gp_ref, bp_ref, g_ref, b_ref,
             u_ref, h_ref, a_ref, xhat_ref, y_ref, rstd_ref, carry_ref):
        @pl.when(pl.program_id(0) == 0)
        def _():
            carry_ref[...] = jnp.zeros_like(carry_ref)

        x_tile = x_ref[...]
        acc = None
        for j in range(4):
            h = []
            for s in (j, 4 + j):
                ub = _dot(x_tile, wu_ref[s]).astype(BF16)
                u_ref[s] = ub
                uf = ub.astype(F32)
                ext = jnp.concatenate([carry_ref[s], uf], axis=0)
                carry_ref[s] = uf[tm - carry_rows:]
                hb = _conv3(ext, dw_ref[s])[0][carry_rows:].astype(BF16)
                h_ref[s] = hb
                h.append(hb.astype(F32))
            a = (h[0] * _sigmoid(h[0]) * h[1]).astype(BF16)
            a_ref[j] = a
            part = _dot(a, wd_ref[j])
            acc = part if acc is None else acc + part
        r = ALPHA * (res_ref[...] * gp_ref[...] + bp_ref[...]) + acc
        xhat, rstd = _ln_stats(r)
        xhat_ref[...] = xhat
        y_ref[...] = (xhat * g_ref[...] + b_ref[...]).astype(BF16)
        rstd_ref[...] = rstd

    row = pl.BlockSpec((1, d), lambda i: (0, 0))
    tile = pl.BlockSpec((tm, d), lambda i: (i, 0))
    return pl.pallas_call(
        body,
        out_shape=(jax.ShapeDtypeStruct((8, t, n), BF16), jax.ShapeDtypeStruct((8, t, n), BF16),
                   jax.ShapeDtypeStruct((4, t, n), BF16), jax.ShapeDtypeStruct((t, d), F32),
                   jax.ShapeDtypeStruct((t, d), BF16), jax.ShapeDtypeStruct((t, 1), F32)),
        grid=(t // tm,),
        in_specs=[tile, _resident((8, d, n)), pl.BlockSpec((8, 3, n), lambda i: (0, 0, 0)), _resident((4, n, d)),
                  tile, row, row, row, row],
        out_specs=(pl.BlockSpec((8, tm, n), lambda i: (0, i, 0)), pl.BlockSpec((8, tm, n), lambda i: (0, i, 0)),
                   pl.BlockSpec((4, tm, n), lambda i: (0, i, 0)),
                   tile, tile, pl.BlockSpec((tm, 1), lambda i: (i, 0))),
        scratch_shapes=[pltpu.VMEM((8, carry_rows, n), F32)],
        name=name, compiler_params=_params(1))(x, w_up, fdw, w_down, res, gp, bp, g, b)


def _ffn_gate_bwd(h, u, da, fdw, name, tm=256):
    _, t, n = u.shape
    tm = min(tm, t)
    hb = tm // FFN_HALO
    n_t = t // tm
    last_halo = t // FFN_HALO - 1
    h4, u4 = h.reshape(2, 4, t, n), u.reshape(2, 4, t, n)
    fdw4 = fdw.reshape(2, 4, 3, n)
    rows = tm + FFN_HALO

    def body(h_ref, hn_ref, u_ref, da_ref, dan_ref, dw_ref, du_ref, ddw_ref):
        i = pl.program_id(1)
        keep_next = (i < n_t - 1).astype(F32)
        hg = jnp.concatenate([h_ref[0], hn_ref[0]], axis=0).astype(F32)
        hv = jnp.concatenate([h_ref[1], hn_ref[1]], axis=0).astype(F32)
        da_ext = jnp.concatenate([da_ref[...].astype(F32), dan_ref[...].astype(F32) * keep_next], axis=0)
        sg = _sigmoid(hg)
        silu = hg * sg
        dh = (da_ext * hv * (sg + silu * (1.0 - sg)), da_ext * silu)
        for p in range(2):
            dwp = dw_ref[p]
            d1 = pltpu.roll(dh[p], rows - 1, 0)
            d2 = pltpu.roll(dh[p], rows - 2, 0)
            du_ref[p] = (dwp[2:3] * dh[p] + dwp[1:2] * d1 + dwp[0:1] * d2)[:tm].astype(BF16)
            up = u_ref[p].astype(F32)
            part = jnp.concatenate([jnp.sum(d[:tm] * up, axis=0, keepdims=True) for d in (d2, d1, dh[p])], axis=0)
            _accumulate(ddw_ref.at[p], part, i == 0)

    tile = pl.BlockSpec((2, None, tm, n), lambda j, i: (0, j, i, 0))
    nxt = pl.BlockSpec((2, None, FFN_HALO, n), lambda j, i: (0, j, jnp.minimum((i + 1) * hb, last_halo), 0))
    du, ddw = pl.pallas_call(
        body, out_shape=(jax.ShapeDtypeStruct((2, 4, t, n), BF16), jax.ShapeDtypeStruct((2, 4, 3, n), F32)),
        grid=(4, n_t),
        in_specs=[tile, nxt, tile,
                  pl.BlockSpec((None, tm, n), lambda j, i: (j, i, 0)),
                  pl.BlockSpec((None, FFN_HALO, n), lambda j, i: (j, jnp.minimum((i + 1) * hb, last_halo), 0)),
                  pl.BlockSpec((2, None, 3, n), lambda j, i: (0, j, 0, 0))],
        out_specs=(tile, pl.BlockSpec((2, None, 3, n), lambda j, i: (0, j, 0, 0))),
        name=name, compiler_params=_params(2))(h4, h4, u4, da, da, fdw4)
    return du.reshape(8, t, n), ddw.reshape(8, 3, n)


def _tril_mask():
    r = lax.broadcasted_iota(jnp.int32, (CHUNK, CHUNK), 0)
    c = lax.broadcasted_iota(jnp.int32, (CHUNK, CHUNK), 1)
    return (r >= c).astype(F32)


def _sgu_fwd(h1, g, b, ws, bst, name, tm=256):
    _, t, _ = h1.shape
    d = g.shape[-1]
    tm = min(tm, t)

    def body(h_ref, g_ref, b_ref, ws_ref, bst_ref, m_ref):
        h = h_ref[...]
        u = _gelu(_cat_lanes(h, 0, 4).astype(F32))
        v = _gelu(_cat_lanes(h, 4, 8).astype(F32))
        vn = (_ln_stats(v)[0] * g_ref[...] + b_ref[...]).astype(BF16)
        mask = _tril_mask()
        for hh in range(HEADS):
            cols = slice(hh * CHUNK, (hh + 1) * CHUNK)
            wm = (ws_ref[hh] * mask).astype(BF16)
            bias = bst_ref[:, hh:hh + 1]
            for c in range(tm // CHUNK):
                rows = slice(c * CHUNK, (c + 1) * CHUNK)
                sblk = _dot(wm, vn[rows, cols]) + bias
                m_ref[rows, cols] = (u[rows, cols] * sblk).astype(BF16)

    row = pl.BlockSpec((1, d), lambda i: (0, 0))
    return pl.pallas_call(
        body, out_shape=jax.ShapeDtypeStruct((t, d), BF16), grid=(t // tm,),
        in_specs=[pl.BlockSpec((8, tm, 256), lambda i: (0, i, 0)), row, row,
                  pl.BlockSpec((HEADS, CHUNK, CHUNK), lambda i: (0, 0, 0)),
                  pl.BlockSpec((CHUNK, HEADS), lambda i: (0, 0))],
        out_specs=pl.BlockSpec((tm, d), lambda i: (i, 0)),
        name=name, compiler_params=_params(1))(h1, g, b, ws, bst)


def _sgu_bwd(h1, dm, g, b, ws, bst, name, tm=256):
    _, t, _ = h1.shape
    d = g.shape[-1]
    tm = min(tm, t)

    def body(h_ref, dm_ref, g_ref, b_ref, ws_ref, bst_ref, dh_ref, dg_ref, db_ref, dws_ref, dbias_ref,
             du_ref, dvn_ref):
        first = pl.program_id(0) == 0
        h = h_ref[...]
        zu = _cat_lanes(h, 0, 4).astype(F32)
        zv = _cat_lanes(h, 4, 8).astype(F32)
        u = _gelu(zu)
        vhat, rstd = _ln_stats(_gelu(zv))
        vn = (vhat * g_ref[...] + b_ref[...]).astype(BF16)
        dm = dm_ref[...].astype(F32)
        mask = _tril_mask()

        @pl.when(first)
        def _():
            dws_ref[...] = jnp.zeros_like(dws_ref)
            dbias_ref[...] = jnp.zeros_like(dbias_ref)

        for hh in range(HEADS):
            cols = slice(hh * CHUNK, (hh + 1) * CHUNK)
            wm = (ws_ref[hh] * mask).astype(BF16)
            bias = bst_ref[:, hh:hh + 1]
            for c in range(tm // CHUNK):
                rows = slice(c * CHUNK, (c + 1) * CHUNK)
                vb = vn[rows, cols]
                sblk = _dot(wm, vb) + bias
                dmb = dm[rows, cols]
                du_ref[rows, cols] = dmb * sblk
                dsb = dmb * u[rows, cols]
                dbias_ref[:, cols] += dsb
                dsb16 = dsb.astype(BF16)
                dws_ref[hh] += _dot_nt(dsb16, vb) * mask
                dvn_ref[rows, cols] = _dot_tn(wm, dsb16)

        dvn = dvn_ref[...]
        dv, dg, db = _ln_bwd(dvn, vhat, rstd, g_ref[...])
        _accumulate(dg_ref, dg, first)
        _accumulate(db_ref, db, first)
        dzu = (du_ref[...] * _gelu_grad(zu)).astype(BF16)
        dzv = (dv * _gelu_grad(zv)).astype(BF16)
        for s in range(4):
            dh_ref[s] = dzu[:, s * 256:(s + 1) * 256]
            dh_ref[4 + s] = dzv[:, s * 256:(s + 1) * 256]

    row = pl.BlockSpec((1, d), lambda i: (0, 0))
    tile = pl.BlockSpec((tm, d), lambda i: (i, 0))
    h_tile = pl.BlockSpec((8, tm, 256), lambda i: (0, i, 0))
    return pl.pallas_call(
        body,
        out_shape=(jax.ShapeDtypeStruct((8, t, 256), BF16), jax.ShapeDtypeStruct((1, d), F32),
                   jax.ShapeDtypeStruct((1, d), F32), jax.ShapeDtypeStruct((HEADS, CHUNK, CHUNK), F32),
                   jax.ShapeDtypeStruct((CHUNK, d), F32)),
        grid=(t // tm,),
        in_specs=[h_tile, tile, row, row, pl.BlockSpec((HEADS, CHUNK, CHUNK), lambda i: (0, 0, 0)),
                  pl.BlockSpec((CHUNK, HEADS), lambda i: (0, 0))],
        out_specs=(h_tile, row, row, pl.BlockSpec((HEADS, CHUNK, CHUNK), lambda i: (0, 0, 0)),
                   pl.BlockSpec((CHUNK, d), lambda i: (0, 0))),
        scratch_shapes=[pltpu.VMEM((tm, d), F32), pltpu.VMEM((tm, d), F32)],
        name=name, compiler_params=_params(1))(h1, dm, g, b, ws, bst)


def _pool_minus_self(ext, first_token, grp):
    s = ext
    for step in range(grp + 1):
        s = s + pltpu.roll(s, 1 << step, 0)
    rows = ext.shape[0] - FFN_HALO
    tok = first_token + lax.broadcasted_iota(jnp.int32, (rows, 1), 0)
    count = jnp.minimum(tok + 1, POOL_WINDOWS[grp]).astype(F32)
    return s[FFN_HALO:] / count - ext[FFN_HALO:]


def _pool_fwd(y, wgrp, scale, name, tm=256):
    t, d = y.shape
    tm = min(tm, t)
    hb = tm // FFN_HALO
    gd = d // len(POOL_WINDOWS)

    def body(y_ref, yp_ref, w_ref, sc_ref, z_ref):
        i = pl.program_id(0)
        ext = jnp.concatenate([yp_ref[...] * (i > 0).astype(F32), y_ref[...]], axis=0)
        for grp in range(len(POOL_WINDOWS)):
            cols = slice(grp * gd, (grp + 1) * gd)
            p = _pool_minus_self(ext[:, cols], i * tm, grp)
            z_ref[:, cols] = (_dot(p.astype(BF16), w_ref[grp]) * sc_ref[:, cols]).astype(BF16)

    return pl.pallas_call(
        body, out_shape=jax.ShapeDtypeStruct((t, d), BF16), grid=(t // tm,),
        in_specs=[pl.BlockSpec((tm, d), lambda i: (i, 0)),
                  pl.BlockSpec((FFN_HALO, d), lambda i: (jnp.maximum(i * hb - 1, 0), 0)),
                  pl.BlockSpec((len(POOL_WINDOWS), gd, gd), lambda i: (0, 0, 0)),
                  pl.BlockSpec((1, d), lambda i: (0, 0))],
        out_specs=pl.BlockSpec((tm, d), lambda i: (i, 0)),
        name=name, compiler_params=_params(1))(y, y, wgrp, scale)


def _pool_bwd(y, dz, wgrp, scale, name, tm=256):
    t, d = y.shape
    tm = min(tm, t)
    hb = tm // FFN_HALO
    n_t = t // tm
    last_halo = t // FFN_HALO - 1
    gd = d // len(POOL_WINDOWS)
    rows = tm + FFN_HALO

    def body(y_ref, yp_ref, dz_ref, dzn_ref, w_ref, sc_ref, dy_ref, dsc_ref, dw_ref):
        i = pl.program_id(0)
        first = i == 0
        ext = jnp.concatenate([yp_ref[...] * (i > 0).astype(F32), y_ref[...]], axis=0)
        dz_ext = jnp.concatenate([dz_ref[...].astype(F32), dzn_ref[...].astype(F32) * (i < n_t - 1).astype(F32)],
                                 axis=0)
        tok = i * tm + lax.broadcasted_iota(jnp.int32, (rows, 1), 0)
        dsc = []
        for grp in range(len(POOL_WINDOWS)):
            cols = slice(grp * gd, (grp + 1) * gd)
            p16 = _pool_minus_self(ext[:, cols], i * tm, grp).astype(BF16)
            zg = _dot(p16, w_ref[grp])
            dsc.append(jnp.sum(dz_ext[:tm, cols] * zg, axis=0, keepdims=True))
            dzg = (dz_ext[:, cols] * sc_ref[:, cols]).astype(BF16)
            _accumulate(dw_ref.at[grp], _dot_tn(p16, dzg[:tm]), first)
            dp = _dot_nt(dzg, w_ref[grp])
            s = dp / jnp.minimum(tok + 1, POOL_WINDOWS[grp]).astype(F32)
            for step in range(grp + 1):
                s = s + pltpu.roll(s, rows - (1 << step), 0)
            dy_ref[:, cols] = (s[:tm] - dp[:tm]).astype(BF16)
        _accumulate(dsc_ref, jnp.concatenate(dsc, axis=-1), first)

    tile = pl.BlockSpec((tm, d), lambda i: (i, 0))
    return pl.pallas_call(
        body,
        out_shape=(jax.ShapeDtypeStruct((t, d), BF16), jax.ShapeDtypeStruct((1, d), F32),
                   jax.ShapeDtypeStruct((len(POOL_WINDOWS), gd, gd), F32)),
        grid=(n_t,),
        in_specs=[tile, pl.BlockSpec((FFN_HALO, d), lambda i: (jnp.maximum(i * hb - 1, 0), 0)),
                  tile, pl.BlockSpec((FFN_HALO, d), lambda i: (jnp.minimum((i + 1) * hb, last_halo), 0)),
                  pl.BlockSpec((len(POOL_WINDOWS), gd, gd), lambda i: (0, 0, 0)),
                  pl.BlockSpec((1, d), lambda i: (0, 0))],
        out_specs=(tile, pl.BlockSpec((1, d), lambda i: (0, 0)),
                   pl.BlockSpec((len(POOL_WINDOWS), gd, gd), lambda i: (0, 0, 0))),
        name=name, compiler_params=_params(1))(y, y, dz, dz, wgrp, scale)


def _loss_head(xhat, rstd, g, b, target, name, tm=512):
    t, d = xhat.shape
    tm = min(tm, t)

    def body(xhat_ref, rstd_ref, g_ref, b_ref, tgt_ref, dr_ref, drb_ref, dg_ref, db_ref, sq_ref):
        first = pl.program_id(0) == 0
        xhat_t = xhat_ref[...]
        diff = xhat_t * g_ref[...] + b_ref[...] - tgt_ref[...]
        dr, dg, db = _ln_bwd(diff * (1.0 / d), xhat_t, rstd_ref[...], g_ref[...])
        dr_ref[...] = dr
        drb_ref[...] = dr.astype(BF16)
        _accumulate(dg_ref, dg, first)
        _accumulate(db_ref, db, first)
        _accumulate(sq_ref, jnp.sum(diff * diff, axis=0, keepdims=True), first)

    row = pl.BlockSpec((1, d), lambda i: (0, 0))
    tile = pl.BlockSpec((tm, d), lambda i: (i, 0))
    return pl.pallas_call(
        body,
        out_shape=(jax.ShapeDtypeStruct((t, d), F32), jax.ShapeDtypeStruct((t, d), BF16),
                   jax.ShapeDtypeStruct((1, d), F32), jax.ShapeDtypeStruct((1, d), F32),
                   jax.ShapeDtypeStruct((1, d), F32)),
        grid=(t // tm,),
        in_specs=[tile, pl.BlockSpec((tm, 1), lambda i: (i, 0)), row, row, tile],
        out_specs=(tile, tile, row, row, row),
        name=name, compiler_params=_params(1))(xhat, rstd, g, b, target)


def _my_place():
    return lax.axis_index("x"), lax.axis_index("y"), lax.axis_index("c")


def _flip(coord, bit):
    return 1 - coord if bit else coord


def _all_gather(arrays, name):
    n = len(arrays)

    def body(*refs):
        ins, outs = refs[:n], refs[n:2 * n]
        send_sems, recv_sems, local_sems = refs[2 * n:]
        x, y, c = _my_place()
        me, sibling = (x, y, c), (x, y, 1 - c)
        chips = [(1 - x, y), (x, 1 - y), (1 - x, 1 - y)]

        def copy(a, k, block, to, src=None):
            idx = 4 * block[0] + 2 * block[1] + block[2]
            return pltpu.make_async_remote_copy(
                src_ref=outs[a].at[idx] if src is None else src, dst_ref=outs[a].at[idx],
                send_sem=send_sems.at[a, k], recv_sem=recv_sems.at[a, k], device_id=to, device_id_type=MESH)

        mine, first, passed = [], [], []
        for a in range(n):
            cp = pltpu.make_async_copy(ins[a], outs[a].at[4 * x + 2 * y + c], local_sems.at[a])
            cp.start()
            mine.append(cp)
            first.append(copy(a, 0, me, sibling, src=ins[a]))
            first += [copy(a, 1 + j, me, (*chip, c), src=ins[a]) for j, chip in enumerate(chips)]
        for cp in first:
            cp.start()
        for j, chip in enumerate(chips):
            for a in range(n):
                copy(a, 1 + j, (*chip, c), me).wait_recv()
                cp = copy(a, 4 + j, (*chip, c), sibling)
                cp.start()
                passed.append(cp)
        for a in range(n):
            copy(a, 0, sibling, me).wait_recv()
            for j, chip in enumerate(chips):
                copy(a, 4 + j, (*chip, 1 - c), me).wait_recv()
        for cp in first + passed:
            cp.wait_send()
        for cp in mine:
            cp.wait()

    hbm = pl.BlockSpec(memory_space=pltpu.HBM)
    return pl.pallas_call(
        body, out_shape=tuple(jax.ShapeDtypeStruct((N_DEV,) + a.shape, a.dtype) for a in arrays),
        in_specs=[hbm] * n, out_specs=tuple([hbm] * n),
        scratch_shapes=[pltpu.SemaphoreType.DMA((n, 7)), pltpu.SemaphoreType.DMA((n, 7)),
                        pltpu.SemaphoreType.DMA((n,))],
        name=name)(*arrays)


def _peers_of(x, y, c):
    peers = [(_flip(x, k & 4), _flip(y, k & 2), _flip(c, k & 1)) for k in range(1, N_DEV)]
    return peers, [4 * p[0] + 2 * p[1] + p[2] for p in peers]


def _exchange_copies(pieces, lands, send_sems, recv_sems, local_sems, whole):
    x, y, c = _my_place()
    me = 4 * x + 2 * y + c
    peers, slots = _peers_of(x, y, c)

    def piece(p, slot):
        return p if whole else p.at[slot]

    local = [pltpu.make_async_copy(piece(p, me), z.at[me], local_sems.at[a])
             for a, (p, z) in enumerate(zip(pieces, lands))]
    remote = []
    for k, peer in enumerate(peers):
        for a, (p, z) in enumerate(zip(pieces, lands)):
            sems = dict(send_sem=send_sems.at[7 * a + k], recv_sem=recv_sems.at[7 * a + k], device_id=peer,
                        device_id_type=MESH)
            remote.append((pltpu.make_async_remote_copy(src_ref=piece(p, slots[k]), dst_ref=z.at[me], **sems),
                           pltpu.make_async_remote_copy(src_ref=piece(p, slots[k]), dst_ref=z.at[slots[k]], **sems)))
    return local, remote


def _exchange_start(pieces, name, whole=False):
    n = len(pieces)

    def body(*refs):
        ins, lands = refs[:n], refs[n:2 * n]
        send_sems, recv_sems, local_sems = refs[2 * n:2 * n + 3]
        token_ref = refs[-1]
        local, remote = _exchange_copies(ins, lands, send_sems, recv_sems, local_sems, whole)
        for cp in local:
            cp.start()
        for cp, _ in remote:
            cp.start()
        token_ref[...] = jnp.zeros_like(token_ref)

    hbm = pl.BlockSpec(memory_space=pltpu.HBM)
    sem = pl.BlockSpec(memory_space=pltpu.SEMAPHORE)
    thru = [pltpu.HBM(p.shape, p.dtype) for p in pieces]
    zones = [pltpu.HBM(((N_DEV,) + p.shape) if whole else p.shape, p.dtype) for p in pieces]
    outs = pl.pallas_call(
        body,
        out_shape=(pltpu.SemaphoreType.DMA((7 * n,)), pltpu.SemaphoreType.DMA((7 * n,)), pltpu.SemaphoreType.DMA((n,)),
                   *thru, *zones, jax.ShapeDtypeStruct((8, 128), F32)),
        in_specs=[hbm] * (2 * n), out_specs=(sem, sem, sem, *([hbm] * (2 * n)), pl.BlockSpec(memory_space=pltpu.VMEM)),
        input_output_aliases={i: 3 + i for i in range(2 * n)},
        compiler_params=pltpu.CompilerParams(has_side_effects=pltpu.SideEffectType.DATAFLOW_SIDE_EFFECTING),
        name=name,
    )(*[pltpu.with_memory_space_constraint(p, pltpu.HBM) for p in pieces],
      *[pltpu.with_memory_space_constraint(lax.empty(z.shape, z.dtype), pltpu.HBM) for z in zones])
    return outs[:-1], outs[-1]


def _exchange_wait(handles, after, name, whole=False):
    send_sems, recv_sems, local_sems = handles[:3]
    n = (len(handles) - 3) // 2
    pieces, lands = handles[3:3 + n], handles[3 + n:]

    def body(*refs):
        ins, zones = refs[:n], refs[n:2 * n]
        s_sems, r_sems, l_sems = refs[2 * n:2 * n + 3]
        local, remote = _exchange_copies(ins, zones, s_sems, r_sems, l_sems, whole)
        for cp in local:
            cp.wait()
        for cp, landed in remote:
            cp.wait_send()
            landed.wait_recv()

    hbm = pl.BlockSpec(memory_space=pltpu.HBM)
    sem = pl.BlockSpec(memory_space=pltpu.SEMAPHORE)
    thru = [pltpu.HBM(p.shape, p.dtype) for p in list(pieces) + list(lands)]
    outs = pl.pallas_call(
        body, out_shape=tuple(thru),
        in_specs=[hbm] * (2 * n) + [sem, sem, sem, pl.BlockSpec(memory_space=pl.ANY)], out_specs=tuple([hbm] * (2 * n)),
        input_output_aliases={i: i for i in range(2 * n)},
        compiler_params=pltpu.CompilerParams(has_side_effects=pltpu.SideEffectType.DATAFLOW_SIDE_EFFECTING),
        name=name,
    )(*pieces, *lands, send_sems, recv_sems, local_sems, after)
    return outs[n:]


def _adamw(pieces, w, m, v, name, max_rows=256):
    n_l = len(pieces)
    _, r, cols = pieces[0].shape
    tr = max(rows for rows in range(16, min(max_rows, r) + 1, 16) if r % rows == 0)
    n_r = r // tr
    c1 = 1.0 / (1.0 - ADAM_B1 ** ADAM_STEP)
    c2 = 1.0 / (1.0 - ADAM_B2 ** ADAM_STEP)

    def body(*refs):
        p_refs = refs[:n_l]
        w_ref, m_ref, v_ref, g_ref, d_ref, nm_ref, nv_ref = refs[n_l:]
        for layer, p_ref in enumerate(p_refs):
            @pl.when(pl.program_id(0) == layer)
            def _(p_ref=p_ref):
                g = p_ref[0].astype(F32)
                for k in range(1, N_DEV):
                    g = g + p_ref[k].astype(F32)
                nm = ADAM_B1 * m_ref[...] + (1.0 - ADAM_B1) * g
                nv = ADAM_B2 * v_ref[...] + (1.0 - ADAM_B2) * (g * g)
                g_ref[...] = g
                nm_ref[...] = nm
                nv_ref[...] = nv
                d_ref[...] = -ADAM_LR * ((nm * c1) / (jnp.sqrt(nv * c2) + ADAM_EPS) + ADAM_WD * w_ref[...])

    def piece_spec(layer):
        return pl.BlockSpec((N_DEV, tr, cols), lambda l, i: (0, jnp.where(l == layer, i, 0), 0))

    tile = pl.BlockSpec((tr, cols), lambda l, i: (l * n_r + i, 0))
    out = jax.ShapeDtypeStruct((n_l * r, cols), F32)
    return pl.pallas_call(
        body, out_shape=(out, out, out, out), grid=(n_l, n_r),
        in_specs=[piece_spec(layer) for layer in range(n_l)] + [tile, tile, tile],
        out_specs=(tile, tile, tile, tile), name=name, compiler_params=_params(2))(*pieces, w, m, v)


def _rows_of(numel, row_tile):
    rows = -(-numel // LANES)
    return -(-rows // row_tile) * row_tile


def _pack(flat_list, row_tile, lead=()):
    parts = []
    for a in flat_list:
        numel = a.shape[-1]
        rows = _rows_of(numel, row_tile)
        pad = [(0, 0)] * len(lead) + [(0, rows * LANES - numel)]
        parts.append(jnp.pad(a, pad).reshape(*lead, rows, LANES))
    return jnp.concatenate(parts, axis=len(lead))


def _unpack(buf, shapes, row_tile, lead=()):
    out, r0 = [], 0
    for shape in shapes:
        numel = 1
        for s in shape:
            numel *= s
        rows = _rows_of(numel, row_tile)
        part = lax.slice_in_dim(buf, r0, r0 + rows, axis=len(lead)).reshape(*lead, rows * LANES)
        out.append(lax.slice_in_dim(part, 0, numel, axis=len(lead)).reshape(*lead, *shape))
        r0 += rows
    return out


def _to_shards(full, axis):
    shape = full.shape
    cut = full.reshape(shape[:axis] + (N_DEV, shape[axis] // N_DEV) + shape[axis + 1:])
    return jnp.moveaxis(cut, axis, 0)


def _step(x, target, w, m, v):
    t, d = x.shape[1], x.shape[2]
    x2 = x.reshape(t, d)
    tgt2 = target.reshape(t, d)

    small = _pack([w[k].reshape(-1) for k in GATHER_F32], 8)
    w_in_first, small_all = _all_gather([w["a_w_in"][:1].astype(BF16), small], "all_gather_first")
    gw = dict(zip(GATHER_F32, _unpack(small_all, [w[k].shape for k in GATHER_F32], 8, (N_DEV,))))
    mats = {("a_w_in", 0): w_in_first.reshape((N_DEV,) + w_in_first.shape[2:])}
    gather_groups = {
        "layer0": [("a_w_out", 0, 1), ("f_w_up", 0, 1), ("f_w_down", 0, 1)],
        "layer1": [("b_w_in", 0, 1), ("b_w_out", 0, 1), ("f_w_up", 1, 2), ("f_w_down", 1, 2)],
        "layer23": [("a_w_in", 1, 2), ("a_w_out", 1, 2), ("c_w_in", 0, 1), ("c_w_grp", 0, 1), ("c_w_out", 0, 1),
                    ("f_w_up", 2, 4), ("f_w_down", 2, 4)],
    }
    gather_handles = {}

    def start_gather(tag):
        gather_handles[tag], token = _exchange_start(
            [w[k][lo:hi].astype(BF16) for k, lo, hi in gather_groups[tag]], f"all_gather_{tag}_start", whole=True)
        return token[0:1, 0:1]

    gather_started = start_gather("layer0")

    def finish_gather(tag, after):
        lands = _exchange_wait(gather_handles[tag], after, f"all_gather_{tag}_wait", whole=True)
        for (k, lo, hi), land in zip(gather_groups[tag], lands):
            mats.update({(k, layer): land[:, layer - lo] for layer in range(lo, hi)})

    def mat(name, layer):
        return mats[(name, layer)]

    def full_cols(name, layer):
        a = gw[name][:, layer]
        if a.ndim == 2:
            return a.reshape(1, -1)
        return jnp.moveaxis(a, 0, 1).reshape(a.shape[1], -1)

    ones = jnp.ones((1, d), F32)
    zeros = jnp.zeros((1, d), F32)

    saved = []
    res, res_g, res_b = x2, ones, zeros
    xin = x2.astype(BF16)
    for i in range(DEPTH):
        kind, j = i % 3, i // 3
        sv = {"xin": xin, "kind": kind, "j": j}
        if i == 1:
            finish_gather("layer1", xin)
            gather_started = start_gather("layer23")
        if i == 2:
            finish_gather("layer23", xin)
        if kind == 0:
            w_in = mat("a_w_in", j)
            dw, dwb = full_cols("a_dw", j), full_cols("a_dw_b", j)
            if i == 0:
                dwb = dwb + gather_started
            lg, lb = full_cols("a_ln_g", j), full_cols("a_ln_b", j)
            h1 = _mm_nn(xin, w_in, BF16, f"conv_in_{i}")
            s_act, q = _conv_fwd(h1, dw, dwb, lg, lb, f"conv_mix_{i}")
            sv.update(h1=h1, q=q, w_in=w_in, dw=dw, lg=lg, lb=lb)
        elif kind == 1:
            w_in = mat("b_w_in", j)
            lg, lb = w["b_ln_g"][j].reshape(1, d) + gather_started, w["b_ln_b"][j].reshape(1, d)
            ws, bst = w["b_ws"][j], w["b_bs"][j].T
            h1 = _mm_nn(xin, w_in, BF16, f"sgu_in_{i}")
            s_act = _sgu_fwd(h1, lg, lb, ws, bst, f"sgu_mix_{i}")
            sv.update(h1=h1, w_in=w_in, lg=lg, lb=lb, ws=ws, bst=bst)
        else:
            w_in = mat("c_w_in", j).reshape(1, d, d)
            wgrp = jnp.moveaxis(mat("c_w_grp", j), 0, 1).reshape(4, d // 4, d // 4)
            scale = full_cols("c_scale", j)
            yp = _mm_nn(xin, w_in, F32, f"pool_in_{i}")[0]
            s_act = _pool_fwd(yp, wgrp, scale, f"pool_mix_{i}")
            sv.update(yp=yp, w_in=w_in, wgrp=wgrp, scale=scale)
        if i == 0:
            finish_gather("layer0", s_act)
            res_g = res_g + start_gather("layer1")
        w_out = mat(("a_w_out", "b_w_out", "c_w_out")[kind], j).reshape(1, d, d)
        g1, b1 = w["ln1_g"][i].reshape(1, d), w["ln1_b"][i].reshape(1, d)
        xhat1, y1, rstd1 = _mm_res_ln(s_act.reshape(1, t, d), w_out, res, res_g, res_b, g1, b1, f"mix_out_ln_{i}")
        w_up = mat("f_w_up", i)
        fdw = gw["f_dw"][:, i]
        n_ff = w_up.shape[-1]
        w_down = mat("f_w_down", i).reshape(4, n_ff, d)
        g2, b2 = w["ln2_g"][i].reshape(1, d), w["ln2_b"][i].reshape(1, d)
        u, h_ffn, a_act, xhat2, y2, rstd2 = _ffn_fwd(y1, w_up, fdw, w_down, xhat1, g1, b1, g2, b2, f"ffn_fwd_{i}")
        sv.update(s_act=s_act, w_out=w_out, xhat1=xhat1, y1=y1, rstd1=rstd1, g1=g1, u=u, h_ffn=h_ffn, a_act=a_act,
                  w_up=w_up,
                  fdw=fdw, w_down=w_down, xhat2=xhat2, rstd2=rstd2, g2=g2, b2=b2)
        saved.append(sv)
        res, res_g, res_b, xin = xhat2, g2, b2, y2

    last = saved[-1]
    dr2, dr2b, dg2, db2, sq = _loss_head(last["xhat2"], last["rstd2"], last["g2"], last["b2"], tgt2, "loss_head")
    loss = lax.psum((0.5 / d) * jnp.sum(sq), ("x", "y", "c"))

    grads = {k: [None] * w[k].shape[0] for k in WEIGHTS}
    grad_x = None
    small_early = ["b_ln_g", "b_ln_b", "b_ws", "b_bs", "c_scale"]
    small_late = [k for k in WEIGHTS if k not in GATHER_BF16 and k not in small_early]
    exchanges = []

    def small_pieces_of(names):
        flat = []
        for k in names:
            if k in REPLICATED:
                full = jnp.stack([gk.reshape(w[k].shape[1:]) for gk in grads[k]], axis=0)
                flat.append(jnp.broadcast_to(full.reshape(1, -1), (N_DEV, full.size)))
            else:
                flat.append(jnp.stack(grads[k], axis=1).reshape(N_DEV, -1))
        rows = _pack(flat, 8, (N_DEV,))
        pad_rows = -(-rows.shape[1] // 128) * 128 - rows.shape[1]
        return jnp.pad(rows, ((0, 0), (0, pad_rows), (0, 0))), pad_rows

    def start_exchange(keys, tag, extra=()):
        pieces = [grads[k][l].astype(BF16).reshape(N_DEV, -1, w[k].shape[-1]) for k, l in keys]
        handles, token = _exchange_start(pieces + list(extra), f"exchange_start_{tag}")
        exchanges.append((keys, handles, tag))
        return token[0:1, 0:1]

    for i in reversed(range(DEPTH)):
        sv = saved[i]
        kind, j = sv["kind"], sv["j"]
        grads["ln2_g"][i], grads["ln2_b"][i] = dg2, db2
        da = _mm_nt_out(dr2b, sv["w_down"], f"ffn_da_{i}")
        grads["f_w_down"][i] = _to_shards(_mm_tn(sv["a_act"], dr2b.reshape(1, t, d), f"ffn_dwdown_{i}")
                                          .reshape(-1, d), 0)
        du, dfdw = _ffn_gate_bwd(sv["h_ffn"], sv["u"], da, sv["fdw"], f"ffn_gate_bwd_{i}")
        grads["f_dw"][i] = dfdw
        grads["f_w_up"][i] = _mm_tn(sv["y1"].reshape(1, t, d), du, f"ffn_dwup_{i}")
        started = start_exchange([("f_w_up", i), ("f_w_down", i)], f"ffn_{i}")
        dr1, dr1b, dg1, db1 = _mm_nt_lnb(du, sv["w_up"], dr2, sv["xhat1"], sv["rstd1"], sv["g1"], f"ffn_dx_ln_{i}")
        grads["ln1_g"][i], grads["ln1_b"][i] = dg1, db1
        ds = _mm_nt_out(dr1b, sv["w_out"], f"mix_ds_{i}")[0]
        dw_out = _to_shards(_mm_tn(sv["s_act"].reshape(1, t, d), dr1b.reshape(1, t, d), f"mix_dwout_{i}")[0], 0)
        xin3 = sv["xin"].reshape(1, t, d)
        if kind == 0:
            dh1, ddw, ddwb, dlg, dlb = _conv_bwd(ds, sv["q"], sv["h1"], sv["dw"], sv["lg"] + started, sv["lb"],
                                                  f"conv_mix_bwd_{i}")
            grads["a_w_out"][j] = dw_out
            grads["a_dw"][j] = _to_shards(jnp.sum(ddw, axis=1), 1)
            grads["a_dw_b"][j] = _to_shards(ddwb[0], 0)
            grads["a_ln_g"][j] = _to_shards(dlg[0], 0)
            grads["a_ln_b"][j] = _to_shards(dlb[0], 0)
            grads["a_w_in"][j] = _mm_tn(xin3, dh1, f"conv_dwin_{i}")
            dh_in, w_in = dh1, sv["w_in"]
            mixer_keys = [("a_w_in", j), ("a_w_out", j)]
        elif kind == 1:
            dh1, dlg, dlb, dws, dbias = _sgu_bwd(sv["h1"], ds, sv["lg"] + started, sv["lb"], sv["ws"], sv["bst"],
                                                  f"sgu_mix_bwd_{i}")
            grads["b_w_out"][j] = dw_out
            grads["b_ln_g"][j], grads["b_ln_b"][j] = dlg[0], dlb[0]
            grads["b_ws"][j] = dws
            grads["b_bs"][j] = jnp.sum(dbias.reshape(CHUNK, HEADS, CHUNK), axis=-1).T
            grads["b_w_in"][j] = _mm_tn(xin3, dh1, f"sgu_dwin_{i}")
            dh_in, w_in = dh1, sv["w_in"]
            mixer_keys = [("b_w_in", j), ("b_w_out", j)]
        else:
            dyp, dscale, dwgrp = _pool_bwd(sv["yp"], ds, sv["wgrp"], sv["scale"] + started, f"pool_mix_bwd_{i}")
            grads["c_w_out"][j] = dw_out
            grads["c_scale"][j] = _to_shards(dscale[0], 0)
            grads["c_w_grp"][j] = _to_shards(dwgrp, 1)
            dh_in, w_in = dyp.reshape(1, t, d), sv["w_in"]
            grads["c_w_in"][j] = _to_shards(_mm_tn(xin3, dh_in, f"pool_dwin_{i}")[0], 0)
            mixer_keys = [("c_w_in", j), ("c_w_grp", j), ("c_w_out", j)]
        if i > 0:
            extra = [small_pieces_of(small_early)[0]] if i == 1 else []
            started = start_exchange(mixer_keys, f"mixer_{i}", extra=extra)
            prev = saved[i - 1]
            prev["fdw"] = prev["fdw"] + started
            dr2, dr2b, dg2, db2 = _mm_nt_lnb(dh_in, w_in, dr1, prev["xhat2"], prev["rstd2"], prev["g2"],
                                              f"mix_dx_ln_{i}")
        else:
            start_exchange(mixer_keys, f"mixer_{i}", extra=[small_pieces_of(small_late)[0]])
            grad_x = _mm_nt_res(dh_in, w_in, dr1, "mix_dx_0").reshape(x.shape)

    kinds = ("grad", "delta", "new_m", "new_v")
    received, result = {}, {}

    def finish_exchange(group, after):
        keys, handles, tag = group
        lands = _exchange_wait(handles, after, f"exchange_wait_{tag}")
        received.update(zip(keys, lands))
        return lands

    def update(k):
        cols = w[k].shape[-1]
        bufs = _adamw([received[(k, l)] for l in range(w[k].shape[0])], w[k].reshape(-1, cols),
                      m[k].reshape(-1, cols), v[k].reshape(-1, cols), f"adamw_{k}")
        result.update({(kind, k): buf.reshape(w[k].shape) for kind, buf in zip(kinds, bufs)})

    small_received = {}
    for group in exchanges[:-1]:
        lands = finish_exchange(group, grad_x)
        if len(lands) > len(group[0]):
            small_received["early"] = lands[-1]
    late = [k for k, _ in exchanges[-1][0]]
    for k in GATHER_BF16:
        if k not in late:
            update(k)
    small_received["late"] = finish_exchange(exchanges[-1], result[("new_v", "f_w_down")])[-1]
    for k in late:
        update(k)

    for tag, names in (("early", small_early), ("late", small_late)):
        def packed(tree):
            rows = _pack([tree[k].reshape(-1) for k in names], 8)
            return jnp.pad(rows, ((0, small_received[tag].shape[1] - rows.shape[0]), (0, 0)))

        bufs = _adamw([small_received[tag]], packed(w), packed(m), packed(v), f"adamw_small_{tag}")
        shapes = [w[k].shape for k in names]
        for kind, buf in zip(kinds, bufs):
            result.update({(kind, k): a for k, a in zip(names, _unpack(buf, shapes, 8))})
    outs = [result[(kind, k)] for kind in kinds for k in WEIGHTS]
    return (loss, grad_x, *outs)


def kernel(x, a_w_in, a_dw, a_dw_b, a_ln_g, a_ln_b, a_w_out, b_w_in, b_ln_g, b_ln_b, b_ws, b_bs, b_w_out, c_w_in, c_w_grp, c_scale, c_w_out, f_w_up, f_dw, f_w_down, ln1_g, ln1_b, ln2_g, ln2_b, loss_target, m_a_w_in, m_a_dw, m_a_dw_b, m_a_ln_g, m_a_ln_b, m_a_w_out, m_b_w_in, m_b_ln_g, m_b_ln_b, m_b_ws, m_b_bs, m_b_w_out, m_c_w_in, m_c_w_grp, m_c_scale, m_c_w_out, m_f_w_up, m_f_dw, m_f_w_down, m_ln1_g, m_ln1_b, m_ln2_g, m_ln2_b, v_a_w_in, v_a_dw, v_a_dw_b, v_a_ln_g, v_a_ln_b, v_a_w_out, v_b_w_in, v_b_ln_g, v_b_ln_b, v_b_ws, v_b_bs, v_b_w_out, v_c_w_in, v_c_w_grp, v_c_scale, v_c_w_out, v_f_w_up, v_f_dw, v_f_w_down, v_ln1_g, v_ln1_b, v_ln2_g, v_ln2_b):
    w = dict(zip(WEIGHTS, (a_w_in, a_dw, a_dw_b, a_ln_g, a_ln_b, a_w_out, b_w_in, b_ln_g, b_ln_b, b_ws, b_bs, b_w_out,
                           c_w_in, c_w_grp, c_scale, c_w_out, f_w_up, f_dw, f_w_down, ln1_g, ln1_b, ln2_g, ln2_b)))
    m = dict(zip(WEIGHTS, (m_a_w_in, m_a_dw, m_a_dw_b, m_a_ln_g, m_a_ln_b, m_a_w_out, m_b_w_in, m_b_ln_g, m_b_ln_b,
                           m_b_ws, m_b_bs, m_b_w_out, m_c_w_in, m_c_w_grp, m_c_scale, m_c_w_out, m_f_w_up, m_f_dw,
                           m_f_w_down, m_ln1_g, m_ln1_b, m_ln2_g, m_ln2_b)))
    v = dict(zip(WEIGHTS, (v_a_w_in, v_a_dw, v_a_dw_b, v_a_ln_g, v_a_ln_b, v_a_w_out, v_b_w_in, v_b_ln_g, v_b_ln_b,
                           v_b_ws, v_b_bs, v_b_w_out, v_c_w_in, v_c_w_grp, v_c_scale, v_c_w_out, v_f_w_up, v_f_dw,
                           v_f_w_down, v_ln1_g, v_ln1_b, v_ln2_g, v_ln2_b)))
    return _step(x, loss_target, w, m, v)
```

```python
import functools

import jax
import jax.numpy as jnp
from jax import lax
from jax.experimental import pallas as pl
from jax.experimental.pallas import tpu as pltpu

N_DEV = 8
DEPTH = 4
ALPHA = float((2 * DEPTH) ** 0.25)
LN_EPS = 1e-5
CONV_W = 31
CONV_HALO = 32
FFN_HALO = 16
POOL_WINDOWS = (2, 4, 8, 16)
CHUNK = 128
HEADS = 8
LANES = 1024
ADAM_LR, ADAM_B1, ADAM_B2, ADAM_EPS, ADAM_WD, ADAM_STEP = 0.001, 0.9, 0.999, 1e-08, 0.01, 10
VMEM_LIMIT = 56 * 1024 * 1024
F32, BF16 = jnp.float32, jnp.bfloat16
MESH = pl.DeviceIdType.MESH

WEIGHTS = ['a_w_in', 'a_dw', 'a_dw_b', 'a_ln_g', 'a_ln_b', 'a_w_out', 'b_w_in', 'b_ln_g', 'b_ln_b', 'b_ws', 'b_bs',
           'b_w_out', 'c_w_in', 'c_w_grp', 'c_scale', 'c_w_out', 'f_w_up', 'f_dw', 'f_w_down', 'ln1_g', 'ln1_b',
           'ln2_g', 'ln2_b']
REPLICATED = ('b_ln_g', 'b_ln_b', 'b_ws', 'b_bs', 'ln1_g', 'ln1_b', 'ln2_g', 'ln2_b')
GATHER_BF16 = ('a_w_in', 'a_w_out', 'b_w_in', 'b_w_out', 'c_w_in', 'c_w_grp', 'c_w_out', 'f_w_up', 'f_w_down')
GATHER_F32 = ('a_dw', 'a_dw_b', 'a_ln_g', 'a_ln_b', 'c_scale', 'f_dw')


def _params(n_axes):
    return pltpu.CompilerParams(dimension_semantics=("arbitrary",) * n_axes, vmem_limit_bytes=VMEM_LIMIT)


def _resident(shape):
    zeros = (0,) * len(shape)
    return pl.BlockSpec(shape, lambda i: zeros, pipeline_mode=pl.Buffered(1))


def _sigmoid(x):
    return 1.0 / (1.0 + jnp.exp(-x))


def _gelu(x):
    return 0.5 * x * (1.0 + lax.erf(x * 0.7071067811865476))


def _gelu_grad(x):
    return 0.5 * (1.0 + lax.erf(x * 0.7071067811865476)) + x * jnp.exp(-0.5 * x * x) * 0.3989422804014327


def _ln_stats(r):
    mu = jnp.mean(r, axis=-1, keepdims=True)
    xc = r - mu
    var = jnp.mean(xc * xc, axis=-1, keepdims=True)
    rstd = lax.rsqrt(var + LN_EPS)
    return xc * rstd, rstd


def _ln_bwd(dy, xhat, rstd, g):
    dxhat = dy * g
    m1 = jnp.mean(dxhat, axis=-1, keepdims=True)
    m2 = jnp.mean(dxhat * xhat, axis=-1, keepdims=True)
    dr = rstd * (dxhat - m1 - xhat * m2)
    return dr, jnp.sum(dy * xhat, axis=0, keepdims=True), jnp.sum(dy, axis=0, keepdims=True)


def _accumulate(ref, value, first):
    @pl.when(first)
    def _():
        ref[...] = value

    @pl.when(jnp.logical_not(first))
    def _():
        ref[...] += value


def _dot(a, b):
    return jnp.dot(a, b, preferred_element_type=F32)


def _dot_nt(a, b):
    return lax.dot_general(a, b, (((1,), (1,)), ((), ())), preferred_element_type=F32)


def _dot_tn(a, b):
    return lax.dot_general(a, b, (((0,), (0,)), ((), ())), preferred_element_type=F32)


def _cat_lanes(h, lo, hi):
    return jnp.concatenate([h[s] for s in range(lo, hi)], axis=-1)


def _mm_nn(x, w, out_dtype, name, tm=512):
    t, k = x.shape
    s_n, _, n = w.shape
    tm = min(tm, t)

    def body(x_ref, w_ref, o_ref):
        x_tile = x_ref[...]
        for s in range(s_n):
            o_ref[s] = _dot(x_tile, w_ref[s]).astype(o_ref.dtype)

    return pl.pallas_call(
        body, out_shape=jax.ShapeDtypeStruct((s_n, t, n), out_dtype), grid=(t // tm,),
        in_specs=[pl.BlockSpec((tm, k), lambda i: (i, 0)), _resident((s_n, k, n))],
        out_specs=pl.BlockSpec((s_n, tm, n), lambda i: (0, i, 0)),
        name=name, compiler_params=_params(1))(x, w)


def _mm_res_ln(a, w, res, gp, bp, g, b, name, tm=512):
    s_n, t, ka = a.shape
    d = w.shape[-1]
    tm = min(tm, t)

    def body(a_ref, w_ref, res_ref, gp_ref, bp_ref, g_ref, b_ref, xhat_ref, y_ref, rstd_ref):
        acc = _dot(a_ref[0], w_ref[0])
        for s in range(1, s_n):
            acc += _dot(a_ref[s], w_ref[s])
        r = ALPHA * (res_ref[...] * gp_ref[...] + bp_ref[...]) + acc
        xhat, rstd = _ln_stats(r)
        xhat_ref[...] = xhat
        y_ref[...] = (xhat * g_ref[...] + b_ref[...]).astype(BF16)
        rstd_ref[...] = rstd

    row = pl.BlockSpec((1, d), lambda i: (0, 0))
    tile = pl.BlockSpec((tm, d), lambda i: (i, 0))
    return pl.pallas_call(
        body,
        out_shape=(jax.ShapeDtypeStruct((t, d), F32), jax.ShapeDtypeStruct((t, d), BF16),
                   jax.ShapeDtypeStruct((t, 1), F32)),
        grid=(t // tm,),
        in_specs=[pl.BlockSpec((s_n, tm, ka), lambda i: (0, i, 0)), _resident((s_n, ka, d)),
                  tile, row, row, row, row],
        out_specs=(tile, tile, pl.BlockSpec((tm, 1), lambda i: (i, 0))),
        name=name, compiler_params=_params(1))(a, w, res, gp, bp, g, b)


def _mm_nt_out(x, w, name, tm=512):
    t, n = x.shape
    s_n, k, _ = w.shape
    tm = min(tm, t)

    def body(x_ref, w_ref, o_ref):
        x_tile = x_ref[...]
        for s in range(s_n):
            o_ref[s] = _dot_nt(x_tile, w_ref[s]).astype(o_ref.dtype)

    return pl.pallas_call(
        body, out_shape=jax.ShapeDtypeStruct((s_n, t, k), BF16), grid=(t // tm,),
        in_specs=[pl.BlockSpec((tm, n), lambda i: (i, 0)), _resident((s_n, k, n))],
        out_specs=pl.BlockSpec((s_n, tm, k), lambda i: (0, i, 0)),
        name=name, compiler_params=_params(1))(x, w)


def _mm_nt_lnb(dh, w, drn, xhat, rstd, g, name, tm=512):
    s_n, t, n = dh.shape
    k = w.shape[1]
    tm = min(tm, t)

    def body(dh_ref, w_ref, drn_ref, xhat_ref, rstd_ref, g_ref, dr_ref, drb_ref, dg_ref, db_ref):
        first = pl.program_id(0) == 0
        acc = _dot_nt(dh_ref[0], w_ref[0])
        for s in range(1, s_n):
            acc += _dot_nt(dh_ref[s], w_ref[s])
        dy = acc + ALPHA * drn_ref[...]
        dr, dg, db = _ln_bwd(dy, xhat_ref[...], rstd_ref[...], g_ref[...])
        dr_ref[...] = dr
        drb_ref[...] = dr.astype(BF16)
        _accumulate(dg_ref, dg, first)
        _accumulate(db_ref, db, first)

    tile = pl.BlockSpec((tm, k), lambda i: (i, 0))
    row = pl.BlockSpec((1, k), lambda i: (0, 0))
    return pl.pallas_call(
        body,
        out_shape=(jax.ShapeDtypeStruct((t, k), F32), jax.ShapeDtypeStruct((t, k), BF16),
                   jax.ShapeDtypeStruct((1, k), F32), jax.ShapeDtypeStruct((1, k), F32)),
        grid=(t // tm,),
        in_specs=[pl.BlockSpec((s_n, tm, n), lambda i: (0, i, 0)), _resident((s_n, k, n)),
                  tile, tile, pl.BlockSpec((tm, 1), lambda i: (i, 0)), row],
        out_specs=(tile, tile, row, row),
        name=name, compiler_params=_params(1))(dh, w, drn, xhat, rstd, g)


def _mm_nt_res(dh, w, drn, name, tm=512):
    s_n, t, n = dh.shape
    k = w.shape[1]
    tm = min(tm, t)

    def body(dh_ref, w_ref, drn_ref, o_ref):
        acc = _dot_nt(dh_ref[0], w_ref[0])
        for s in range(1, s_n):
            acc += _dot_nt(dh_ref[s], w_ref[s])
        o_ref[...] = acc + ALPHA * drn_ref[...]

    tile = pl.BlockSpec((tm, k), lambda i: (i, 0))
    return pl.pallas_call(
        body, out_shape=jax.ShapeDtypeStruct((t, k), F32), grid=(t // tm,),
        in_specs=[pl.BlockSpec((s_n, tm, n), lambda i: (0, i, 0)), _resident((s_n, k, n)), tile],
        out_specs=tile, name=name, compiler_params=_params(1))(dh, w, drn)


def _mm_tn(lhs, rhs, name, tm=2048):
    sl, t, kl = lhs.shape
    sr, _, n = rhs.shape
    s_n = max(sl, sr)
    tm = min(tm, t)

    def body(l_ref, r_ref, o_ref):
        _accumulate(o_ref, _dot_tn(l_ref[...], r_ref[...]), pl.program_id(1) == 0)

    return pl.pallas_call(
        body, out_shape=jax.ShapeDtypeStruct((s_n, kl, n), F32), grid=(s_n, t // tm),
        in_specs=[pl.BlockSpec((None, tm, kl), (lambda s, i: (s, i, 0)) if sl > 1 else (lambda s, i: (0, i, 0))),
                  pl.BlockSpec((None, tm, n), (lambda s, i: (s, i, 0)) if sr > 1 else (lambda s, i: (0, i, 0)))],
        out_specs=pl.BlockSpec((None, kl, n), lambda s, i: (s, 0, 0)),
        name=name, compiler_params=_params(2))(lhs, rhs)


def _glu(h):
    return _cat_lanes(h, 0, 4).astype(F32) * _sigmoid(_cat_lanes(h, 4, 8).astype(F32))


def _shifted_windows(src_ref, r0, c0, row_block, offsets):
    span = row_block + CONV_HALO
    big = src_ref[pl.ds(r0, span), pl.ds(c0, 128)]
    for sub in range(8):
        taps = [k for k, o in enumerate(offsets) if o % 8 == sub]
        if not taps:
            continue
        rolled = big if sub == 0 else pltpu.roll(big, span - sub, 0)
        for k in taps:
            lo = offsets[k] - sub
            yield k, rolled[lo:lo + row_block]


def _tap_loop(src_ref, dst_ref, weight_ref, rows, offsets, weight_rows, row_block=64):
    d = dst_ref.shape[-1]

    def block(cb, carry):
        c0 = pl.multiple_of(cb * 128, 128)
        for r0 in range(0, rows, row_block):
            acc = jnp.zeros((row_block, 128), F32)
            for k, window in _shifted_windows(src_ref, r0, c0, row_block, offsets):
                acc += weight_ref[pl.ds(weight_rows[k], 1), pl.ds(c0, 128)] * window
            dst_ref[pl.ds(r0, row_block), pl.ds(c0, 128)] = acc
        return carry

    lax.fori_loop(0, d // 128, block, 0)


def _conv_fwd(h1, dw, dwb, g, b, name, tm=256):
    _, t, _ = h1.shape
    d = dw.shape[-1]
    tm = min(tm, t)
    hb = tm // CONV_HALO

    def body(h_ref, halo_ref, dw_ref, dwb_ref, g_ref, b_ref, s_ref, q_ref, ext_ref):
        i = pl.program_id(0)
        ext_ref[pl.ds(0, CONV_HALO), :] = _glu(halo_ref[...]) * (i > 0).astype(F32)
        ext_ref[pl.ds(CONV_HALO, tm), :] = _glu(h_ref[...])
        _tap_loop(ext_ref, q_ref, dw_ref, tm, [2 + k for k in range(CONV_W)], list(range(CONV_W)))
        q = q_ref[...] + dwb_ref[...]
        q_ref[...] = q
        qhat, _ = _ln_stats(q)
        z = qhat * g_ref[...] + b_ref[...]
        s_ref[...] = (z * _sigmoid(z)).astype(BF16)

    row = pl.BlockSpec((1, d), lambda i: (0, 0))
    tile = pl.BlockSpec((tm, d), lambda i: (i, 0))
    return pl.pallas_call(
        body, out_shape=(jax.ShapeDtypeStruct((t, d), BF16), jax.ShapeDtypeStruct((t, d), F32)), grid=(t // tm,),
        in_specs=[pl.BlockSpec((8, tm, 256), lambda i: (0, i, 0)),
                  pl.BlockSpec((8, CONV_HALO, 256), lambda i: (0, jnp.maximum(i * hb - 1, 0), 0)),
                  pl.BlockSpec((CONV_W, d), lambda i: (0, 0)), row, row, row],
        out_specs=(tile, tile), scratch_shapes=[pltpu.VMEM((tm + CONV_HALO, d), F32)],
        name=name, compiler_params=_params(1))(h1, h1, dw, dwb, g, b)


def _conv_bwd(ds, q, h1, dw, g, b, name, tm=256):
    _, t, _ = h1.shape
    d = dw.shape[-1]
    tm = min(tm, t)
    hb = tm // CONV_HALO
    n_t = t // tm
    last_halo = t // CONV_HALO - 1

    def body(ds_ref, dsn_ref, q_ref, qn_ref, h_ref, hp_ref, dw_ref, g_ref, b_ref,
             dh_ref, ddw_ref, ddwb_ref, dg_ref, db_ref, dq_ref, p_ref, dp_ref):
        i = pl.program_id(0)
        first = i == 0
        valid = (i < n_t - 1).astype(F32)

        def dq_rows(ds_rows, q_rows, scale):
            qhat, rstd = _ln_stats(q_rows)
            z = qhat * g_ref[...] + b_ref[...]
            sg = _sigmoid(z)
            dz = ds_rows.astype(F32) * (sg * (1.0 + z * (1.0 - sg))) * scale
            dq, dg, db = _ln_bwd(dz, qhat, rstd, g_ref[...])
            return dq, dg, db

        dq, dg, db = dq_rows(ds_ref[...], q_ref[...], 1.0)
        dq_ref[pl.ds(0, tm), :] = dq
        dq_ref[pl.ds(tm, CONV_HALO), :] = dq_rows(dsn_ref[...], qn_ref[...], valid)[0]
        _accumulate(dg_ref, dg, first)
        _accumulate(db_ref, db, first)
        _accumulate(ddwb_ref, jnp.sum(dq, axis=0, keepdims=True), first)

        p_ref[pl.ds(0, CONV_HALO), :] = _glu(hp_ref[...]) * (i > 0).astype(F32)
        p_ref[pl.ds(CONV_HALO, tm), :] = _glu(h_ref[...])

        _tap_loop(dq_ref, dp_ref, dw_ref, tm, list(range(CONV_W)), [CONV_W - 1 - o for o in range(CONV_W)])

        @pl.when(first)
        def _():
            ddw_ref[...] = jnp.zeros_like(ddw_ref)

        row_block = 64

        def block(cb, carry):
            c0 = pl.multiple_of(cb * 128, 128)
            for r0 in range(0, tm, row_block):
                dqb = dq_ref[pl.ds(r0, row_block), pl.ds(c0, 128)]
                for k, window in _shifted_windows(p_ref, r0, c0, row_block, [2 + k for k in range(CONV_W)]):
                    prod = dqb * window
                    ddw_ref[k, :, pl.ds(c0, 128)] += jnp.sum(prod.reshape(row_block // 8, 8, 128), axis=0)
            return carry

        lax.fori_loop(0, d // 128, block, 0)

        h = h_ref[...]
        a = _cat_lanes(h, 0, 4).astype(F32)
        sg = _sigmoid(_cat_lanes(h, 4, 8).astype(F32))
        dp = dp_ref[...]
        da = (dp * sg).astype(BF16)
        dgate = (dp * a * sg * (1.0 - sg)).astype(BF16)
        for s in range(4):
            dh_ref[s] = da[:, s * 256:(s + 1) * 256]
            dh_ref[4 + s] = dgate[:, s * 256:(s + 1) * 256]

    row = pl.BlockSpec((1, d), lambda i: (0, 0))
    tile = pl.BlockSpec((tm, d), lambda i: (i, 0))
    nxt = pl.BlockSpec((CONV_HALO, d), lambda i: (jnp.minimum((i + 1) * hb, last_halo), 0))
    return pl.pallas_call(
        body,
        out_shape=(jax.ShapeDtypeStruct((8, t, 256), BF16), jax.ShapeDtypeStruct((CONV_W, 8, d), F32),
                   jax.ShapeDtypeStruct((1, d), F32), jax.ShapeDtypeStruct((1, d), F32),
                   jax.ShapeDtypeStruct((1, d), F32)),
        grid=(n_t,),
        in_specs=[tile, nxt, tile, nxt,
                  pl.BlockSpec((8, tm, 256), lambda i: (0, i, 0)),
                  pl.BlockSpec((8, CONV_HALO, 256), lambda i: (0, jnp.maximum(i * hb - 1, 0), 0)),
                  pl.BlockSpec((CONV_W, d), lambda i: (0, 0)), row, row],
        out_specs=(pl.BlockSpec((8, tm, 256), lambda i: (0, i, 0)),
                   pl.BlockSpec((CONV_W, 8, d), lambda i: (0, 0, 0)), row, row, row),
        scratch_shapes=[pltpu.VMEM((tm + CONV_HALO, d), F32), pltpu.VMEM((tm + CONV_HALO, d), F32),
                        pltpu.VMEM((tm, d), F32)],
        name=name, compiler_params=_params(1))(ds, ds, q, q, h1, h1, dw, g, b)


def _conv3(ext, dw):
    e1 = pltpu.roll(ext, 1, 0)
    e2 = pltpu.roll(ext, 2, 0)
    return dw[2:3] * ext + dw[1:2] * e1 + dw[0:1] * e2, e1, e2


def _ffn_fwd(x, w_up, fdw, w_down, res, gp, bp, g, b, name, tm=256):
    t, d = x.shape
    _, _, n = w_up.shape
    tm = min(tm, t)
    carry_rows = 8

    def body(x_ref, wu_ref, dw_ref, wd_ref, res_ref, gp_ref, bp_ref, g_ref, b_ref,
             u_ref, h_ref, a_ref, xhat_ref, y_ref, rstd_ref, carry_ref):
        @pl.when(pl.program_id(0) == 0)
        def _():
            carry_ref[...] = jnp.zeros_like(carry_ref)

        x_tile = x_ref[...]
        for j in range(4):
            h = []
            for s in (j, 4 + j):
                ub = _dot(x_tile, wu_ref[s]).astype(BF16)
                u_ref[s] = ub
                uf = ub.astype(F32)
                ext = jnp.concatenate([carry_ref[s], uf], axis=0)
                carry_ref[s] = uf[tm - carry_rows:]
                hb = _conv3(ext, dw_ref[s])[0][carry_rows:].astype(BF16)
                h_ref[s] = hb
                h.append(hb.astype(F32))
            a_ref[j] = (h[0] * _sigmoid(h[0]) * h[1]).astype(BF16)
        acc = _dot(a_ref[0], wd_ref[0])
        for j in range(1, 4):
            acc += _dot(a_ref[j], wd_ref[j])
        r = ALPHA * (res_ref[...] * gp_ref[...] + bp_ref[...]) + acc
        xhat, rstd = _ln_stats(r)
        xhat_ref[...] = xhat
        y_ref[...] = (xhat * g_ref[...] + b_ref[...]).astype(BF16)
        rstd_ref[...] = rstd

    row = pl.BlockSpec((1, d), lambda i: (0, 0))
    tile = pl.BlockSpec((tm, d), lambda i: (i, 0))
    return pl.pallas_call(
        body,
        out_shape=(jax.ShapeDtypeStruct((8, t, n), BF16), jax.ShapeDtypeStruct((8, t, n), BF16),
                   jax.ShapeDtypeStruct((4, t, n), BF16), jax.ShapeDtypeStruct((t, d), F32),
                   jax.ShapeDtypeStruct((t, d), BF16), jax.ShapeDtypeStruct((t, 1), F32)),
        grid=(t // tm,),
        in_specs=[tile, _resident((8, d, n)), pl.BlockSpec((8, 3, n), lambda i: (0, 0, 0)), _resident((4, n, d)),
                  tile, row, row, row, row],
        out_specs=(pl.BlockSpec((8, tm, n), lambda i: (0, i, 0)), pl.BlockSpec((8, tm, n), lambda i: (0, i, 0)),
                   pl.BlockSpec((4, tm, n), lambda i: (0, i, 0)),
                   tile, tile, pl.BlockSpec((tm, 1), lambda i: (i, 0))),
        scratch_shapes=[pltpu.VMEM((8, carry_rows, n), F32)],
        name=name, compiler_params=_params(1))(x, w_up, fdw, w_down, res, gp, bp, g, b)


def _ffn_gate_bwd(h, u, da, fdw, name, tm=256):
    _, t, n = u.shape
    tm = min(tm, t)
    hb = tm // FFN_HALO
    n_t = t // tm
    last_halo = t // FFN_HALO - 1
    h4, u4 = h.reshape(2, 4, t, n), u.reshape(2, 4, t, n)
    fdw4 = fdw.reshape(2, 4, 3, n)
    rows = tm + FFN_HALO

    def body(h_ref, hn_ref, u_ref, da_ref, dan_ref, dw_ref, du_ref, ddw_ref):
        i = pl.program_id(1)
        keep_next = (i < n_t - 1).astype(F32)
        hg = jnp.concatenate([h_ref[0], hn_ref[0]], axis=0).astype(F32)
        hv = jnp.concatenate([h_ref[1], hn_ref[1]], axis=0).astype(F32)
        da_ext = jnp.concatenate([da_ref[...].astype(F32), dan_ref[...].astype(F32) * keep_next], axis=0)
        sg = _sigmoid(hg)
        silu = hg * sg
        dh = (da_ext * hv * (sg + silu * (1.0 - sg)), da_ext * silu)
        for p in range(2):
            dwp = dw_ref[p]
            d1 = pltpu.roll(dh[p], rows - 1, 0)
            d2 = pltpu.roll(dh[p], rows - 2, 0)
            du_ref[p] = (dwp[2:3] * dh[p] + dwp[1:2] * d1 + dwp[0:1] * d2)[:tm].astype(BF16)
            up = u_ref[p].astype(F32)
            part = jnp.concatenate([jnp.sum(d[:tm] * up, axis=0, keepdims=True) for d in (d2, d1, dh[p])], axis=0)
            _accumulate(ddw_ref.at[p], part, i == 0)

    tile = pl.BlockSpec((2, None, tm, n), lambda j, i: (0, j, i, 0))
    nxt = pl.BlockSpec((2, None, FFN_HALO, n), lambda j, i: (0, j, jnp.minimum((i + 1) * hb, last_halo), 0))
    du, ddw = pl.pallas_call(
        body, out_shape=(jax.ShapeDtypeStruct((2, 4, t, n), BF16), jax.ShapeDtypeStruct((2, 4, 3, n), F32)),
        grid=(4, n_t),
        in_specs=[tile, nxt, tile,
                  pl.BlockSpec((None, tm, n), lambda j, i: (j, i, 0)),
                  pl.BlockSpec((None, FFN_HALO, n), lambda j, i: (j, jnp.minimum((i + 1) * hb, last_halo), 0)),
                  pl.BlockSpec((2, None, 3, n), lambda j, i: (0, j, 0, 0))],
        out_specs=(tile, pl.BlockSpec((2, None, 3, n), lambda j, i: (0, j, 0, 0))),
        name=name, compiler_params=_params(2))(h4, h4, u4, da, da, fdw4)
    return du.reshape(8, t, n), ddw.reshape(8, 3, n)


def _tril_mask():
    r = lax.broadcasted_iota(jnp.int32, (CHUNK, CHUNK), 0)
    c = lax.broadcasted_iota(jnp.int32, (CHUNK, CHUNK), 1)
    return (r >= c).astype(F32)


def _sgu_fwd(h1, g, b, ws, bst, name, tm=256):
    _, t, _ = h1.shape
    d = g.shape[-1]
    tm = min(tm, t)

    def body(h_ref, g_ref, b_ref, ws_ref, bst_ref, m_ref):
        h = h_ref[...]
        u = _gelu(_cat_lanes(h, 0, 4).astype(F32))
        v = _gelu(_cat_lanes(h, 4, 8).astype(F32))
        vn = (_ln_stats(v)[0] * g_ref[...] + b_ref[...]).astype(BF16)
        mask = _tril_mask()
        for hh in range(HEADS):
            cols = slice(hh * CHUNK, (hh + 1) * CHUNK)
            wm = (ws_ref[hh] * mask).astype(BF16)
            bias = bst_ref[:, hh:hh + 1]
            for c in range(tm // CHUNK):
                rows = slice(c * CHUNK, (c + 1) * CHUNK)
                sblk = _dot(wm, vn[rows, cols]) + bias
                m_ref[rows, cols] = (u[rows, cols] * sblk).astype(BF16)

    row = pl.BlockSpec((1, d), lambda i: (0, 0))
    return pl.pallas_call(
        body, out_shape=jax.ShapeDtypeStruct((t, d), BF16), grid=(t // tm,),
        in_specs=[pl.BlockSpec((8, tm, 256), lambda i: (0, i, 0)), row, row,
                  pl.BlockSpec((HEADS, CHUNK, CHUNK), lambda i: (0, 0, 0)),
                  pl.BlockSpec((CHUNK, HEADS), lambda i: (0, 0))],
        out_specs=pl.BlockSpec((tm, d), lambda i: (i, 0)),
        name=name, compiler_params=_params(1))(h1, g, b, ws, bst)


def _sgu_bwd(h1, dm, g, b, ws, bst, name, tm=256):
    _, t, _ = h1.shape
    d = g.shape[-1]
    tm = min(tm, t)

    def body(h_ref, dm_ref, g_ref, b_ref, ws_ref, bst_ref, dh_ref, dg_ref, db_ref, dws_ref, dbias_ref,
             du_ref, dvn_ref):
        first = pl.program_id(0) == 0
        h = h_ref[...]
        zu = _cat_lanes(h, 0, 4).astype(F32)
        zv = _cat_lanes(h, 4, 8).astype(F32)
        u = _gelu(zu)
        vhat, rstd = _ln_stats(_gelu(zv))
        vn = (vhat * g_ref[...] + b_ref[...]).astype(BF16)
        dm = dm_ref[...].astype(F32)
        mask = _tril_mask()

        @pl.when(first)
        def _():
            dws_ref[...] = jnp.zeros_like(dws_ref)
            dbias_ref[...] = jnp.zeros_like(dbias_ref)

        for hh in range(HEADS):
            cols = slice(hh * CHUNK, (hh + 1) * CHUNK)
            wm = (ws_ref[hh] * mask).astype(BF16)
            bias = bst_ref[:, hh:hh + 1]
            for c in range(tm // CHUNK):
                rows = slice(c * CHUNK, (c + 1) * CHUNK)
                vb = vn[rows, cols]
                sblk = _dot(wm, vb) + bias
                dmb = dm[rows, cols]
                du_ref[rows, cols] = dmb * sblk
                dsb = dmb * u[rows, cols]
                dbias_ref[:, cols] += dsb
                dsb16 = dsb.astype(BF16)
                dws_ref[hh] += _dot_nt(dsb16, vb) * mask
                dvn_ref[rows, cols] = _dot_tn(wm, dsb16)

        dvn = dvn_ref[...]
        dv, dg, db = _ln_bwd(dvn, vhat, rstd, g_ref[...])
        _accumulate(dg_ref, dg, first)
        _accumulate(db_ref, db, first)
        dzu = (du_ref[...] * _gelu_grad(zu)).astype(BF16)
        dzv = (dv * _gelu_grad(zv)).astype(BF16)
        for s in range(4):
            dh_ref[s] = dzu[:, s * 256:(s + 1) * 256]
            dh_ref[4 + s] = dzv[:, s * 256:(s + 1) * 256]

    row = pl.BlockSpec((1, d), lambda i: (0, 0))
    tile = pl.BlockSpec((tm, d), lambda i: (i, 0))
    h_tile = pl.BlockSpec((8, tm, 256), lambda i: (0, i, 0))
    return pl.pallas_call(
        body,
        out_shape=(jax.ShapeDtypeStruct((8, t, 256), BF16), jax.ShapeDtypeStruct((1, d), F32),
                   jax.ShapeDtypeStruct((1, d), F32), jax.ShapeDtypeStruct((HEADS, CHUNK, CHUNK), F32),
                   jax.ShapeDtypeStruct((CHUNK, d), F32)),
        grid=(t // tm,),
        in_specs=[h_tile, tile, row, row, pl.BlockSpec((HEADS, CHUNK, CHUNK), lambda i: (0, 0, 0)),
                  pl.BlockSpec((CHUNK, HEADS), lambda i: (0, 0))],
        out_specs=(h_tile, row, row, pl.BlockSpec((HEADS, CHUNK, CHUNK), lambda i: (0, 0, 0)),
                   pl.BlockSpec((CHUNK, d), lambda i: (0, 0))),
        scratch_shapes=[pltpu.VMEM((tm, d), F32), pltpu.VMEM((tm, d), F32)],
        name=name, compiler_params=_params(1))(h1, dm, g, b, ws, bst)


def _pool_minus_self(ext, first_token, grp):
    s = ext
    for step in range(grp + 1):
        s = s + pltpu.roll(s, 1 << step, 0)
    rows = ext.shape[0] - FFN_HALO
    tok = first_token + lax.broadcasted_iota(jnp.int32, (rows, 1), 0)
    count = jnp.minimum(tok + 1, POOL_WINDOWS[grp]).astype(F32)
    return s[FFN_HALO:] / count - ext[FFN_HALO:]


def _pool_fwd(y, wgrp, scale, name, tm=256):
    t, d = y.shape
    tm = min(tm, t)
    hb = tm // FFN_HALO
    gd = d // len(POOL_WINDOWS)

    def body(y_ref, yp_ref, w_ref, sc_ref, z_ref):
        i = pl.program_id(0)
        ext = jnp.concatenate([yp_ref[...] * (i > 0).astype(F32), y_ref[...]], axis=0)
        for grp in range(len(POOL_WINDOWS)):
            cols = slice(grp * gd, (grp + 1) * gd)
            p = _pool_minus_self(ext[:, cols], i * tm, grp)
            z_ref[:, cols] = (_dot(p.astype(BF16), w_ref[grp]) * sc_ref[:, cols]).astype(BF16)

    return pl.pallas_call(
        body, out_shape=jax.ShapeDtypeStruct((t, d), BF16), grid=(t // tm,),
        in_specs=[pl.BlockSpec((tm, d), lambda i: (i, 0)),
                  pl.BlockSpec((FFN_HALO, d), lambda i: (jnp.maximum(i * hb - 1, 0), 0)),
                  pl.BlockSpec((len(POOL_WINDOWS), gd, gd), lambda i: (0, 0, 0)),
                  pl.BlockSpec((1, d), lambda i: (0, 0))],
        out_specs=pl.BlockSpec((tm, d), lambda i: (i, 0)),
        name=name, compiler_params=_params(1))(y, y, wgrp, scale)


def _pool_bwd(y, dz, wgrp, scale, name, tm=256):
    t, d = y.shape
    tm = min(tm, t)
    hb = tm // FFN_HALO
    n_t = t // tm
    last_halo = t // FFN_HALO - 1
    gd = d // len(POOL_WINDOWS)
    rows = tm + FFN_HALO

    def body(y_ref, yp_ref, dz_ref, dzn_ref, w_ref, sc_ref, dy_ref, dsc_ref, dw_ref):
        i = pl.program_id(0)
        first = i == 0
        ext = jnp.concatenate([yp_ref[...] * (i > 0).astype(F32), y_ref[...]], axis=0)
        dz_ext = jnp.concatenate([dz_ref[...].astype(F32), dzn_ref[...].astype(F32) * (i < n_t - 1).astype(F32)],
                                 axis=0)
        tok = i * tm + lax.broadcasted_iota(jnp.int32, (rows, 1), 0)
        dsc = []
        for grp in range(len(POOL_WINDOWS)):
            cols = slice(grp * gd, (grp + 1) * gd)
            p16 = _pool_minus_self(ext[:, cols], i * tm, grp).astype(BF16)
            zg = _dot(p16, w_ref[grp])
            dsc.append(jnp.sum(dz_ext[:tm, cols] * zg, axis=0, keepdims=True))
            dzg = (dz_ext[:, cols] * sc_ref[:, cols]).astype(BF16)
            _accumulate(dw_ref.at[grp], _dot_tn(p16, dzg[:tm]), first)
            dp = _dot_nt(dzg, w_ref[grp])
            s = dp / jnp.minimum(tok + 1, POOL_WINDOWS[grp]).astype(F32)
            for step in range(grp + 1):
                s = s + pltpu.roll(s, rows - (1 << step), 0)
            dy_ref[:, cols] = (s[:tm] - dp[:tm]).astype(BF16)
        _accumulate(dsc_ref, jnp.concatenate(dsc, axis=-1), first)

    tile = pl.BlockSpec((tm, d), lambda i: (i, 0))
    return pl.pallas_call(
        body,
        out_shape=(jax.ShapeDtypeStruct((t, d), BF16), jax.ShapeDtypeStruct((1, d), F32),
                   jax.ShapeDtypeStruct((len(POOL_WINDOWS), gd, gd), F32)),
        grid=(n_t,),
        in_specs=[tile, pl.BlockSpec((FFN_HALO, d), lambda i: (jnp.maximum(i * hb - 1, 0), 0)),
                  tile, pl.BlockSpec((FFN_HALO, d), lambda i: (jnp.minimum((i + 1) * hb, last_halo), 0)),
                  pl.BlockSpec((len(POOL_WINDOWS), gd, gd), lambda i: (0, 0, 0)),
                  pl.BlockSpec((1, d), lambda i: (0, 0))],
        out_specs=(tile, pl.BlockSpec((1, d), lambda i: (0, 0)),
                   pl.BlockSpec((len(POOL_WINDOWS), gd, gd), lambda i: (0, 0, 0))),
        name=name, compiler_params=_params(1))(y, y, dz, dz, wgrp, scale)


def _loss_head(xhat, rstd, g, b, target, name, tm=512):
    t, d = xhat.shape
    tm = min(tm, t)

    def body(xhat_ref, rstd_ref, g_ref, b_ref, tgt_ref, dr_ref, drb_ref, dg_ref, db_ref, sq_ref):
        first = pl.program_id(0) == 0
        xhat_t = xhat_ref[...]
        diff = xhat_t * g_ref[...] + b_ref[...] - tgt_ref[...]
        dr, dg, db = _ln_bwd(diff * (1.0 / d), xhat_t, rstd_ref[...], g_ref[...])
        dr_ref[...] = dr
        drb_ref[...] = dr.astype(BF16)
        _accumulate(dg_ref, dg, first)
        _accumulate(db_ref, db, first)
        _accumulate(sq_ref, jnp.sum(diff * diff, axis=0, keepdims=True), first)

    row = pl.BlockSpec((1, d), lambda i: (0, 0))
    tile = pl.BlockSpec((tm, d), lambda i: (i, 0))
    return pl.pallas_call(
        body,
        out_shape=(jax.ShapeDtypeStruct((t, d), F32), jax.ShapeDtypeStruct((t, d), BF16),
                   jax.ShapeDtypeStruct((1, d), F32), jax.ShapeDtypeStruct((1, d), F32),
                   jax.ShapeDtypeStruct((1, d), F32)),
        grid=(t // tm,),
        in_specs=[tile, pl.BlockSpec((tm, 1), lambda i: (i, 0)), row, row, tile],
        out_specs=(tile, tile, row, row, row),
        name=name, compiler_params=_params(1))(xhat, rstd, g, b, target)


def _my_place():
    return lax.axis_index("x"), lax.axis_index("y"), lax.axis_index("c")


def _flip(coord, bit):
    return 1 - coord if bit else coord


def _all_gather(arrays, name):
    n = len(arrays)

    def body(*refs):
        ins, outs = refs[:n], refs[n:2 * n]
        send_sems, recv_sems, local_sems = refs[2 * n:]
        x, y, c = _my_place()
        me, sibling = (x, y, c), (x, y, 1 - c)
        chips = [(1 - x, y), (x, 1 - y), (1 - x, 1 - y)]

        def copy(a, k, block, to, src=None):
            idx = 4 * block[0] + 2 * block[1] + block[2]
            return pltpu.make_async_remote_copy(
                src_ref=outs[a].at[idx] if src is None else src, dst_ref=outs[a].at[idx],
                send_sem=send_sems.at[a, k], recv_sem=recv_sems.at[a, k], device_id=to, device_id_type=MESH)

        mine, first, passed = [], [], []
        for a in range(n):
            cp = pltpu.make_async_copy(ins[a], outs[a].at[4 * x + 2 * y + c], local_sems.at[a])
            cp.start()
            mine.append(cp)
            first.append(copy(a, 0, me, sibling, src=ins[a]))
            first += [copy(a, 1 + j, me, (*chip, c), src=ins[a]) for j, chip in enumerate(chips)]
        for cp in first:
            cp.start()
        for j, chip in enumerate(chips):
            for a in range(n):
                copy(a, 1 + j, (*chip, c), me).wait_recv()
                cp = copy(a, 4 + j, (*chip, c), sibling)
                cp.start()
                passed.append(cp)
        for a in range(n):
            copy(a, 0, sibling, me).wait_recv()
            for j, chip in enumerate(chips):
                copy(a, 4 + j, (*chip, 1 - c), me).wait_recv()
        for cp in first + passed:
            cp.wait_send()
        for cp in mine:
            cp.wait()

    hbm = pl.BlockSpec(memory_space=pltpu.HBM)
    return pl.pallas_call(
        body, out_shape=tuple(jax.ShapeDtypeStruct((N_DEV,) + a.shape, a.dtype) for a in arrays),
        in_specs=[hbm] * n, out_specs=tuple([hbm] * n),
        scratch_shapes=[pltpu.SemaphoreType.DMA((n, 7)), pltpu.SemaphoreType.DMA((n, 7)),
                        pltpu.SemaphoreType.DMA((n,))],
        name=name)(*arrays)


def _peers_of(x, y, c):
    peers = [(_flip(x, k & 4), _flip(y, k & 2), _flip(c, k & 1)) for k in range(1, N_DEV)]
    return peers, [4 * p[0] + 2 * p[1] + p[2] for p in peers]


def _exchange_copies(pieces, lands, send_sems, recv_sems, local_sems, whole):
    x, y, c = _my_place()
    me = 4 * x + 2 * y + c
    peers, slots = _peers_of(x, y, c)

    def piece(p, slot):
        return p if whole else p.at[slot]

    local = [pltpu.make_async_copy(piece(p, me), z.at[me], local_sems.at[a])
             for a, (p, z) in enumerate(zip(pieces, lands))]
    remote = []
    for k, peer in enumerate(peers):
        for a, (p, z) in enumerate(zip(pieces, lands)):
            sems = dict(send_sem=send_sems.at[7 * a + k], recv_sem=recv_sems.at[7 * a + k], device_id=peer,
                        device_id_type=MESH)
            remote.append((pltpu.make_async_remote_copy(src_ref=piece(p, slots[k]), dst_ref=z.at[me], **sems),
                           pltpu.make_async_remote_copy(src_ref=piece(p, slots[k]), dst_ref=z.at[slots[k]], **sems)))
    return local, remote


def _exchange_start(pieces, name, whole=False):
    n = len(pieces)

    def body(*refs):
        ins, lands = refs[:n], refs[n:2 * n]
        send_sems, recv_sems, local_sems = refs[2 * n:2 * n + 3]
        token_ref = refs[-1]
        local, remote = _exchange_copies(ins, lands, send_sems, recv_sems, local_sems, whole)
        for cp in local:
            cp.start()
        for cp, _ in remote:
            cp.start()
        token_ref[...] = jnp.zeros_like(token_ref)

    hbm = pl.BlockSpec(memory_space=pltpu.HBM)
    sem = pl.BlockSpec(memory_space=pltpu.SEMAPHORE)
    thru = [pltpu.HBM(p.shape, p.dtype) for p in pieces]
    zones = [pltpu.HBM(((N_DEV,) + p.shape) if whole else p.shape, p.dtype) for p in pieces]
    outs = pl.pallas_call(
        body,
        out_shape=(pltpu.SemaphoreType.DMA((7 * n,)), pltpu.SemaphoreType.DMA((7 * n,)), pltpu.SemaphoreType.DMA((n,)),
                   *thru, *zones, jax.ShapeDtypeStruct((8, 128), F32)),
        in_specs=[hbm] * (2 * n), out_specs=(sem, sem, sem, *([hbm] * (2 * n)), pl.BlockSpec(memory_space=pltpu.VMEM)),
        input_output_aliases={i: 3 + i for i in range(2 * n)},
        compiler_params=pltpu.CompilerParams(has_side_effects=pltpu.SideEffectType.DATAFLOW_SIDE_EFFECTING),
        name=name,
    )(*[pltpu.with_memory_space_constraint(p, pltpu.HBM) for p in pieces],
      *[pltpu.with_memory_space_constraint(lax.empty(z.shape, z.dtype), pltpu.HBM) for z in zones])
    return outs[:-1], outs[-1]


def _exchange_wait(handles, after, name, whole=False):
    send_sems, recv_sems, local_sems = handles[:3]
    n = (len(handles) - 3) // 2
    pieces, lands = handles[3:3 + n], handles[3 + n:]

    def body(*refs):
        ins, zones = refs[:n], refs[n:2 * n]
        s_sems, r_sems, l_sems = refs[2 * n:2 * n + 3]
        local, remote = _exchange_copies(ins, zones, s_sems, r_sems, l_sems, whole)
        for cp in local:
            cp.wait()
        for cp, landed in remote:
            cp.wait_send()
            landed.wait_recv()

    hbm = pl.BlockSpec(memory_space=pltpu.HBM)
    sem = pl.BlockSpec(memory_space=pltpu.SEMAPHORE)
    thru = [pltpu.HBM(p.shape, p.dtype) for p in list(pieces) + list(lands)]
    outs = pl.pallas_call(
        body, out_shape=tuple(thru),
        in_specs=[hbm] * (2 * n) + [sem, sem, sem, pl.BlockSpec(memory_space=pl.ANY)], out_specs=tuple([hbm] * (2 * n)),
        input_output_aliases={i: i for i in range(2 * n)},
        compiler_params=pltpu.CompilerParams(has_side_effects=pltpu.SideEffectType.DATAFLOW_SIDE_EFFECTING),
        name=name,
    )(*pieces, *lands, send_sems, recv_sems, local_sems, after)
    return outs[n:]


def _adamw(pieces, w, m, v, name, max_rows=256):
    n_l = len(pieces)
    _, r, cols = pieces[0].shape
    tr = max(rows for rows in range(16, min(max_rows, r) + 1, 16) if r % rows == 0)
    n_r = r // tr
    c1 = 1.0 / (1.0 - ADAM_B1 ** ADAM_STEP)
    c2 = 1.0 / (1.0 - ADAM_B2 ** ADAM_STEP)

    def body(*refs):
        p_refs = refs[:n_l]
        w_ref, m_ref, v_ref, g_ref, d_ref, nm_ref, nv_ref = refs[n_l:]
        for layer, p_ref in enumerate(p_refs):
            @pl.when(pl.program_id(0) == layer)
            def _(p_ref=p_ref):
                g = p_ref[0].astype(F32)
                for k in range(1, N_DEV):
                    g = g + p_ref[k].astype(F32)
                nm = ADAM_B1 * m_ref[...] + (1.0 - ADAM_B1) * g
                nv = ADAM_B2 * v_ref[...] + (1.0 - ADAM_B2) * (g * g)
                g_ref[...] = g
                nm_ref[...] = nm
                nv_ref[...] = nv
                d_ref[...] = -ADAM_LR * ((nm * c1) / (jnp.sqrt(nv * c2) + ADAM_EPS) + ADAM_WD * w_ref[...])

    def piece_spec(layer):
        return pl.BlockSpec((N_DEV, tr, cols), lambda l, i: (0, jnp.where(l == layer, i, 0), 0))

    tile = pl.BlockSpec((tr, cols), lambda l, i: (l * n_r + i, 0))
    out = jax.ShapeDtypeStruct((n_l * r, cols), F32)
    return pl.pallas_call(
        body, out_shape=(out, out, out, out), grid=(n_l, n_r),
        in_specs=[piece_spec(layer) for layer in range(n_l)] + [tile, tile, tile],
        out_specs=(tile, tile, tile, tile), name=name, compiler_params=_params(2))(*pieces, w, m, v)


def _rows_of(numel, row_tile):
    rows = -(-numel // LANES)
    return -(-rows // row_tile) * row_tile


def _pack(flat_list, row_tile, lead=()):
    parts = []
    for a in flat_list:
        numel = a.shape[-1]
        rows = _rows_of(numel, row_tile)
        pad = [(0, 0)] * len(lead) + [(0, rows * LANES - numel)]
        parts.append(jnp.pad(a, pad).reshape(*lead, rows, LANES))
    return jnp.concatenate(parts, axis=len(lead))


def _unpack(buf, shapes, row_tile, lead=()):
    out, r0 = [], 0
    for shape in shapes:
        numel = 1
        for s in shape:
            numel *= s
        rows = _rows_of(numel, row_tile)
        part = lax.slice_in_dim(buf, r0, r0 + rows, axis=len(lead)).reshape(*lead, rows * LANES)
        out.append(lax.slice_in_dim(part, 0, numel, axis=len(lead)).reshape(*lead, *shape))
        r0 += rows
    return out


def _to_shards(full, axis):
    shape = full.shape
    cut = full.reshape(shape[:axis] + (N_DEV, shape[axis] // N_DEV) + shape[axis + 1:])
    return jnp.moveaxis(cut, axis, 0)


def _step(x, target, w, m, v):
    t, d = x.shape[1], x.shape[2]
    x2 = x.reshape(t, d)
    tgt2 = target.reshape(t, d)

    small = _pack([w[k].reshape(-1) for k in GATHER_F32], 8)
    w_in_first, small_all = _all_gather([w["a_w_in"][:1].astype(BF16), small], "all_gather_first")
    gw = dict(zip(GATHER_F32, _unpack(small_all, [w[k].shape for k in GATHER_F32], 8, (N_DEV,))))
    mats = {("a_w_in", 0): w_in_first.reshape((N_DEV,) + w_in_first.shape[2:])}
    gather_groups = {
        "layer0": [("a_w_out", 0, 1), ("f_w_up", 0, 1), ("f_w_down", 0, 1)],
        "layer1": [("b_w_in", 0, 1), ("b_w_out", 0, 1), ("f_w_up", 1, 2), ("f_w_down", 1, 2)],
        "layer23": [("a_w_in", 1, 2), ("a_w_out", 1, 2), ("c_w_in", 0, 1), ("c_w_grp", 0, 1), ("c_w_out", 0, 1),
                    ("f_w_up", 2, 4), ("f_w_down", 2, 4)],
    }
    gather_handles = {}

    def start_gather(tag):
        gather_handles[tag], token = _exchange_start(
            [w[k][lo:hi].astype(BF16) for k, lo, hi in gather_groups[tag]], f"all_gather_{tag}_start", whole=True)
        return token[0:1, 0:1]

    gather_started = start_gather("layer0")

    def finish_gather(tag, after):
        lands = _exchange_wait(gather_handles[tag], after, f"all_gather_{tag}_wait", whole=True)
        for (k, lo, hi), land in zip(gather_groups[tag], lands):
            mats.update({(k, layer): land[:, layer - lo] for layer in range(lo, hi)})

    def mat(name, layer):
        return mats[(name, layer)]

    def full_cols(name, layer):
        a = gw[name][:, layer]
        if a.ndim == 2:
            return a.reshape(1, -1)
        return jnp.moveaxis(a, 0, 1).reshape(a.shape[1], -1)

    ones = jnp.ones((1, d), F32)
    zeros = jnp.zeros((1, d), F32)

    saved = []
    res, res_g, res_b = x2, ones, zeros
    xin = x2.astype(BF16)
    for i in range(DEPTH):
        kind, j = i % 3, i // 3
        sv = {"xin": xin, "kind": kind, "j": j}
        if i == 1:
            finish_gather("layer1", xin)
            gather_started = start_gather("layer23")
        if i == 2:
            finish_gather("layer23", xin)
        if kind == 0:
            w_in = mat("a_w_in", j)
            dw, dwb = full_cols("a_dw", j), full_cols("a_dw_b", j)
            if i == 0:
                dwb = dwb + gather_started
            lg, lb = full_cols("a_ln_g", j), full_cols("a_ln_b", j)
            h1 = _mm_nn(xin, w_in, BF16, f"conv_in_{i}")
            s_act, q = _conv_fwd(h1, dw, dwb, lg, lb, f"conv_mix_{i}")
            sv.update(h1=h1, q=q, w_in=w_in, dw=dw, lg=lg, lb=lb)
        elif kind == 1:
            w_in = mat("b_w_in", j)
            lg, lb = w["b_ln_g"][j].reshape(1, d) + gather_started, w["b_ln_b"][j].reshape(1, d)
            ws, bst = w["b_ws"][j], w["b_bs"][j].T
            h1 = _mm_nn(xin, w_in, BF16, f"sgu_in_{i}")
            s_act = _sgu_fwd(h1, lg, lb, ws, bst, f"sgu_mix_{i}")
            sv.update(h1=h1, w_in=w_in, lg=lg, lb=lb, ws=ws, bst=bst)
        else:
            w_in = mat("c_w_in", j).reshape(1, d, d)
            wgrp = jnp.moveaxis(mat("c_w_grp", j), 0, 1).reshape(4, d // 4, d // 4)
            scale = full_cols("c_scale", j)
            yp = _mm_nn(xin, w_in, F32, f"pool_in_{i}")[0]
            s_act = _pool_fwd(yp, wgrp, scale, f"pool_mix_{i}")
            sv.update(yp=yp, w_in=w_in, wgrp=wgrp, scale=scale)
        if i == 0:
            finish_gather("layer0", s_act)
            res_g = res_g + start_gather("layer1")
        w_out = mat(("a_w_out", "b_w_out", "c_w_out")[kind], j).reshape(1, d, d)
        g1, b1 = w["ln1_g"][i].reshape(1, d), w["ln1_b"][i].reshape(1, d)
        xhat1, y1, rstd1 = _mm_res_ln(s_act.reshape(1, t, d), w_out, res, res_g, res_b, g1, b1, f"mix_out_ln_{i}")
        w_up = mat("f_w_up", i)
        fdw = gw["f_dw"][:, i]
        n_ff = w_up.shape[-1]
        w_down = mat("f_w_down", i).reshape(4, n_ff, d)
        g2, b2 = w["ln2_g"][i].reshape(1, d), w["ln2_b"][i].reshape(1, d)
        u, h_ffn, a_act, xhat2, y2, rstd2 = _ffn_fwd(y1, w_up, fdw, w_down, xhat1, g1, b1, g2, b2, f"ffn_fwd_{i}")
        sv.update(s_act=s_act, w_out=w_out, xhat1=xhat1, y1=y1, rstd1=rstd1, g1=g1, u=u, h_ffn=h_ffn, a_act=a_act,
                  w_up=w_up,
                  fdw=fdw, w_down=w_down, xhat2=xhat2, rstd2=rstd2, g2=g2, b2=b2)
        saved.append(sv)
        res, res_g, res_b, xin = xhat2, g2, b2, y2

    last = saved[-1]
    dr2, dr2b, dg2, db2, sq = _loss_head(last["xhat2"], last["rstd2"], last["g2"], last["b2"], tgt2, "loss_head")
    loss = lax.psum((0.5 / d) * jnp.sum(sq), ("x", "y", "c"))

    grads = {k: [None] * w[k].shape[0] for k in WEIGHTS}
    grad_x = None
    small_early = ["b_ln_g", "b_ln_b", "b_ws", "b_bs", "c_scale"]
    small_late = [k for k in WEIGHTS if k not in GATHER_BF16 and k not in small_early]
    exchanges = []

    def small_pieces_of(names):
        flat = []
        for k in names:
            if k in REPLICATED:
                full = jnp.stack([gk.reshape(w[k].shape[1:]) for gk in grads[k]], axis=0)
                flat.append(jnp.broadcast_to(full.reshape(1, -1), (N_DEV, full.size)))
            else:
                flat.append(jnp.stack(grads[k], axis=1).reshape(N_DEV, -1))
        rows = _pack(flat, 8, (N_DEV,))
        pad_rows = -(-rows.shape[1] // 128) * 128 - rows.shape[1]
        return jnp.pad(rows, ((0, 0), (0, pad_rows), (0, 0))), pad_rows

    def start_exchange(keys, tag, extra=()):
        pieces = [grads[k][l].astype(BF16).reshape(N_DEV, -1, w[k].shape[-1]) for k, l in keys]
        handles, token = _exchange_start(pieces + list(extra), f"exchange_start_{tag}")
        exchanges.append((keys, handles, tag))
        return token[0:1, 0:1]

    for i in reversed(range(DEPTH)):
        sv = saved[i]
        kind, j = sv["kind"], sv["j"]
        grads["ln2_g"][i], grads["ln2_b"][i] = dg2, db2
        da = _mm_nt_out(dr2b, sv["w_down"], f"ffn_da_{i}")
        grads["f_w_down"][i] = _to_shards(_mm_tn(sv["a_act"], dr2b.reshape(1, t, d), f"ffn_dwdown_{i}")
                                          .reshape(-1, d), 0)
        du, dfdw = _ffn_gate_bwd(sv["h_ffn"], sv["u"], da, sv["fdw"], f"ffn_gate_bwd_{i}")
        grads["f_dw"][i] = dfdw
        grads["f_w_up"][i] = _mm_tn(sv["y1"].reshape(1, t, d), du, f"ffn_dwup_{i}")
        started = start_exchange([("f_w_up", i), ("f_w_down", i)], f"ffn_{i}")
        dr1, dr1b, dg1, db1 = _mm_nt_lnb(du, sv["w_up"], dr2, sv["xhat1"], sv["rstd1"], sv["g1"], f"ffn_dx_ln_{i}")
        grads["ln1_g"][i], grads["ln1_b"][i] = dg1, db1
        ds = _mm_nt_out(dr1b, sv["w_out"], f"mix_ds_{i}")[0]
        dw_out = _to_shards(_mm_tn(sv["s_act"].reshape(1, t, d), dr1b.reshape(1, t, d), f"mix_dwout_{i}")[0], 0)
        xin3 = sv["xin"].reshape(1, t, d)
        if kind == 0:
            dh1, ddw, ddwb, dlg, dlb = _conv_bwd(ds, sv["q"], sv["h1"], sv["dw"], sv["lg"] + started, sv["lb"],
                                                  f"conv_mix_bwd_{i}")
            grads["a_w_out"][j] = dw_out
            grads["a_dw"][j] = _to_shards(jnp.sum(ddw, axis=1), 1)
            grads["a_dw_b"][j] = _to_shards(ddwb[0], 0)
            grads["a_ln_g"][j] = _to_shards(dlg[0], 0)
            grads["a_ln_b"][j] = _to_shards(dlb[0], 0)
            grads["a_w_in"][j] = _mm_tn(xin3, dh1, f"conv_dwin_{i}")
            dh_in, w_in = dh1, sv["w_in"]
            mixer_keys = [("a_w_in", j), ("a_w_out", j)]
        elif kind == 1:
            dh1, dlg, dlb, dws, dbias = _sgu_bwd(sv["h1"], ds, sv["lg"] + started, sv["lb"], sv["ws"], sv["bst"],
                                                  f"sgu_mix_bwd_{i}")
            grads["b_w_out"][j] = dw_out
            grads["b_ln_g"][j], grads["b_ln_b"][j] = dlg[0], dlb[0]
            grads["b_ws"][j] = dws
            grads["b_bs"][j] = jnp.sum(dbias.reshape(CHUNK, HEADS, CHUNK), axis=-1).T
            grads["b_w_in"][j] = _mm_tn(xin3, dh1, f"sgu_dwin_{i}")
            dh_in, w_in = dh1, sv["w_in"]
            mixer_keys = [("b_w_in", j), ("b_w_out", j)]
        else:
            dyp, dscale, dwgrp = _pool_bwd(sv["yp"], ds, sv["wgrp"], sv["scale"] + started, f"pool_mix_bwd_{i}")
            grads["c_w_out"][j] = dw_out
            grads["c_scale"][j] = _to_shards(dscale[0], 0)
            grads["c_w_grp"][j] = _to_shards(dwgrp, 1)
            dh_in, w_in = dyp.reshape(1, t, d), sv["w_in"]
            grads["c_w_in"][j] = _to_shards(_mm_tn(xin3, dh_in, f"pool_dwin_{i}")[0], 0)
            mixer_keys = [("c_w_in", j), ("c_w_grp", j), ("c_w_out", j)]
        if i > 0:
            extra = [small_pieces_of(small_early)[0]] if i == 1 else []
            started = start_exchange(mixer_keys, f"mixer_{i}", extra=extra)
            prev = saved[i - 1]
            prev["fdw"] = prev["fdw"] + started
            dr2, dr2b, dg2, db2 = _mm_nt_lnb(dh_in, w_in, dr1, prev["xhat2"], prev["rstd2"], prev["g2"],
                                              f"mix_dx_ln_{i}")
        else:
            start_exchange(mixer_keys, f"mixer_{i}", extra=[small_pieces_of(small_late)[0]])
            grad_x = _mm_nt_res(dh_in, w_in, dr1, "mix_dx_0").reshape(x.shape)

    kinds = ("grad", "delta", "new_m", "new_v")
    received, result = {}, {}

    def finish_exchange(group, after):
        keys, handles, tag = group
        lands = _exchange_wait(handles, after, f"exchange_wait_{tag}")
        received.update(zip(keys, lands))
        return lands

    def update(k):
        cols = w[k].shape[-1]
        bufs = _adamw([received[(k, l)] for l in range(w[k].shape[0])], w[k].reshape(-1, cols),
                      m[k].reshape(-1, cols), v[k].reshape(-1, cols), f"adamw_{k}")
        result.update({(kind, k): buf.reshape(w[k].shape) for kind, buf in zip(kinds, bufs)})

    small_received = {}
    for group in exchanges[:-1]:
        lands = finish_exchange(group, grad_x)
        if len(lands) > len(group[0]):
            small_received["early"] = lands[-1]
    late = [k for k, _ in exchanges[-1][0]]
    for k in GATHER_BF16:
        if k not in late:
            update(k)
    small_received["late"] = finish_exchange(exchanges[-1], result[("new_v", "f_w_down")])[-1]
    for k in late:
        update(k)

    for tag, names in (("early", small_early), ("late", small_late)):
        def packed(tree):
            rows = _pack([tree[k].reshape(-1) for k in names], 8)
            return jnp.pad(rows, ((0, small_received[tag].shape[1] - rows.shape[0]), (0, 0)))

        bufs = _adamw([small_received[tag]], packed(w), packed(m), packed(v), f"adamw_small_{tag}")
        shapes = [w[k].shape for k in names]
        for kind, buf in zip(kinds, bufs):
            result.update({(kind, k): a for k, a in zip(names, _unpack(buf, shapes, 8))})
    outs = [result[(kind, k)] for kind in kinds for k in WEIGHTS]
    return (loss, grad_x, *outs)


def kernel(x, a_w_in, a_dw, a_dw_b, a_ln_g, a_ln_b, a_w_out, b_w_in, b_ln_g, b_ln_b, b_ws, b_bs, b_w_out, c_w_in, c_w_grp, c_scale, c_w_out, f_w_up, f_dw, f_w_down, ln1_g, ln1_b, ln2_g, ln2_b, loss_target, m_a_w_in, m_a_dw, m_a_dw_b, m_a_ln_g, m_a_ln_b, m_a_w_out, m_b_w_in, m_b_ln_g, m_b_ln_b, m_b_ws, m_b_bs, m_b_w_out, m_c_w_in, m_c_w_grp, m_c_scale, m_c_w_out, m_f_w_up, m_f_dw, m_f_w_down, m_ln1_g, m_ln1_b, m_ln2_g, m_ln2_b, v_a_w_in, v_a_dw, v_a_dw_b, v_a_ln_g, v_a_ln_b, v_a_w_out, v_b_w_in, v_b_ln_g, v_b_ln_b, v_b_ws, v_b_bs, v_b_w_out, v_c_w_in, v_c_w_grp, v_c_scale, v_c_w_out, v_f_w_up, v_f_dw, v_f_w_down, v_ln1_g, v_ln1_b, v_ln2_g, v_ln2_b):
    w = dict(zip(WEIGHTS, (a_w_in, a_dw, a_dw_b, a_ln_g, a_ln_b, a_w_out, b_w_in, b_ln_g, b_ln_b, b_ws, b_bs, b_w_out,
                           c_w_in, c_w_grp, c_scale, c_w_out, f_w_up, f_dw, f_w_down, ln1_g, ln1_b, ln2_g, ln2_b)))
    m = dict(zip(WEIGHTS, (m_a_w_in, m_a_dw, m_a_dw_b, m_a_ln_g, m_a_ln_b, m_a_w_out, m_b_w_in, m_b_ln_g, m_b_ln_b,
                           m_b_ws, m_b_bs, m_b_w_out, m_c_w_in, m_c_w_grp, m_c_scale, m_c_w_out, m_f_w_up, m_f_dw,
                           m_f_w_down, m_ln1_g, m_ln1_b, m_ln2_g, m_ln2_b)))
    v = dict(zip(WEIGHTS, (v_a_w_in, v_a_dw, v_a_dw_b, v_a_ln_g, v_a_ln_b, v_a_w_out, v_b_w_in, v_b_ln_g, v_b_ln_b,
                           v_b_ws, v_b_bs, v_b_w_out, v_c_w_in, v_c_w_grp, v_c_scale, v_c_w_out, v_f_w_up, v_f_dw,
                           v_f_w_down, v_ln1_g, v_ln1_b, v_ln2_g, v_ln2_b)))
    return _step(x, loss_target, w, m, v)
```
